```python
import math
import jax, jax.numpy as jnp
from jax import lax
import numpy as np

D_MODEL = 1024
BATCH = 16
SEQ = 256
DEPTH = 4
DEC_BATCH = 8
DEC_SEQ = 4096
PAST_LEN = 512

GRID_W = 64
N_MIXERS = 2
N_RET = (DEPTH + 1) // 2
N_ATTN = DEPTH // 2
RET_HEADS = 4
RET_DK = 256
RET_DV = 512
RET_CHUNK = 128
RET_IN = RET_HEADS * (2 * RET_DK + 2 * RET_DV)
ATT_HQ = 16
ATT_HKV = 4
ATT_G = ATT_HQ // ATT_HKV
ATT_DH = 64
WINDOW = 128
ATT_BLK = 128
ATT_IN = (ATT_HQ + 2 * ATT_HKV) * ATT_DH
ROPE_BASE = 10000.0
PEER_HEADS = 8
PEER_NKEYS = 128
PEER_EXPERTS = PEER_NKEYS * PEER_NKEYS
PEER_DQ = 256
PEER_DHALF = PEER_DQ // 2
PEER_TOPK = 16
PEER_CHUNK = 128
DEEPNORM_ALPHA = (2.0 * DEPTH) ** 0.25
DEEPNORM_BETA = (8.0 * DEPTH) ** -0.25
LN_EPS = 1e-5
GN_EPS = 1e-5
NEG_INF = -1e30

kernel_name = 'hybrid_retention_swa_peer_diffusion_step'


def layer_norm(x, g, b):
    xf = x.astype(jnp.float32)
    mu = jnp.mean(xf, -1, keepdims=True)
    var = jnp.mean(jnp.square(xf - mu), -1, keepdims=True)
    y = (xf - mu) * lax.rsqrt(var + LN_EPS) * g.astype(jnp.float32) + b.astype(jnp.float32)
    return y.astype(x.dtype)


def modulation(cond, w, b):
    m = jnp.einsum('nd,de->ne', jax.nn.silu(cond), w) + b
    return jnp.split(m[:, None, :], 6, axis=-1)


def rope_angles(pos, dim):
    inv = ROPE_BASE ** (-jnp.arange(0, dim, 2, dtype=jnp.float32) / dim)
    ang = pos.astype(jnp.float32)[:, None] * inv[None, :]
    return jnp.cos(ang), jnp.sin(ang)


def apply_rope(x, cos, sin):
    half = x.shape[-1] // 2
    x1, x2 = x[..., :half], x[..., half:]
    cs = cos[None, :, None, :].astype(x.dtype)
    sn = sin[None, :, None, :].astype(x.dtype)
    return jnp.concatenate([x1 * cs - x2 * sn, x1 * sn + x2 * cs], -1)


def axial_rope(x):
    L = x.shape[1]
    rows = L // GRID_W
    t = jnp.arange(rows * GRID_W)
    half = x.shape[-1] // 2
    cr, sr = rope_angles(t // GRID_W, half)
    cc, sc = rope_angles(t % GRID_W, half)
    return jnp.concatenate([apply_rope(x[..., :half], cr, sr), apply_rope(x[..., half:], cc, sc)], -1)


def chunk_retention(q, k, v, log_g, s0, strict):
    B, L, H, _ = q.shape
    dv = v.shape[-1]
    nc = L // RET_CHUNK

    def chunks(a):
        return a.reshape(B, nc, RET_CHUNK, H, a.shape[-1]).transpose(1, 0, 3, 2, 4)

    idx = jnp.arange(RET_CHUNK, dtype=jnp.float32)
    diff = idx[:, None] - idx[None, :]
    mask = (diff > 0) if strict else (diff >= 0)
    dmat = jnp.where(mask[None], jnp.exp(jnp.where(mask, diff, 0.0)[None] * log_g[:, None, None]), 0.0)
    q_dec = jnp.exp((idx + 1.0)[None, :] * log_g[:, None])
    k_dec = jnp.exp((RET_CHUNK - 1.0 - idx)[None, :] * log_g[:, None])
    c_dec = jnp.exp(RET_CHUNK * log_g)

    def step(S, qkv):
        qc, kc, vc = qkv
        att = jnp.einsum('bhid,bhjd->bhij', qc, kc) * dmat
        o = jnp.einsum('bhij,bhjv->bhiv', att, vc) + jnp.einsum('bhid,bhdv->bhiv', qc * q_dec[..., None], S)
        S = S * c_dec[:, None, None] + jnp.einsum('bhjd,bhjv->bhdv', kc * k_dec[..., None], vc)
        return S, o

    s_fin, o = lax.scan(step, s0, (chunks(q), chunks(k), chunks(v)))
    o = o.transpose(1, 0, 3, 2, 4).reshape(B, L, H, dv)
    return o, s_fin


def bidir_retention(q, k, v, log_gf, log_gb, s0f, s0b):
    of, sf = chunk_retention(q, k, v, log_gf, s0f, False)
    ob, sb = chunk_retention(q[:, ::-1], k[:, ::-1], v[:, ::-1], log_gb, s0b, True)
    return of + ob[:, ::-1], sf, sb


def retention_mixer(h, s0f, s0b, w_in, w_out, decay, latent):
    B, L, _ = h.shape
    z = h @ w_in
    q, k, v, g = jnp.split(z, [RET_HEADS * RET_DK, 2 * RET_HEADS * RET_DK,
                               2 * RET_HEADS * RET_DK + RET_HEADS * RET_DV], -1)
    q = q.reshape(B, L, RET_HEADS, RET_DK)
    k = k.reshape(B, L, RET_HEADS, RET_DK) * (RET_DK ** -0.5)
    v = v.reshape(B, L, RET_HEADS, RET_DV)
    if latent:
        cos, sin = rope_angles(jnp.arange(L), RET_DK)
        q = apply_rope(q, cos, sin)
        k = apply_rope(k, cos, sin)
    log_g = jax.nn.log_sigmoid(decay.astype(jnp.float32))
    f32 = jnp.float32
    o, sf, sb = bidir_retention(q.astype(f32), k.astype(f32), v.astype(f32), log_g[0], log_g[1],
                                s0f.astype(f32), s0b.astype(f32))
    mu = jnp.mean(o, -1, keepdims=True)
    var = jnp.mean(jnp.square(o - mu), -1, keepdims=True)
    o = ((o - mu) * lax.rsqrt(var + GN_EPS)).reshape(B, L, RET_HEADS * RET_DV).astype(h.dtype)
    y = (jax.nn.silu(g) * o) @ w_out
    return y, sf, sb


def attn_project(h, w_in):
    B, L, _ = h.shape
    z = h @ w_in
    q, k, v = jnp.split(z, [ATT_HQ * ATT_DH, (ATT_HQ + ATT_HKV) * ATT_DH], -1)
    return (q.reshape(B, L, ATT_HQ, ATT_DH), k.reshape(B, L, ATT_HKV, ATT_DH),
            v.reshape(B, L, ATT_HKV, ATT_DH))


def attn_context(h, w_in, w_out, sink):
    B, L, _ = h.shape
    q, k, v = attn_project(h, w_in)
    qg = q.reshape(B, L, ATT_HKV, ATT_G, ATT_DH)
    s = jnp.einsum('bqhgd,bkhd->bhgqk', qg, k).astype(jnp.float32) * (ATT_DH ** -0.5)
    s_sink = jnp.broadcast_to(sink.reshape(ATT_HKV, ATT_G, 1, 1).astype(jnp.float32), s.shape[:-1] + (1,))
    p = jax.nn.softmax(jnp.concatenate([s, s_sink], -1), -1)[..., :L].astype(v.dtype)
    o = jnp.einsum('bhgqk,bkhd->bqhgd', p, v).reshape(B, L, ATT_HQ * ATT_DH)
    return o @ w_out, k, v


def attn_latent(h, ck, cv, w_in, w_out, sink):
    B, L, _ = h.shape
    Lc = ck.shape[1]
    nb = L // ATT_BLK
    q, k, v = attn_project(h, w_in)
    q = axial_rope(q)
    k = axial_rope(k)
    qg = q.reshape(B, L, ATT_HKV, ATT_G, ATT_DH)
    pad = ((0, 0), (ATT_BLK, ATT_BLK), (0, 0), (0, 0))
    kp = jnp.pad(k, pad)
    vp = jnp.pad(v, pad)
    ck = ck.astype(k.dtype)
    cv = cv.astype(v.dtype)
    i = jnp.arange(ATT_BLK)[:, None]
    j = jnp.arange(3 * ATT_BLK)[None, :]
    rel = j - ATT_BLK - i
    sink_f = sink.reshape(ATT_HKV, ATT_G, 1, 1).astype(jnp.float32)
    scale = ATT_DH ** -0.5

    def block(b):
        start = b * ATT_BLK
        qb = lax.dynamic_slice_in_dim(qg, start, ATT_BLK, axis=1)
        kb = lax.dynamic_slice_in_dim(kp, start, 3 * ATT_BLK, axis=1)
        vb = lax.dynamic_slice_in_dim(vp, start, 3 * ATT_BLK, axis=1)
        kpos = start - ATT_BLK + j
        valid = (jnp.abs(rel) <= WINDOW) & (kpos >= 0) & (kpos < L)
        s_loc = jnp.einsum('bqhgd,bkhd->bhgqk', qb, kb).astype(jnp.float32) * scale
        s_loc = jnp.where(valid, s_loc, NEG_INF)
        s_ctx = jnp.einsum('bqhgd,bchd->bhgqc', qb, ck).astype(jnp.float32) * scale
        s_sink = jnp.broadcast_to(sink_f, s_loc.shape[:-1] + (1,))
        p = jax.nn.softmax(jnp.concatenate([s_loc, s_ctx, s_sink], -1), -1).astype(v.dtype)
        return (jnp.einsum('bhgqk,bkhd->bqhgd', p[..., :3 * ATT_BLK], vb)
                + jnp.einsum('bhgqc,bchd->bqhgd', p[..., 3 * ATT_BLK:3 * ATT_BLK + Lc], cv))

    o = lax.map(block, jnp.arange(nb))
    o = o.transpose(1, 0, 2, 3, 4, 5).reshape(B, L, ATT_HQ * ATT_DH)
    return o @ w_out


def peer_ffn(h, wq, keys, u, v):
    B, L, D = h.shape
    xs = h.reshape((B * L) // PEER_CHUNK, PEER_CHUNK, D)

    def chunk(xc):
        q = (xc @ wq).reshape(PEER_CHUNK, PEER_HEADS, 2, PEER_DHALF)
        s = jnp.einsum('tphd,phnd->tphn', q, keys).astype(jnp.float32)
        sv, si = lax.top_k(s, PEER_TOPK)
        comb = (sv[:, :, 0, :, None] + sv[:, :, 1, None, :]).reshape(PEER_CHUNK, PEER_HEADS, PEER_TOPK * PEER_TOPK)
        cs, ci = lax.top_k(comb, PEER_TOPK)
        i1 = jnp.take_along_axis(si[:, :, 0], ci // PEER_TOPK, -1)
        i2 = jnp.take_along_axis(si[:, :, 1], ci % PEER_TOPK, -1)
        e = i1 * PEER_NKEYS + i2
        w = jax.nn.softmax(cs, -1)
        a = jax.nn.gelu(jnp.einsum('tpkd,td->tpk', u[e], xc))
        coef = (w * a.astype(jnp.float32)).astype(xc.dtype)
        return jnp.einsum('tpk,tpkd->td', coef, v[e])

    return lax.map(chunk, xs).reshape(B, L, D)


def setup_inputs(seed: int = 0) -> dict:
    key = jax.random.key(seed)
    ks = jax.random.split(key, 22)
    f32 = jnp.float32
    D = D_MODEL

    def nrm(k, shape, scale):
        return jax.random.normal(k, shape, f32) * scale

    gamma0 = 1.0 - 2.0 ** (-5.0 - jnp.arange(RET_HEADS, dtype=f32))
    logit0 = jnp.log(gamma0) - jnp.log1p(-gamma0)
    return {
        'x_prompt': nrm(ks[0], (BATCH, SEQ, D), 1.0),
        'x_sample': nrm(ks[1], (DEC_BATCH, DEC_SEQ, D), 1.0),
        'state_ret_fwd': nrm(ks[2], (DEC_BATCH, N_RET, RET_HEADS, RET_DK, RET_DV), 0.5),
        'state_ret_bwd': nrm(ks[3], (DEC_BATCH, N_RET, RET_HEADS, RET_DK, RET_DV), 0.5),
        'cache_k': nrm(ks[4], (DEC_BATCH, N_ATTN, PAST_LEN, ATT_HKV, ATT_DH), 1.0),
        'cache_v': nrm(ks[5], (DEC_BATCH, N_ATTN, PAST_LEN, ATT_HKV, ATT_DH), 1.0),
        'c': nrm(ks[6], (DEC_BATCH, D), 1.0),
        'c_ctx': nrm(ks[7], (D,), 1.0),
        'mod_w': nrm(ks[8], (DEPTH, D, 6 * D), 0.5 * D ** -0.5),
        'mod_b': nrm(ks[9], (DEPTH, 6 * D), 0.02),
        'ln_g': 1.0 + nrm(ks[10], (DEPTH, 2, D), 0.02),
        'ln_b': nrm(ks[11], (DEPTH, 2, D), 0.02),
        'ret_w_in': nrm(ks[12], (N_RET, D, RET_IN), D ** -0.5),
        'ret_w_out': nrm(ks[13], (N_RET, RET_HEADS * RET_DV, D), DEEPNORM_BETA * (RET_HEADS * RET_DV) ** -0.5),
        'ret_decay': logit0 + nrm(ks[14], (N_RET, 2, RET_HEADS), 0.1),
        'attn_w_in': nrm(ks[15], (N_ATTN, D, ATT_IN), D ** -0.5),
        'attn_w_out': nrm(ks[16], (N_ATTN, ATT_HQ * ATT_DH, D), DEEPNORM_BETA * (ATT_HQ * ATT_DH) ** -0.5),
        'attn_sink': nrm(ks[17], (N_ATTN, ATT_HQ), 0.5),
        'peer_wq': nrm(ks[18], (DEPTH, D, PEER_HEADS * PEER_DQ), D ** -0.5),
        'peer_keys': nrm(ks[19], (DEPTH, PEER_HEADS, 2, PEER_NKEYS, PEER_DHALF), PEER_DHALF ** -0.5),
        'peer_u': nrm(ks[20], (DEPTH, PEER_EXPERTS, D), D ** -0.5),
        'peer_v': nrm(ks[21], (DEPTH, PEER_EXPERTS, D), DEEPNORM_BETA * PEER_HEADS ** -0.5),
    }


def reference(x_prompt, x_sample, state_ret_fwd, state_ret_bwd, cache_k, cache_v, c, c_ctx,
              mod_w, mod_b, ln_g, ln_b, ret_w_in, ret_w_out, ret_decay, attn_w_in, attn_w_out, attn_sink,
              peer_wq, peer_keys, peer_u, peer_v):
    xp, xs = x_prompt, x_sample
    B = xp.shape[0]
    new_sf, new_sb, new_k, new_v = [], [], [], []
    for i in range(DEPTH):
        j = i // N_MIXERS
        shp1, scp1, gp1, shp2, scp2, gp2 = modulation(c_ctx[None, :], mod_w[i], mod_b[i])
        shs1, scs1, gs1, shs2, scs2, gs2 = modulation(c, mod_w[i], mod_b[i])
        hp = xp * (1.0 + scp1) + shp1
        hs = xs * (1.0 + scs1) + shs1
        if i % N_MIXERS == 0:
            z0 = jnp.zeros((B, RET_HEADS, RET_DK, RET_DV), jnp.float32)
            op, sf, sb = retention_mixer(hp, z0, z0, ret_w_in[j], ret_w_out[j], ret_decay[j], False)
            os_, _, _ = retention_mixer(hs, state_ret_fwd[:, j], state_ret_bwd[:, j],
                                        ret_w_in[j], ret_w_out[j], ret_decay[j], True)
            new_sf.append(sf)
            new_sb.append(sb)
        else:
            op, kc, vc = attn_context(hp, attn_w_in[j], attn_w_out[j], attn_sink[j])
            os_ = attn_latent(hs, cache_k[:, j], cache_v[:, j], attn_w_in[j], attn_w_out[j], attn_sink[j])
            new_k.append(kc)
            new_v.append(vc)
        xp = layer_norm(DEEPNORM_ALPHA * xp + gp1 * op, ln_g[i, 0], ln_b[i, 0])
        xs = layer_norm(DEEPNORM_ALPHA * xs + gs1 * os_, ln_g[i, 0], ln_b[i, 0])
        hp = xp * (1.0 + scp2) + shp2
        hs = xs * (1.0 + scs2) + shs2
        fp = peer_ffn(hp, peer_wq[i], peer_keys[i], peer_u[i], peer_v[i])
        fs = peer_ffn(hs, peer_wq[i], peer_keys[i], peer_u[i], peer_v[i])
        xp = layer_norm(DEEPNORM_ALPHA * xp + gp2 * fp, ln_g[i, 1], ln_b[i, 1])
        xs = layer_norm(DEEPNORM_ALPHA * xs + gs2 * fs, ln_g[i, 1], ln_b[i, 1])
    return (xp, xs, jnp.stack(new_sf, 1), jnp.stack(new_sb, 1), jnp.stack(new_k, 1), jnp.stack(new_v, 1))
```

```python
import functools
import math

import jax
import jax.numpy as jnp
from jax import lax
from jax.experimental import pallas as pl
from jax.experimental.pallas import tpu as pltpu

F32 = jnp.float32
BF16 = jnp.bfloat16

DEPTH = 4
N_MIXERS = 2
GRID_W = 64
RET_HEADS = 4
RET_DK = 256
RET_DV = 512
RET_CHUNK = 128
ATT_HQ = 16
ATT_HKV = 4
ATT_G = ATT_HQ // ATT_HKV
ATT_DH = 64
ATT_BLK = 128
ROPE_BASE = 10000.0
PEER_HEADS = 8
PEER_NKEYS = 128
PEER_DHALF = 128
PEER_TOPK = 16
DEEPNORM_ALPHA = (2.0 * DEPTH) ** 0.25
LN_EPS = 1e-5
GN_EPS = 1e-5
NEG_INF = -1e30

LANES = 128
VMEM_LIMIT = 56 * 1024 * 1024
PROJ_TM = 512
OUT_TM = 256
PEER_T = 512
PEER_NI1 = 4


def _dot(a, b):
    return jnp.dot(a, b, preferred_element_type=F32)


def _dot_nt(a, b):
    return lax.dot_general(a, b, (((1,), (1,)), ((), ())), preferred_element_type=F32)


def _params(sem):
    return pltpu.CompilerParams(dimension_semantics=sem, vmem_limit_bytes=VMEM_LIMIT)


def _group_index(blk, n_prompt_blocks, blocks_per_seq):
    return jnp.where(blk < n_prompt_blocks, 0, 1 + (blk - n_prompt_blocks) // blocks_per_seq)


def _layer_norm(r, g, b):
    mu = jnp.mean(r, -1, keepdims=True)
    d = r - mu
    var = jnp.mean(d * d, -1, keepdims=True)
    return d * lax.rsqrt(var + LN_EPS) * g + b


def _mod_kernel(c_ref, w_ref, b_ref, o_ref):
    c = c_ref[...]
    a = (c * jax.nn.sigmoid(c)).astype(BF16)
    o_ref[...] = _dot(a, w_ref[...].astype(BF16)) + b_ref[...]


def _modulation(cond, mod_w, mod_b):
    ngp, d = cond.shape
    n_out = mod_w.shape[-1]
    tn = 1536
    return pl.pallas_call(
        _mod_kernel,
        grid=(DEPTH, n_out // tn),
        in_specs=[
            pl.BlockSpec((ngp, d), lambda i, n: (0, 0)),
            pl.BlockSpec((None, d, tn), lambda i, n: (i, 0, n)),
            pl.BlockSpec((None, 1, tn), lambda i, n: (i, 0, n)),
        ],
        out_specs=pl.BlockSpec((None, ngp, tn), lambda i, n: (i, 0, n)),
        out_shape=jax.ShapeDtypeStruct((DEPTH, ngp, n_out), F32),
        compiler_params=_params(("parallel", "parallel")),
        name="modulation",
    )(cond, mod_w, mod_b.reshape(DEPTH, 1, n_out))


def _proj_kernel(x_ref, m_ref, w_ref, o_ref, *, off):
    sh = m_ref[off:off + 1, :]
    sc = m_ref[off + 1:off + 2, :]
    h = (x_ref[...] * (1.0 + sc) + sh).astype(BF16)
    o_ref[...] = _dot(h, w_ref[...])


def _project(x, mod, w_bf, layer, off, grp):
    nt, d = x.shape
    n = w_bf.shape[1]
    tn = 1536
    tm = PROJ_TM
    return pl.pallas_call(
        functools.partial(_proj_kernel, off=off),
        grid=(n // tn, nt // tm),
        in_specs=[
            pl.BlockSpec((tm, d), lambda j, b: (b, 0)),
            pl.BlockSpec((None, None, 6, d), lambda j, b: (layer, grp(b, tm), 0, 0)),
            pl.BlockSpec((d, tn), lambda j, b: (0, j)),
        ],
        out_specs=pl.BlockSpec((tm, tn), lambda j, b: (b, j)),
        out_shape=jax.ShapeDtypeStruct((nt, n), F32),
        compiler_params=_params(("parallel", "parallel")),
        name="mod_proj",
    )(x, mod, w_bf)


def _rope_half(x, cos, sin):
    x1 = x[:, :LANES]
    x2 = x[:, LANES:]
    return jnp.concatenate([x1 * cos - x2 * sin, x1 * sin + x2 * cos], axis=1)


def _ret_kernel(*refs, latent, nc):
    it = iter(refs)
    qa, ka, va, qb, kb, vb = (next(it) for _ in range(6))
    if latent:
        cosa, sina, cosb, sinb = (next(it) for _ in range(4))
    dfb, qdf, qdb, kdf, kdb, cdf, cdb = (next(it) for _ in range(7))
    if latent:
        s0f, s0b = next(it), next(it)
    o_ref = next(it)
    if not latent:
        sf_out, sb_out = next(it), next(it)
    sf, sb = next(it), next(it)

    c = pl.program_id(2)

    @pl.when(c == 0)
    def _init():
        o_ref[...] = jnp.zeros(o_ref.shape, F32)
        if latent:
            sf[...] = s0f[...]
            sb[...] = s0b[...]
        else:
            sf[...] = jnp.zeros(sf.shape, F32)
            sb[...] = jnp.zeros(sb.shape, F32)

    def prep(q_ref, k_ref, cos_ref, sin_ref):
        q = q_ref[...]
        k = k_ref[...] * (RET_DK ** -0.5)
        if latent:
            cos = cos_ref[...]
            sin = sin_ref[...]
            q = _rope_half(q, cos, sin)
            k = _rope_half(k, cos, sin)
        return q, k

    q, k = prep(qa, ka, cosa if latent else None, sina if latent else None)
    v = va[...].astype(BF16)
    att = (_dot_nt(q.astype(BF16), k.astype(BF16)) * dfb[...]).astype(BF16)
    o_c = _dot(att, v) + _dot((q * qdf[...]).astype(BF16), sf[...].astype(BF16))
    rows_c = pl.ds(pl.multiple_of(c * RET_CHUNK, RET_CHUNK), RET_CHUNK)
    o_ref[rows_c, :] += o_c
    kd_t = (k * kdf[...]).T.astype(BF16)
    sf[...] = sf[...] * cdf[...] + _dot(kd_t, v)

    q2, k2 = prep(qb, kb, cosb if latent else None, sinb if latent else None)
    v2 = vb[...].astype(BF16)
    o_b = _dot((q2 * qdb[...]).astype(BF16), sb[...].astype(BF16))
    rows_b = pl.ds(pl.multiple_of((nc - 1 - c) * RET_CHUNK, RET_CHUNK), RET_CHUNK)
    o_ref[rows_b, :] += o_b
    kd2_t = (k2 * kdb[...]).T.astype(BF16)
    sb[...] = sb[...] * cdb[...] + _dot(kd2_t, v2)

    if not latent:
        @pl.when(c == nc - 1)
        def _fin():
            sf_out[...] = sf[...]
            sb_out[...] = sb[...]


def _ret_tables(decay):
    log_g = jax.nn.log_sigmoid(decay.astype(F32))
    lf, lb = log_g[0], log_g[1]
    idx = jnp.arange(RET_CHUNK, dtype=F32)
    diff = idx[:, None] - idx[None, :]
    fmask = diff >= 0
    bmask = diff < 0
    df = jnp.where(fmask[None], jnp.exp(jnp.where(fmask, diff, 0.0)[None] * lf[:, None, None]), 0.0)
    db = jnp.where(bmask[None], jnp.exp(jnp.where(bmask, -diff, 0.0)[None] * lb[:, None, None]), 0.0)
    dfb = df + db
    def rows(e):
        return jnp.broadcast_to(e[:, :, None], (RET_HEADS, RET_CHUNK, RET_DK))
    qdf = rows(jnp.exp((idx + 1.0)[None, :] * lf[:, None]))
    kdf = rows(jnp.exp((RET_CHUNK - 1.0 - idx)[None, :] * lf[:, None]))
    qdb = rows(jnp.exp((RET_CHUNK - idx)[None, :] * lb[:, None]))
    kdb = rows(jnp.exp(idx[None, :] * lb[:, None]))
    cdf = jnp.broadcast_to(jnp.exp(RET_CHUNK * lf)[:, None, None], (RET_HEADS, 1, RET_DV))
    cdb = jnp.broadcast_to(jnp.exp(RET_CHUNK * lb)[:, None, None], (RET_HEADS, 1, RET_DV))
    return dfb, qdf, qdb, kdf, kdb, cdf, cdb


def _retention(z, tables, row0, nseq, seq_len, latent, rope=None, states=None, j=0):
    nc = seq_len // RET_CHUNK
    base = row0 // RET_CHUNK
    h_ = RET_HEADS
    kcol = (h_ * RET_DK) // RET_DK
    vcol = (2 * h_ * RET_DK) // RET_DV

    def fw(b, h, c):
        return base + b * nc + c

    def bw(b, h, c):
        return base + b * nc + (nc - 1 - c)

    in_specs = [
        pl.BlockSpec((RET_CHUNK, RET_DK), lambda b, h, c: (fw(b, h, c), h)),
        pl.BlockSpec((RET_CHUNK, RET_DK), lambda b, h, c: (fw(b, h, c), kcol + h)),
        pl.BlockSpec((RET_CHUNK, RET_DV), lambda b, h, c: (fw(b, h, c), vcol + h)),
        pl.BlockSpec((RET_CHUNK, RET_DK), lambda b, h, c: (bw(b, h, c), h)),
        pl.BlockSpec((RET_CHUNK, RET_DK), lambda b, h, c: (bw(b, h, c), kcol + h)),
        pl.BlockSpec((RET_CHUNK, RET_DV), lambda b, h, c: (bw(b, h, c), vcol + h)),
    ]
    args = [z, z, z, z, z, z]
    if latent:
        cos, sin = rope
        in_specs += [
            pl.BlockSpec((RET_CHUNK, LANES), lambda b, h, c: (c, 0)),
            pl.BlockSpec((RET_CHUNK, LANES), lambda b, h, c: (c, 0)),
            pl.BlockSpec((RET_CHUNK, LANES), lambda b, h, c: (nc - 1 - c, 0)),
            pl.BlockSpec((RET_CHUNK, LANES), lambda b, h, c: (nc - 1 - c, 0)),
        ]
        args += [cos, sin, cos, sin]
    dfb, qdf, qdb, kdf, kdb, cdf, cdb = tables
    in_specs += [pl.BlockSpec((None, RET_CHUNK, RET_CHUNK), lambda b, h, c: (h, 0, 0))]
    in_specs += [pl.BlockSpec((None, RET_CHUNK, RET_DK), lambda b, h, c: (h, 0, 0))] * 4
    in_specs += [pl.BlockSpec((None, 1, RET_DV), lambda b, h, c: (h, 0, 0))] * 2
    args += [dfb, qdf, qdb, kdf, kdb, cdf, cdb]
    if latent:
        s0f, s0b = states
        st_spec = pl.BlockSpec((None, None, None, RET_DK, RET_DV), lambda b, h, c: (b, j, h, 0, 0))
        in_specs += [st_spec, st_spec]
        args += [s0f, s0b]

    o_shape = jax.ShapeDtypeStruct((nseq * seq_len, h_ * RET_DV), F32)
    o_spec = pl.BlockSpec((seq_len, RET_DV), lambda b, h, c: (b, h))
    if latent:
        out_shape, out_specs = o_shape, o_spec
    else:
        s_shape = jax.ShapeDtypeStruct((nseq, h_, RET_DK, RET_DV), F32)
        s_spec = pl.BlockSpec((None, None, RET_DK, RET_DV), lambda b, h, c: (b, h, 0, 0))
        out_shape, out_specs = (o_shape, s_shape, s_shape), (o_spec, s_spec, s_spec)

    return pl.pallas_call(
        functools.partial(_ret_kernel, latent=latent, nc=nc),
        grid=(nseq, h_, nc),
        in_specs=in_specs,
        out_specs=out_specs,
        out_shape=out_shape,
        scratch_shapes=[pltpu.VMEM((RET_DK, RET_DV), F32), pltpu.VMEM((RET_DK, RET_DV), F32)],
        compiler_params=_params(("parallel", "parallel", "arbitrary")),
        name="retention_latent" if latent else "retention_context",
    )(*args)


def _ret_out_kernel(o_ref, g_ref, x_ref, m_ref, w_ref, lg_ref, lb_ref, out_ref):
    y = None
    for h in range(RET_HEADS):
        cols = slice(h * RET_DV, (h + 1) * RET_DV)
        oh = o_ref[:, cols]
        mu = jnp.mean(oh, -1, keepdims=True)
        d = oh - mu
        var = jnp.mean(d * d, -1, keepdims=True)
        on = d * lax.rsqrt(var + GN_EPS)
        gh = g_ref[:, cols]
        act = (gh * jax.nn.sigmoid(gh) * on).astype(BF16)
        part = _dot(act, w_ref[cols, :])
        y = part if y is None else y + part
    gate = m_ref[2:3, :]
    r = DEEPNORM_ALPHA * x_ref[...] + gate * y
    out_ref[...] = _layer_norm(r, lg_ref[...], lb_ref[...])


def _ret_out(o, z, x, mod, w_bf, ln_g, ln_b, layer, grp):
    nt, d = x.shape
    tm = OUT_TM
    hv = RET_HEADS * RET_DV
    gcol = (2 * RET_HEADS * RET_DK + hv) // hv
    return pl.pallas_call(
        _ret_out_kernel,
        grid=(nt // tm,),
        in_specs=[
            pl.BlockSpec((tm, hv), lambda b: (b, 0)),
            pl.BlockSpec((tm, hv), lambda b: (b, gcol)),
            pl.BlockSpec((tm, d), lambda b: (b, 0)),
            pl.BlockSpec((None, None, 6, d), lambda b: (layer, grp(b, tm), 0, 0)),
            pl.BlockSpec((hv, d), lambda b: (0, 0)),
            pl.BlockSpec((None, 1, d), lambda b: (2 * layer, 0, 0)),
            pl.BlockSpec((None, 1, d), lambda b: (2 * layer, 0, 0)),
        ],
        out_specs=pl.BlockSpec((tm, d), lambda b: (b, 0)),
        out_shape=jax.ShapeDtypeStruct((nt, d), F32),
        compiler_params=_params(("parallel",)),
        name="retention_out",
    )(o, z, x, mod, w_bf, ln_g, ln_b)


def _softmax_pv(parts, sink):
    m = sink
    for s, _ in parts:
        m = jnp.maximum(m, jnp.max(s, -1, keepdims=True))
    den = jnp.exp(sink - m)
    o = None
    for s, v in parts:
        p = jnp.exp(s - m)
        den = den + jnp.sum(p, -1, keepdims=True)
        pv = _dot(p.astype(BF16), v)
        o = pv if o is None else o + pv
    return o / den


def _attn_ctx_kernel(sink_ref, q_ref, k_ref, v_ref, o_ref):
    scale = ATT_DH ** -0.5
    for h in range(ATT_HKV):
        kh = k_ref[:, h * ATT_DH:(h + 1) * ATT_DH].astype(BF16)
        vh = v_ref[:, h * ATT_DH:(h + 1) * ATT_DH].astype(BF16)
        for g in range(ATT_G):
            hq = h * ATT_G + g
            cols = slice(hq * ATT_DH, (hq + 1) * ATT_DH)
            qh = q_ref[:, cols].astype(BF16)
            s = _dot_nt(qh, kh) * scale
            o_ref[:, cols] = _softmax_pv([(s, vh)], sink_ref[hq])


def _attn_context(z, sink, nseq, seq_len):
    hq_w = ATT_HQ * ATT_DH
    kv_w = ATT_HKV * ATT_DH
    return pl.pallas_call(
        _attn_ctx_kernel,
        grid=(nseq,),
        in_specs=[
            pl.BlockSpec(memory_space=pltpu.SMEM),
            pl.BlockSpec((seq_len, hq_w), lambda b: (b, 0)),
            pl.BlockSpec((seq_len, kv_w), lambda b: (b, hq_w // kv_w)),
            pl.BlockSpec((seq_len, kv_w), lambda b: (b, hq_w // kv_w + 1)),
        ],
        out_specs=pl.BlockSpec((seq_len, hq_w), lambda b: (b, 0)),
        out_shape=jax.ShapeDtypeStruct((nseq * seq_len, hq_w), F32),
        compiler_params=_params(("parallel",)),
        name="attention_context",
    )(sink, z, z, z)


def _rope_axial(x, cos, sin):
    w = x.shape[1]
    lane = lax.broadcasted_iota(jnp.int32, (x.shape[0], LANES), 1)
    first = (lane % 32) < 16
    outs = []
    for cg in range(w // LANES):
        xg = x[:, cg * LANES:(cg + 1) * LANES]
        up = pltpu.roll(xg, LANES - 16, 1)
        dn = pltpu.roll(xg, 16, 1)
        outs.append(xg * cos + jnp.where(first, up, dn) * sin)
    return jnp.concatenate(outs, axis=1) if len(outs) > 1 else outs[0]


def _attn_lat_kernel(sink_ref, q_ref, kp_ref, kc_ref, kn_ref, vp_ref, vc_ref, vn_ref, ck_ref, cv_ref,
                     cq_ref, sq_ref, cp_ref, sp_ref, cn_ref, sn_ref, o_ref, *, nb):
    scale = ATT_DH ** -0.5
    qb = pl.program_id(1)
    q = _rope_axial(q_ref[...], cq_ref[...], sq_ref[...])
    kw = jnp.concatenate([
        _rope_axial(kp_ref[...], cp_ref[...], sp_ref[...]),
        _rope_axial(kc_ref[...], cq_ref[...], sq_ref[...]),
        _rope_axial(kn_ref[...], cn_ref[...], sn_ref[...]),
    ], axis=0)
    vw = jnp.concatenate([vp_ref[...], vc_ref[...], vn_ref[...]], axis=0)
    ck = ck_ref[...]
    cv = cv_ref[...]

    i = lax.broadcasted_iota(jnp.int32, (ATT_BLK, 3 * ATT_BLK), 0)
    jj = lax.broadcasted_iota(jnp.int32, (ATT_BLK, 3 * ATT_BLK), 1)
    lo = jnp.where(qb > 0, 0, ATT_BLK)
    hi = jnp.where(qb < nb - 1, 3 * ATT_BLK, 2 * ATT_BLK)
    valid = (jj >= jnp.maximum(i, lo)) & (jj < jnp.minimum(i + 2 * ATT_BLK + 1, hi))

    for h in range(ATT_HKV):
        hs = slice(h * ATT_DH, (h + 1) * ATT_DH)
        kh = kw[:, hs].astype(BF16)
        vh = vw[:, hs].astype(BF16)
        ckh = ck[:, hs].astype(BF16)
        cvh = cv[:, hs].astype(BF16)
        for g in range(ATT_G):
            hq = h * ATT_G + g
            cols = slice(hq * ATT_DH, (hq + 1) * ATT_DH)
            qh = q[:, cols].astype(BF16)
            s_loc = jnp.where(valid, _dot_nt(qh, kh) * scale, NEG_INF)
            s_ctx = _dot_nt(qh, ckh) * scale
            o_ref[:, cols] = _softmax_pv([(s_loc, vh), (s_ctx, cvh)], sink_ref[hq])


def _attn_latent(z, sink, cache_k, cache_v, rope, row0, nseq, seq_len, j):
    nb = seq_len // ATT_BLK
    base = row0 // ATT_BLK
    hq_w = ATT_HQ * ATT_DH
    kv_w = ATT_HKV * ATT_DH
    kcol = hq_w // kv_w
    past = cache_k.shape[2]
    cos, sin = rope

    def prv(q):
        return jnp.maximum(q - 1, 0)

    def nxt(q):
        return jnp.minimum(q + 1, nb - 1)

    def kv_spec(sel, col):
        return pl.BlockSpec((ATT_BLK, kv_w), lambda b, q: (base + b * nb + sel(q), col))

    def rope_spec(sel):
        return pl.BlockSpec((ATT_BLK, LANES), lambda b, q: (sel(q), 0))

    same = lambda q: q
    cache_spec = pl.BlockSpec((None, None, past, kv_w), lambda b, q: (b, j, 0, 0))
    return pl.pallas_call(
        functools.partial(_attn_lat_kernel, nb=nb),
        grid=(nseq, nb),
        in_specs=[
            pl.BlockSpec(memory_space=pltpu.SMEM),
            pl.BlockSpec((ATT_BLK, hq_w), lambda b, q: (base + b * nb + q, 0)),
            kv_spec(prv, kcol), kv_spec(same, kcol), kv_spec(nxt, kcol),
            kv_spec(prv, kcol + 1), kv_spec(same, kcol + 1), kv_spec(nxt, kcol + 1),
            cache_spec, cache_spec,
            rope_spec(same), rope_spec(same), rope_spec(prv), rope_spec(prv), rope_spec(nxt), rope_spec(nxt),
        ],
        out_specs=pl.BlockSpec((ATT_BLK, hq_w), lambda b, q: (b * nb + q, 0)),
        out_shape=jax.ShapeDtypeStruct((nseq * seq_len, hq_w), F32),
        compiler_params=_params(("parallel", "parallel")),
        name="attention_latent",
    )(sink, z, z, z, z, z, z, z, cache_k, cache_v, cos, sin, cos, sin, cos, sin)


def _attn_out_kernel(o_ref, x_ref, m_ref, w_ref, lg_ref, lb_ref, out_ref):
    y = _dot(o_ref[...].astype(BF16), w_ref[...])
    gate = m_ref[2:3, :]
    r = DEEPNORM_ALPHA * x_ref[...] + gate * y
    out_ref[...] = _layer_norm(r, lg_ref[...], lb_ref[...])


def _attn_out(o, x, mod, w_bf, ln_g, ln_b, layer, grp):
    nt, d = x.shape
    tm = OUT_TM
    k = o.shape[1]
    return pl.pallas_call(
        _attn_out_kernel,
        grid=(nt // tm,),
        in_specs=[
            pl.BlockSpec((tm, k), lambda b: (b, 0)),
            pl.BlockSpec((tm, d), lambda b: (b, 0)),
            pl.BlockSpec((None, None, 6, d), lambda b: (layer, grp(b, tm), 0, 0)),
            pl.BlockSpec((k, d), lambda b: (0, 0)),
            pl.BlockSpec((None, 1, d), lambda b: (2 * layer, 0, 0)),
            pl.BlockSpec((None, 1, d), lambda b: (2 * layer, 0, 0)),
        ],
        out_specs=pl.BlockSpec((tm, d), lambda b: (b, 0)),
        out_shape=jax.ShapeDtypeStruct((nt, d), F32),
        compiler_params=_params(("parallel",)),
        name="attention_out",
    )(o, x, mod, w_bf, ln_g, ln_b)


def _top_values(s, k):
    vals = []
    work = s
    for r in range(k):
        m = jnp.max(work, axis=0, keepdims=True)
        vals.append(m)
        if r + 1 < k:
            work = jnp.where(work >= m, NEG_INF, work)
    return vals


def _stack_rows(rows_list):
    n = len(rows_list)
    ridx = lax.broadcasted_iota(jnp.int32, (n, rows_list[0].shape[1]), 0)
    out = jnp.zeros((n, rows_list[0].shape[1]), F32)
    for r, row in enumerate(rows_list):
        out = jnp.where(ridx == r, row, out)
    return out


def _peer_kernel(x_ref, m_ref, wq_ref, keys_ref, u_ref, vt_ref, lg_ref, lb_ref, out_ref,
                 h_scr, q_scr, s1_scr, e1_scr, s2_scr, e2_scr, thr_scr, g_scr, acc_scr, *, n_eblk):
    t = x_ref.shape[0]
    nlt = t // LANES
    e = pl.program_id(1)
    k = PEER_TOPK
    nk = PEER_NKEYS

    @pl.when(e == 0)
    def _scores():
        sh = m_ref[3:4, :]
        sc = m_ref[4:5, :]
        h_scr[...] = (x_ref[...] * (1.0 + sc) + sh).astype(BF16)
        q = _dot(h_scr[...], wq_ref[...])
        for ph in range(2 * PEER_HEADS):
            q_scr[ph] = q[:, ph * PEER_DHALF:(ph + 1) * PEER_DHALF].astype(BF16)
        acc_scr[...] = jnp.zeros(acc_scr.shape, F32)

        def unit(uidx, carry):
            lt = uidx // PEER_HEADS
            p = uidx % PEER_HEADS
            t0 = pl.multiple_of(lt * LANES, LANES)
            s, vals = [], []
            for hh in range(2):
                sk = _dot_nt(keys_ref[p, hh], q_scr[2 * p + hh, pl.ds(t0, LANES), :])
                s.append(sk)
                vals.append(_top_values(sk, k))
            v1m = [v - vals[0][0] for v in vals[0]]
            v2m = [v - vals[1][0] for v in vals[1]]
            v1s = _stack_rows(v1m)
            v2s = _stack_rows(v2m)
            ridx = lax.broadcasted_iota(jnp.int32, (k, LANES), 0)
            cands = [v1m[a] + v2s for a in range(4)]
            cands += [jnp.where(ridx >= 4, v2m[b] + v1s, NEG_INF) for b in range(4)]
            work = cands
            thr = None
            for r in range(k):
                mm = work[0]
                for cnd in work[1:]:
                    mm = jnp.maximum(mm, cnd)
                thr = jnp.max(mm, axis=0, keepdims=True)
                if r + 1 < k:
                    work = [jnp.where(cnd >= thr, NEG_INF, cnd) for cnd in work]
            zsum = jnp.zeros((1, LANES), F32)
            for cnd in cands:
                zsum = zsum + jnp.sum(jnp.where(cnd >= thr, jnp.exp(cnd), 0.0), axis=0, keepdims=True)
            s1m = jnp.where(s[0] >= vals[0][k - 1], s[0] - vals[0][0], NEG_INF)
            s2m = jnp.where(s[1] >= vals[1][k - 1], s[1] - vals[1][0], NEG_INF)
            s1_scr[p, lt] = s1m
            e1_scr[p, lt] = jnp.exp(s1m) / zsum
            s2_scr[p, lt] = s2m
            e2_scr[p, lt] = jnp.exp(s2m)
            thr_scr[p * nlt + lt] = thr
            return carry

        lax.fori_loop(0, nlt * PEER_HEADS, unit, 0)

    def expert_rows(il, carry):
        i1 = e * PEER_NI1 + il
        r0 = pl.multiple_of(il * nk, nk)
        at = _dot_nt(u_ref[pl.ds(r0, nk), :], h_scr[...])
        ga = jax.nn.gelu(at)
        for lt in range(nlt):
            w = jnp.zeros((nk, LANES), F32)
            for p in range(PEER_HEADS):
                v = s1_scr[p, lt, pl.ds(i1, 1), :] + s2_scr[p, lt]
                ww = e1_scr[p, lt, pl.ds(i1, 1), :] * e2_scr[p, lt]
                w = w + jnp.where(v >= thr_scr[p * nlt + lt], ww, 0.0)
            g_scr[pl.ds(r0, nk), lt * LANES:(lt + 1) * LANES] = (w * ga[:, lt * LANES:(lt + 1) * LANES]).astype(BF16)
        return carry

    lax.fori_loop(0, PEER_NI1, expert_rows, 0)
    acc_scr[...] += _dot(vt_ref[...], g_scr[...])

    @pl.when(e == n_eblk - 1)
    def _finish():
        y = acc_scr[...].T
        gate = m_ref[5:6, :]
        r = DEEPNORM_ALPHA * x_ref[...] + gate * y
        out_ref[...] = _layer_norm(r, lg_ref[...], lb_ref[...])


def _peer(x, mod, wq_bf, keys_bf, u_bf, vt_bf, ln_g, ln_b, layer, grp):
    nt, d = x.shape
    t = PEER_T
    nlt = t // LANES
    eblk = PEER_NI1 * PEER_NKEYS
    n_eblk = u_bf.shape[0] // eblk
    nq = wq_bf.shape[1]
    sel_shape = (PEER_HEADS, nlt, PEER_NKEYS, LANES)
    return pl.pallas_call(
        functools.partial(_peer_kernel, n_eblk=n_eblk),
        grid=(nt // t, n_eblk),
        in_specs=[
            pl.BlockSpec((t, d), lambda b, e: (b, 0)),
            pl.BlockSpec((None, None, 6, d), lambda b, e: (layer, grp(b, t), 0, 0)),
            pl.BlockSpec((d, nq), lambda b, e: (0, 0)),
            pl.BlockSpec((PEER_HEADS, 2, PEER_NKEYS, PEER_DHALF), lambda b, e: (0, 0, 0, 0)),
            pl.BlockSpec((eblk, d), lambda b, e: (e, 0)),
            pl.BlockSpec((d, eblk), lambda b, e: (0, e)),
            pl.BlockSpec((None, 1, d), lambda b, e: (2 * layer + 1, 0, 0)),
            pl.BlockSpec((None, 1, d), lambda b, e: (2 * layer + 1, 0, 0)),
        ],
        out_specs=pl.BlockSpec((t, d), lambda b, e: (b, 0)),
        out_shape=jax.ShapeDtypeStruct((nt, d), F32),
        scratch_shapes=[
            pltpu.VMEM((t, d), BF16),
            pltpu.VMEM((2 * PEER_HEADS, t, PEER_DHALF), BF16),
            pltpu.VMEM(sel_shape, F32), pltpu.VMEM(sel_shape, F32),
            pltpu.VMEM(sel_shape, F32), pltpu.VMEM(sel_shape, F32),
            pltpu.VMEM((PEER_HEADS * nlt, 1, LANES), F32),
            pltpu.VMEM((eblk, t), BF16),
            pltpu.VMEM((d, t), F32),
        ],
        compiler_params=_params(("parallel", "arbitrary")),
        name="peer",
    )(x, mod, wq_bf, keys_bf, u_bf, vt_bf, ln_g, ln_b)


def _rope_angles(pos, dim):
    inv = ROPE_BASE ** (-jnp.arange(0, dim, 2, dtype=F32) / dim)
    ang = pos.astype(F32)[:, None] * inv[None, :]
    return jnp.cos(ang), jnp.sin(ang)


def _axial_tables(seq_len):
    t = jnp.arange(seq_len)
    half = ATT_DH // 2
    cr, sr = _rope_angles(t // GRID_W, half)
    cc, sc = _rope_angles(t % GRID_W, half)
    cos = jnp.concatenate([cr, cr, cc, cc], -1)
    sin = jnp.concatenate([-sr, sr, -sc, sc], -1)
    reps = LANES // ATT_DH
    return jnp.tile(cos, (1, reps)), jnp.tile(sin, (1, reps))


def kernel(x_prompt, x_sample, state_ret_fwd, state_ret_bwd, cache_k, cache_v, c, c_ctx, mod_w, mod_b, ln_g, ln_b,
           ret_w_in, ret_w_out, ret_decay, attn_w_in, attn_w_out, attn_sink, peer_wq, peer_keys, peer_u, peer_v):
    nb_p, len_p, d = x_prompt.shape
    nb_s, len_s, _ = x_sample.shape
    n_p = nb_p * len_p
    n_s = nb_s * len_s
    for tile in (PROJ_TM, OUT_TM, PEER_T):
        assert n_p % tile == 0 and len_s % tile == 0

    def grp(blk, tile):
        return _group_index(blk, n_p // tile, len_s // tile)

    x = jnp.concatenate([x_prompt.reshape(n_p, d), x_sample.reshape(n_s, d)], axis=0)
    n_grp = 1 + nb_s
    n_grp_pad = -(-n_grp // 8) * 8
    cond = jnp.concatenate([c_ctx[None, :], c, jnp.zeros((n_grp_pad - n_grp, d), F32)], axis=0)
    mod = _modulation(cond, mod_w, mod_b).reshape(DEPTH, n_grp_pad, 6, d)

    ln_g3 = ln_g.reshape(DEPTH * 2, 1, d)
    ln_b3 = ln_b.reshape(DEPTH * 2, 1, d)
    ret_rope = _rope_angles(jnp.arange(len_s), RET_DK)
    att_rope = _axial_tables(len_s)
    kv_w = ATT_HKV * ATT_DH
    cache_k4 = cache_k.reshape(cache_k.shape[0], cache_k.shape[1], cache_k.shape[2], kv_w)
    cache_v4 = cache_v.reshape(cache_v.shape[0], cache_v.shape[1], cache_v.shape[2], kv_w)

    new_sf, new_sb, new_k, new_v = [], [], [], []
    for i in range(DEPTH):
        j = i // N_MIXERS
        if i % N_MIXERS == 0:
            z = _project(x, mod, ret_w_in[j].astype(BF16), i, 0, grp)
            tables = _ret_tables(ret_decay[j])
            o_p, sf, sb = _retention(z, tables, 0, nb_p, len_p, False)
            o_s = _retention(z, tables, n_p, nb_s, len_s, True, rope=ret_rope,
                             states=(state_ret_fwd, state_ret_bwd), j=j)
            new_sf.append(sf)
            new_sb.append(sb)
            o = jnp.concatenate([o_p, o_s], axis=0)
            x = _ret_out(o, z, x, mod, ret_w_out[j].astype(BF16), ln_g3, ln_b3, i, grp)
        else:
            z = _project(x, mod, attn_w_in[j].astype(BF16), i, 0, grp)
            sink = attn_sink[j].astype(F32)
            o_p = _attn_context(z, sink, nb_p, len_p)
            o_s = _attn_latent(z, sink, cache_k4, cache_v4, att_rope, n_p, nb_s, len_s, j)
            hq_w = ATT_HQ * ATT_DH
            new_k.append(z[:n_p, hq_w:hq_w + kv_w].reshape(nb_p, len_p, ATT_HKV, ATT_DH))
            new_v.append(z[:n_p, hq_w + kv_w:hq_w + 2 * kv_w].reshape(nb_p, len_p, ATT_HKV, ATT_DH))
            o = jnp.concatenate([o_p, o_s], axis=0)
            x = _attn_out(o, x, mod, attn_w_out[j].astype(BF16), ln_g3, ln_b3, i, grp)
        x = _peer(x, mod, peer_wq[i].astype(BF16), peer_keys[i].astype(BF16), peer_u[i].astype(BF16),
                  peer_v[i].T.astype(BF16), ln_g3, ln_b3, i, grp)

    y_p = x[:n_p].reshape(nb_p, len_p, d)
    y_s = x[n_p:].reshape(nb_s, len_s, d)
    return (y_p, y_s, jnp.stack(new_sf, 1), jnp.stack(new_sb, 1), jnp.stack(new_k, 1), jnp.stack(new_v, 1))
```

```python
import functools
import math

import jax
import jax.numpy as jnp
from jax import lax
from jax.experimental import pallas as pl
from jax.experimental.pallas import tpu as pltpu

F32 = jnp.float32
BF16 = jnp.bfloat16

DEPTH = 4
N_MIXERS = 2
GRID_W = 64
RET_HEADS = 4
RET_DK = 256
RET_DV = 512
RET_CHUNK = 128
ATT_HQ = 16
ATT_HKV = 4
ATT_G = ATT_HQ // ATT_HKV
ATT_DH = 64
ATT_BLK = 128
ROPE_BASE = 10000.0
PEER_HEADS = 8
PEER_NKEYS = 128
PEER_DHALF = 128
PEER_TOPK = 16
DEEPNORM_ALPHA = (2.0 * DEPTH) ** 0.25
LN_EPS = 1e-5
GN_EPS = 1e-5
NEG_INF = -1e30
GELU_C = math.sqrt(2.0 / math.pi)

LANES = 128
VMEM_LIMIT = 56 * 1024 * 1024
PROJ_TM = 512
OUT_TM = 256
PEER_T = 512
PEER_NI1 = 4
PEER_NSUB = 4


def _dot(a, b):
    return jnp.dot(a, b, preferred_element_type=F32)


def _dot_nt(a, b):
    return lax.dot_general(a, b, (((1,), (1,)), ((), ())), preferred_element_type=F32)


def _params(sem, flags=None):
    return pltpu.CompilerParams(dimension_semantics=sem, vmem_limit_bytes=VMEM_LIMIT, flags=flags)


def _group_index(blk, n_prompt_blocks, blocks_per_seq):
    return jnp.where(blk < n_prompt_blocks, 0, 1 + (blk - n_prompt_blocks) // blocks_per_seq)


def _layer_norm(r, g, b):
    mu = jnp.mean(r, -1, keepdims=True)
    d = r - mu
    var = jnp.mean(d * d, -1, keepdims=True)
    return d * lax.rsqrt(var + LN_EPS) * g + b


def _mod_kernel(c_ref, w_ref, b_ref, o_ref):
    c = c_ref[...]
    a = (c * jax.nn.sigmoid(c)).astype(BF16)
    o_ref[...] = _dot(a, w_ref[...].astype(BF16)) + b_ref[...]


def _modulation(cond, mod_w, mod_b):
    ngp, d = cond.shape
    n_out = mod_w.shape[-1]
    tn = 1536
    return pl.pallas_call(
        _mod_kernel,
        grid=(DEPTH, n_out // tn),
        in_specs=[
            pl.BlockSpec((ngp, d), lambda i, n: (0, 0)),
            pl.BlockSpec((None, d, tn), lambda i, n: (i, 0, n)),
            pl.BlockSpec((None, 1, tn), lambda i, n: (i, 0, n)),
        ],
        out_specs=pl.BlockSpec((None, ngp, tn), lambda i, n: (i, 0, n)),
        out_shape=jax.ShapeDtypeStruct((DEPTH, ngp, n_out), F32),
        compiler_params=_params(("parallel", "parallel")),
        name="modulation",
    )(cond, mod_w, mod_b.reshape(DEPTH, 1, n_out))


def _proj_kernel(x_ref, m_ref, w_ref, o_ref, *, off):
    sh = m_ref[off:off + 1, :]
    sc = m_ref[off + 1:off + 2, :]
    h = (x_ref[...] * (1.0 + sc) + sh).astype(BF16)
    o_ref[...] = _dot(h, w_ref[...])


def _project(x, mod, w_bf, layer, off, grp):
    nt, d = x.shape
    n = w_bf.shape[1]
    tn = 1536
    tm = PROJ_TM
    return pl.pallas_call(
        functools.partial(_proj_kernel, off=off),
        grid=(n // tn, nt // tm),
        in_specs=[
            pl.BlockSpec((tm, d), lambda j, b: (b, 0)),
            pl.BlockSpec((None, None, 6, d), lambda j, b: (layer, grp(b, tm), 0, 0)),
            pl.BlockSpec((d, tn), lambda j, b: (0, j)),
        ],
        out_specs=pl.BlockSpec((tm, tn), lambda j, b: (b, j)),
        out_shape=jax.ShapeDtypeStruct((nt, n), F32),
        compiler_params=_params(("parallel", "parallel")),
        name="mod_proj",
    )(x, mod, w_bf)


def _rope_half(x, cos, sin):
    x1 = x[:, :LANES]
    x2 = x[:, LANES:]
    return jnp.concatenate([x1 * cos - x2 * sin, x1 * sin + x2 * cos], axis=1)


def _ret_kernel(*refs, latent, nc):
    it = iter(refs)
    qa, ka, va, qb, kb, vb = (next(it) for _ in range(6))
    if latent:
        cosa, sina, cosb, sinb = (next(it) for _ in range(4))
    dfb, qdf, qdb, kdf, kdb, cdf, cdb = (next(it) for _ in range(7))
    if latent:
        s0f, s0b = next(it), next(it)
    o_ref = next(it)
    if not latent:
        sf_out, sb_out = next(it), next(it)
    sf, sb = next(it), next(it)

    c = pl.program_id(2)

    @pl.when(c == 0)
    def _init():
        o_ref[...] = jnp.zeros(o_ref.shape, F32)
        if latent:
            sf[...] = s0f[...]
            sb[...] = s0b[...]
        else:
            sf[...] = jnp.zeros(sf.shape, F32)
            sb[...] = jnp.zeros(sb.shape, F32)

    def prep(q_ref, k_ref, cos_ref, sin_ref):
        q = q_ref[...]
        k = k_ref[...] * (RET_DK ** -0.5)
        if latent:
            cos = cos_ref[...]
            sin = sin_ref[...]
            q = _rope_half(q, cos, sin)
            k = _rope_half(k, cos, sin)
        return q, k

    q, k = prep(qa, ka, cosa if latent else None, sina if latent else None)
    v = va[...].astype(BF16)
    att = (_dot_nt(q.astype(BF16), k.astype(BF16)) * dfb[...]).astype(BF16)
    o_c = _dot(att, v) + _dot((q * qdf[...]).astype(BF16), sf[...].astype(BF16))
    rows_c = pl.ds(pl.multiple_of(c * RET_CHUNK, RET_CHUNK), RET_CHUNK)
    o_ref[rows_c, :] += o_c
    kd_t = (k * kdf[...]).T.astype(BF16)
    sf[...] = sf[...] * cdf[...] + _dot(kd_t, v)

    q2, k2 = prep(qb, kb, cosb if latent else None, sinb if latent else None)
    v2 = vb[...].astype(BF16)
    o_b = _dot((q2 * qdb[...]).astype(BF16), sb[...].astype(BF16))
    rows_b = pl.ds(pl.multiple_of((nc - 1 - c) * RET_CHUNK, RET_CHUNK), RET_CHUNK)
    o_ref[rows_b, :] += o_b
    kd2_t = (k2 * kdb[...]).T.astype(BF16)
    sb[...] = sb[...] * cdb[...] + _dot(kd2_t, v2)

    if not latent:
        @pl.when(c == nc - 1)
        def _fin():
            sf_out[...] = sf[...]
            sb_out[...] = sb[...]


def _ret_tables(decay):
    log_g = jax.nn.log_sigmoid(decay.astype(F32))
    lf, lb = log_g[0], log_g[1]
    idx = jnp.arange(RET_CHUNK, dtype=F32)
    diff = idx[:, None] - idx[None, :]
    fmask = diff >= 0
    bmask = diff < 0
    df = jnp.where(fmask[None], jnp.exp(jnp.where(fmask, diff, 0.0)[None] * lf[:, None, None]), 0.0)
    db = jnp.where(bmask[None], jnp.exp(jnp.where(bmask, -diff, 0.0)[None] * lb[:, None, None]), 0.0)
    dfb = df + db
    def rows(e):
        return jnp.broadcast_to(e[:, :, None], (RET_HEADS, RET_CHUNK, RET_DK))
    qdf = rows(jnp.exp((idx + 1.0)[None, :] * lf[:, None]))
    kdf = rows(jnp.exp((RET_CHUNK - 1.0 - idx)[None, :] * lf[:, None]))
    qdb = rows(jnp.exp((RET_CHUNK - idx)[None, :] * lb[:, None]))
    kdb = rows(jnp.exp(idx[None, :] * lb[:, None]))
    cdf = jnp.broadcast_to(jnp.exp(RET_CHUNK * lf)[:, None, None], (RET_HEADS, 1, RET_DV))
    cdb = jnp.broadcast_to(jnp.exp(RET_CHUNK * lb)[:, None, None], (RET_HEADS, 1, RET_DV))
    return dfb, qdf, qdb, kdf, kdb, cdf, cdb


def _retention(z, tables, row0, nseq, seq_len, latent, rope=None, states=None, j=0):
    nc = seq_len // RET_CHUNK
    base = row0 // RET_CHUNK
    h_ = RET_HEADS
    kcol = (h_ * RET_DK) // RET_DK
    vcol = (2 * h_ * RET_DK) // RET_DV

    def fw(b, h, c):
        return base + b * nc + c

    def bw(b, h, c):
        return base + b * nc + (nc - 1 - c)

    in_specs = [
        pl.BlockSpec((RET_CHUNK, RET_DK), lambda b, h, c: (fw(b, h, c), h)),
        pl.BlockSpec((RET_CHUNK, RET_DK), lambda b, h, c: (fw(b, h, c), kcol + h)),
        pl.BlockSpec((RET_CHUNK, RET_DV), lambda b, h, c: (fw(b, h, c), vcol + h)),
        pl.BlockSpec((RET_CHUNK, RET_DK), lambda b, h, c: (bw(b, h, c), h)),
        pl.BlockSpec((RET_CHUNK, RET_DK), lambda b, h, c: (bw(b, h, c), kcol + h)),
        pl.BlockSpec((RET_CHUNK, RET_DV), lambda b, h, c: (bw(b, h, c), vcol + h)),
    ]
    args = [z, z, z, z, z, z]
    if latent:
        cos, sin = rope
        in_specs += [
            pl.BlockSpec((RET_CHUNK, LANES), lambda b, h, c: (c, 0)),
            pl.BlockSpec((RET_CHUNK, LANES), lambda b, h, c: (c, 0)),
            pl.BlockSpec((RET_CHUNK, LANES), lambda b, h, c: (nc - 1 - c, 0)),
            pl.BlockSpec((RET_CHUNK, LANES), lambda b, h, c: (nc - 1 - c, 0)),
        ]
        args += [cos, sin, cos, sin]
    dfb, qdf, qdb, kdf, kdb, cdf, cdb = tables
    in_specs += [pl.BlockSpec((None, RET_CHUNK, RET_CHUNK), lambda b, h, c: (h, 0, 0))]
    in_specs += [pl.BlockSpec((None, RET_CHUNK, RET_DK), lambda b, h, c: (h, 0, 0))] * 4
    in_specs += [pl.BlockSpec((None, 1, RET_DV), lambda b, h, c: (h, 0, 0))] * 2
    args += [dfb, qdf, qdb, kdf, kdb, cdf, cdb]
    if latent:
        s0f, s0b = states
        st_spec = pl.BlockSpec((None, None, None, RET_DK, RET_DV), lambda b, h, c: (b, j, h, 0, 0))
        in_specs += [st_spec, st_spec]
        args += [s0f, s0b]

    o_shape = jax.ShapeDtypeStruct((nseq * seq_len, h_ * RET_DV), F32)
    o_spec = pl.BlockSpec((seq_len, RET_DV), lambda b, h, c: (b, h))
    if latent:
        out_shape, out_specs = o_shape, o_spec
    else:
        s_shape = jax.ShapeDtypeStruct((nseq, h_, RET_DK, RET_DV), F32)
        s_spec = pl.BlockSpec((None, None, RET_DK, RET_DV), lambda b, h, c: (b, h, 0, 0))
        out_shape, out_specs = (o_shape, s_shape, s_shape), (o_spec, s_spec, s_spec)

    return pl.pallas_call(
        functools.partial(_ret_kernel, latent=latent, nc=nc),
        grid=(nseq, h_, nc),
        in_specs=in_specs,
        out_specs=out_specs,
        out_shape=out_shape,
        scratch_shapes=[pltpu.VMEM((RET_DK, RET_DV), F32), pltpu.VMEM((RET_DK, RET_DV), F32)],
        compiler_params=_params(("parallel", "parallel", "arbitrary")),
        name="retention_latent" if latent else "retention_context",
    )(*args)


def _ret_out_kernel(o_ref, g_ref, x_ref, m_ref, w_ref, lg_ref, lb_ref, out_ref):
    y = None
    for h in range(RET_HEADS):
        cols = slice(h * RET_DV, (h + 1) * RET_DV)
        oh = o_ref[:, cols]
        mu = jnp.mean(oh, -1, keepdims=True)
        d = oh - mu
        var = jnp.mean(d * d, -1, keepdims=True)
        on = d * lax.rsqrt(var + GN_EPS)
        gh = g_ref[:, cols]
        act = (gh * jax.nn.sigmoid(gh) * on).astype(BF16)
        part = _dot(act, w_ref[cols, :])
        y = part if y is None else y + part
    gate = m_ref[2:3, :]
    r = DEEPNORM_ALPHA * x_ref[...] + gate * y
    out_ref[...] = _layer_norm(r, lg_ref[...], lb_ref[...])


def _ret_out(o, z, x, mod, w_bf, ln_g, ln_b, layer, grp):
    nt, d = x.shape
    tm = OUT_TM
    hv = RET_HEADS * RET_DV
    gcol = (2 * RET_HEADS * RET_DK + hv) // hv
    return pl.pallas_call(
        _ret_out_kernel,
        grid=(nt // tm,),
        in_specs=[
            pl.BlockSpec((tm, hv), lambda b: (b, 0)),
            pl.BlockSpec((tm, hv), lambda b: (b, gcol)),
            pl.BlockSpec((tm, d), lambda b: (b, 0)),
            pl.BlockSpec((None, None, 6, d), lambda b: (layer, grp(b, tm), 0, 0)),
            pl.BlockSpec((hv, d), lambda b: (0, 0)),
            pl.BlockSpec((None, 1, d), lambda b: (2 * layer, 0, 0)),
            pl.BlockSpec((None, 1, d), lambda b: (2 * layer, 0, 0)),
        ],
        out_specs=pl.BlockSpec((tm, d), lambda b: (b, 0)),
        out_shape=jax.ShapeDtypeStruct((nt, d), F32),
        compiler_params=_params(("parallel",)),
        name="retention_out",
    )(o, z, x, mod, w_bf, ln_g, ln_b)


def _softmax_pv(parts, sink):
    m = sink
    for s, _ in parts:
        m = jnp.maximum(m, jnp.max(s, -1, keepdims=True))
    den = jnp.exp(sink - m)
    o = None
    for s, v in parts:
        p = jnp.exp(s - m)
        den = den + jnp.sum(p, -1, keepdims=True)
        pv = _dot(p.astype(BF16), v)
        o = pv if o is None else o + pv
    return o / den


def _attn_ctx_kernel(sink_ref, q_ref, k_ref, v_ref, o_ref):
    scale = ATT_DH ** -0.5
    for h in range(ATT_HKV):
        kh = k_ref[:, h * ATT_DH:(h + 1) * ATT_DH].astype(BF16)
        vh = v_ref[:, h * ATT_DH:(h + 1) * ATT_DH].astype(BF16)
        for g in range(ATT_G):
            hq = h * ATT_G + g
            cols = slice(hq * ATT_DH, (hq + 1) * ATT_DH)
            qh = q_ref[:, cols].astype(BF16)
            s = _dot_nt(qh, kh) * scale
            o_ref[:, cols] = _softmax_pv([(s, vh)], sink_ref[hq])


def _attn_context(z, sink, nseq, seq_len):
    hq_w = ATT_HQ * ATT_DH
    kv_w = ATT_HKV * ATT_DH
    return pl.pallas_call(
        _attn_ctx_kernel,
        grid=(nseq,),
        in_specs=[
            pl.BlockSpec(memory_space=pltpu.SMEM),
            pl.BlockSpec((seq_len, hq_w), lambda b: (b, 0)),
            pl.BlockSpec((seq_len, kv_w), lambda b: (b, hq_w // kv_w)),
            pl.BlockSpec((seq_len, kv_w), lambda b: (b, hq_w // kv_w + 1)),
        ],
        out_specs=pl.BlockSpec((seq_len, hq_w), lambda b: (b, 0)),
        out_shape=jax.ShapeDtypeStruct((nseq * seq_len, hq_w), F32),
        compiler_params=_params(("parallel",)),
        name="attention_context",
    )(sink, z, z, z)


def _rope_axial(x, cos, sin):
    w = x.shape[1]
    lane = lax.broadcasted_iota(jnp.int32, (x.shape[0], LANES), 1)
    first = (lane % 32) < 16
    outs = []
    for cg in range(w // LANES):
        xg = x[:, cg * LANES:(cg + 1) * LANES]
        up = pltpu.roll(xg, LANES - 16, 1)
        dn = pltpu.roll(xg, 16, 1)
        outs.append(xg * cos + jnp.where(first, up, dn) * sin)
    return jnp.concatenate(outs, axis=1) if len(outs) > 1 else outs[0]


def _attn_lat_kernel(sink_ref, q_ref, kp_ref, kc_ref, kn_ref, vp_ref, vc_ref, vn_ref, ck_ref, cv_ref,
                     cq_ref, sq_ref, cp_ref, sp_ref, cn_ref, sn_ref, o_ref, *, nb):
    scale = ATT_DH ** -0.5
    qb = pl.program_id(1)
    q = _rope_axial(q_ref[...], cq_ref[...], sq_ref[...])
    kw = jnp.concatenate([
        _rope_axial(kp_ref[...], cp_ref[...], sp_ref[...]),
        _rope_axial(kc_ref[...], cq_ref[...], sq_ref[...]),
        _rope_axial(kn_ref[...], cn_ref[...], sn_ref[...]),
    ], axis=0)
    vw = jnp.concatenate([vp_ref[...], vc_ref[...], vn_ref[...]], axis=0)
    ck = ck_ref[...]
    cv = cv_ref[...]

    i = lax.broadcasted_iota(jnp.int32, (ATT_BLK, 3 * ATT_BLK), 0)
    jj = lax.broadcasted_iota(jnp.int32, (ATT_BLK, 3 * ATT_BLK), 1)
    lo = jnp.where(qb > 0, 0, ATT_BLK)
    hi = jnp.where(qb < nb - 1, 3 * ATT_BLK, 2 * ATT_BLK)
    valid = (jj >= jnp.maximum(i, lo)) & (jj < jnp.minimum(i + 2 * ATT_BLK + 1, hi))

    for h in range(ATT_HKV):
        hs = slice(h * ATT_DH, (h + 1) * ATT_DH)
        kh = kw[:, hs].astype(BF16)
        vh = vw[:, hs].astype(BF16)
        ckh = ck[:, hs].astype(BF16)
        cvh = cv[:, hs].astype(BF16)
        for g in range(ATT_G):
            hq = h * ATT_G + g
            cols = slice(hq * ATT_DH, (hq + 1) * ATT_DH)
            qh = q[:, cols].astype(BF16)
            s_loc = jnp.where(valid, _dot_nt(qh, kh) * scale, NEG_INF)
            s_ctx = _dot_nt(qh, ckh) * scale
            o_ref[:, cols] = _softmax_pv([(s_loc, vh), (s_ctx, cvh)], sink_ref[hq])


def _attn_latent(z, sink, cache_k, cache_v, rope, row0, nseq, seq_len, j):
    nb = seq_len // ATT_BLK
    base = row0 // ATT_BLK
    hq_w = ATT_HQ * ATT_DH
    kv_w = ATT_HKV * ATT_DH
    kcol = hq_w // kv_w
    past = cache_k.shape[2]
    cos, sin = rope

    def prv(q):
        return jnp.maximum(q - 1, 0)

    def nxt(q):
        return jnp.minimum(q + 1, nb - 1)

    def kv_spec(sel, col):
        return pl.BlockSpec((ATT_BLK, kv_w), lambda b, q: (base + b * nb + sel(q), col))

    def rope_spec(sel):
        return pl.BlockSpec((ATT_BLK, LANES), lambda b, q: (sel(q), 0))

    same = lambda q: q
    cache_spec = pl.BlockSpec((None, None, past, kv_w), lambda b, q: (b, j, 0, 0))
    return pl.pallas_call(
        functools.partial(_attn_lat_kernel, nb=nb),
        grid=(nseq, nb),
        in_specs=[
            pl.BlockSpec(memory_space=pltpu.SMEM),
            pl.BlockSpec((ATT_BLK, hq_w), lambda b, q: (base + b * nb + q, 0)),
            kv_spec(prv, kcol), kv_spec(same, kcol), kv_spec(nxt, kcol),
            kv_spec(prv, kcol + 1), kv_spec(same, kcol + 1), kv_spec(nxt, kcol + 1),
            cache_spec, cache_spec,
            rope_spec(same), rope_spec(same), rope_spec(prv), rope_spec(prv), rope_spec(nxt), rope_spec(nxt),
        ],
        out_specs=pl.BlockSpec((ATT_BLK, hq_w), lambda b, q: (b * nb + q, 0)),
        out_shape=jax.ShapeDtypeStruct((nseq * seq_len, hq_w), F32),
        compiler_params=_params(("parallel", "parallel")),
        name="attention_latent",
    )(sink, z, z, z, z, z, z, z, cache_k, cache_v, cos, sin, cos, sin, cos, sin)


def _attn_out_kernel(o_ref, x_ref, m_ref, w_ref, lg_ref, lb_ref, out_ref):
    y = _dot(o_ref[...].astype(BF16), w_ref[...])
    gate = m_ref[2:3, :]
    r = DEEPNORM_ALPHA * x_ref[...] + gate * y
    out_ref[...] = _layer_norm(r, lg_ref[...], lb_ref[...])


def _attn_out(o, x, mod, w_bf, ln_g, ln_b, layer, grp):
    nt, d = x.shape
    tm = OUT_TM
    k = o.shape[1]
    return pl.pallas_call(
        _attn_out_kernel,
        grid=(nt // tm,),
        in_specs=[
            pl.BlockSpec((tm, k), lambda b: (b, 0)),
            pl.BlockSpec((tm, d), lambda b: (b, 0)),
            pl.BlockSpec((None, None, 6, d), lambda b: (layer, grp(b, tm), 0, 0)),
            pl.BlockSpec((k, d), lambda b: (0, 0)),
            pl.BlockSpec((None, 1, d), lambda b: (2 * layer, 0, 0)),
            pl.BlockSpec((None, 1, d), lambda b: (2 * layer, 0, 0)),
        ],
        out_specs=pl.BlockSpec((tm, d), lambda b: (b, 0)),
        out_shape=jax.ShapeDtypeStruct((nt, d), F32),
        compiler_params=_params(("parallel",)),
        name="attention_out",
    )(o, x, mod, w_bf, ln_g, ln_b)


def _top_values(s, k):
    vals = []
    work = s
    for r in range(k):
        m = jnp.max(work, axis=0, keepdims=True)
        vals.append(m)
        if r + 1 < k:
            work = jnp.where(work >= m, NEG_INF, work)
    return vals


def _stack_rows(rows_list):
    n = len(rows_list)
    ridx = lax.broadcasted_iota(jnp.int32, (n, rows_list[0].shape[1]), 0)
    out = jnp.zeros((n, rows_list[0].shape[1]), F32)
    for r, row in enumerate(rows_list):
        out = jnp.where(ridx == r, row, out)
    return out


def _peer_kernel(x_ref, m_ref, wq_ref, keys_ref, u0_ref, u_ref, vt_ref, vtl_ref, lg_ref, lb_ref, out_ref,
                 h_scr, q_scr, d1_scr, e1_scr, s2_scr, e2_scr, at0_scr, at1_scr, g0_scr, g1_scr, acc_scr,
                 *, n_eblk):
    t = x_ref.shape[0]
    nlt = t // LANES
    e = pl.program_id(1)
    k = PEER_TOPK
    nk = PEER_NKEYS
    hk = nk // PEER_NSUB

    @pl.when(e == 0)
    def _scores():
        sh = m_ref[3:4, :]
        sc = m_ref[4:5, :]
        h_scr[...] = (x_ref[...] * (1.0 + sc) + sh).astype(BF16)
        q = _dot(h_scr[...], wq_ref[...])
        for ph in range(2 * PEER_HEADS):
            q_scr[ph] = q[:, ph * PEER_DHALF:(ph + 1) * PEER_DHALF].astype(BF16)
        acc_scr[...] = jnp.zeros(acc_scr.shape, F32)
        g1_scr[...] = jnp.zeros(g1_scr.shape, BF16)
        at0_scr[...] = _dot_nt(u0_ref[...], h_scr[...])

        def unit(uidx, carry):
            lt = uidx // PEER_HEADS
            p = uidx % PEER_HEADS
            t0 = pl.multiple_of(lt * LANES, LANES)
            s, vals = [], []
            for hh in range(2):
                sk = _dot_nt(keys_ref[p, hh], q_scr[2 * p + hh, pl.ds(t0, LANES), :])
                s.append(sk)
                vals.append(_top_values(sk, k))
            v1m = [v - vals[0][0] for v in vals[0]]
            v2m = [v - vals[1][0] for v in vals[1]]
            v1s = _stack_rows(v1m)
            v2s = _stack_rows(v2m)
            ridx = lax.broadcasted_iota(jnp.int32, (k, LANES), 0)
            cands = [v1m[a] + v2s for a in range(4)]
            cands += [jnp.where(ridx >= 4, v2m[b] + v1s, NEG_INF) for b in range(4)]
            work = cands
            tops = []
            for r in range(k + 1):
                mm = work[0]
                for cnd in work[1:]:
                    mm = jnp.maximum(mm, cnd)
                tops.append(jnp.max(mm, axis=0, keepdims=True))
                if r < k:
                    work = [jnp.where(cnd >= tops[-1], NEG_INF, cnd) for cnd in work]
            thr = 0.5 * (tops[k - 1] + tops[k])
            zsum = jnp.zeros((1, LANES), F32)
            for cnd in cands:
                zsum = zsum + jnp.sum(jnp.where(cnd >= thr, jnp.exp(cnd), 0.0), axis=0, keepdims=True)
            s1m = jnp.where(s[0] >= vals[0][k - 1], s[0] - vals[0][0], NEG_INF)
            s2m = jnp.where(s[1] >= vals[1][k - 1], s[1] - vals[1][0], NEG_INF)
            d1_scr[p, lt] = thr - s1m
            e1_scr[p, lt] = jnp.exp(s1m) * (0.5 / zsum)
            s2_scr[p, lt] = s2m
            e2_scr[p, lt] = jnp.exp(s2m)
            return carry

        lax.fori_loop(0, nlt * PEER_HEADS, unit, 0, unroll=2)

    def step(at_r, at_w, g_w, g_r):
        i1_0 = e * PEER_NI1
        mxu_w = 2 * LANES
        for lt in range(nlt):
            tl = slice(lt * LANES, (lt + 1) * LANES)
            if (lt * LANES) % mxu_w == 0:
                ts = slice(lt * LANES, lt * LANES + mxu_w)
                acc_scr[:, ts] += _dot(vt_ref[...], g_r[:, ts])
                at_w[:, ts] = _dot_nt(u_ref[...], h_scr[ts, :])
            for hf in range(PEER_NSUB):
                rows2 = slice(hf * hk, (hf + 1) * hk)
                w = [jnp.zeros((hk, LANES), F32) for _ in range(PEER_NI1)]
                for p in range(PEER_HEADS):
                    s2h = s2_scr[p, lt, rows2, :]
                    e2h = e2_scr[p, lt, rows2, :]
                    for il in range(PEER_NI1):
                        d1 = d1_scr[p, lt, pl.ds(i1_0 + il, 1), :]
                        e1 = e1_scr[p, lt, pl.ds(i1_0 + il, 1), :]
                        w[il] = w[il] + jnp.where(s2h >= d1, e2h, 0.0) * e1
                for il in range(PEER_NI1):
                    rows = slice(il * nk + hf * hk, il * nk + (hf + 1) * hk)
                    a = at_r[rows, tl]
                    u = a * (a * a * (GELU_C * 0.044715) + GELU_C)
                    g_w[rows, tl] = (w[il] * (a + a * jnp.tanh(u))).astype(BF16)

    @pl.when(e % 2 == 0)
    def _even():
        step(at0_scr, at1_scr, g0_scr, g1_scr)

    @pl.when(e % 2 == 1)
    def _odd():
        step(at1_scr, at0_scr, g1_scr, g0_scr)

    @pl.when(e == n_eblk - 1)
    def _finish():
        g_last = g1_scr if (n_eblk - 1) % 2 else g0_scr
        y = (acc_scr[...] + _dot(vtl_ref[...], g_last[...])).T
        gate = m_ref[5:6, :]
        r = DEEPNORM_ALPHA * x_ref[...] + gate * y
        out_ref[...] = _layer_norm(r, lg_ref[...], lb_ref[...])


def _peer(x, mod, wq_bf, keys_bf, u_bf, vt_bf, ln_g, ln_b, layer, grp):
    nt, d = x.shape
    t = PEER_T
    nlt = t // LANES
    eblk = PEER_NI1 * PEER_NKEYS
    n_eblk = u_bf.shape[0] // eblk
    nq = wq_bf.shape[1]
    sel_shape = (PEER_HEADS, nlt, PEER_NKEYS, LANES)
    return pl.pallas_call(
        functools.partial(_peer_kernel, n_eblk=n_eblk),
        grid=(nt // t, n_eblk),
        in_specs=[
            pl.BlockSpec((t, d), lambda b, e: (b, 0)),
            pl.BlockSpec((None, None, 6, d), lambda b, e: (layer, grp(b, t), 0, 0)),
            pl.BlockSpec((d, nq), lambda b, e: (0, 0)),
            pl.BlockSpec((PEER_HEADS, 2, PEER_NKEYS, PEER_DHALF), lambda b, e: (0, 0, 0, 0)),
            pl.BlockSpec((eblk, d), lambda b, e: (0, 0)),
            pl.BlockSpec((eblk, d), lambda b, e: (jnp.minimum(e + 1, n_eblk - 1), 0)),
            pl.BlockSpec((d, eblk), lambda b, e: (0, jnp.maximum(e - 1, 0))),
            pl.BlockSpec((d, eblk), lambda b, e: (0, n_eblk - 1)),
            pl.BlockSpec((None, 1, d), lambda b, e: (2 * layer + 1, 0, 0)),
            pl.BlockSpec((None, 1, d), lambda b, e: (2 * layer + 1, 0, 0)),
        ],
        out_specs=pl.BlockSpec((t, d), lambda b, e: (b, 0)),
        out_shape=jax.ShapeDtypeStruct((nt, d), F32),
        scratch_shapes=[
            pltpu.VMEM((t, d), BF16),
            pltpu.VMEM((2 * PEER_HEADS, t, PEER_DHALF), BF16),
            pltpu.VMEM(sel_shape, F32), pltpu.VMEM(sel_shape, F32),
            pltpu.VMEM(sel_shape, F32), pltpu.VMEM(sel_shape, F32),
            pltpu.VMEM((eblk, t), F32), pltpu.VMEM((eblk, t), F32),
            pltpu.VMEM((eblk, t), BF16), pltpu.VMEM((eblk, t), BF16),
            pltpu.VMEM((d, t), F32),
        ],
        compiler_params=_params(("parallel", "arbitrary")),
        name="peer",
    )(x, mod, wq_bf, keys_bf, u_bf, u_bf, vt_bf, vt_bf, ln_g, ln_b)


def _rope_angles(pos, dim):
    inv = ROPE_BASE ** (-jnp.arange(0, dim, 2, dtype=F32) / dim)
    ang = pos.astype(F32)[:, None] * inv[None, :]
    return jnp.cos(ang), jnp.sin(ang)


def _axial_tables(seq_len):
    t = jnp.arange(seq_len)
    half = ATT_DH // 2
    cr, sr = _rope_angles(t // GRID_W, half)
    cc, sc = _rope_angles(t % GRID_W, half)
    cos = jnp.concatenate([cr, cr, cc, cc], -1)
    sin = jnp.concatenate([-sr, sr, -sc, sc], -1)
    reps = LANES // ATT_DH
    return jnp.tile(cos, (1, reps)), jnp.tile(sin, (1, reps))


def kernel(x_prompt, x_sample, state_ret_fwd, state_ret_bwd, cache_k, cache_v, c, c_ctx, mod_w, mod_b, ln_g, ln_b,
           ret_w_in, ret_w_out, ret_decay, attn_w_in, attn_w_out, attn_sink, peer_wq, peer_keys, peer_u, peer_v):
    nb_p, len_p, d = x_prompt.shape
    nb_s, len_s, _ = x_sample.shape
    n_p = nb_p * len_p
    n_s = nb_s * len_s
    for tile in (PROJ_TM, OUT_TM, PEER_T):
        assert n_p % tile == 0 and len_s % tile == 0

    def grp(blk, tile):
        return _group_index(blk, n_p // tile, len_s // tile)

    x = jnp.concatenate([x_prompt.reshape(n_p, d), x_sample.reshape(n_s, d)], axis=0)
    n_grp = 1 + nb_s
    n_grp_pad = -(-n_grp // 8) * 8
    cond = jnp.concatenate([c_ctx[None, :], c, jnp.zeros((n_grp_pad - n_grp, d), F32)], axis=0)
    mod = _modulation(cond, mod_w, mod_b).reshape(DEPTH, n_grp_pad, 6, d)

    ln_g3 = ln_g.reshape(DEPTH * 2, 1, d)
    ln_b3 = ln_b.reshape(DEPTH * 2, 1, d)
    ret_rope = _rope_angles(jnp.arange(len_s), RET_DK)
    att_rope = _axial_tables(len_s)
    kv_w = ATT_HKV * ATT_DH
    cache_k4 = cache_k.reshape(cache_k.shape[0], cache_k.shape[1], cache_k.shape[2], kv_w)
    cache_v4 = cache_v.reshape(cache_v.shape[0], cache_v.shape[1], cache_v.shape[2], kv_w)

    new_sf, new_sb, new_k, new_v = [], [], [], []
    for i in range(DEPTH):
        j = i // N_MIXERS
        if i % N_MIXERS == 0:
            z = _project(x, mod, ret_w_in[j].astype(BF16), i, 0, grp)
            tables = _ret_tables(ret_decay[j])
            o_p, sf, sb = _retention(z, tables, 0, nb_p, len_p, False)
            o_s = _retention(z, tables, n_p, nb_s, len_s, True, rope=ret_rope,
                             states=(state_ret_fwd, state_ret_bwd), j=j)
            new_sf.append(sf)
            new_sb.append(sb)
            o = jnp.concatenate([o_p, o_s], axis=0)
            x = _ret_out(o, z, x, mod, ret_w_out[j].astype(BF16), ln_g3, ln_b3, i, grp)
        else:
            z = _project(x, mod, attn_w_in[j].astype(BF16), i, 0, grp)
            sink = attn_sink[j].astype(F32)
            o_p = _attn_context(z, sink, nb_p, len_p)
            o_s = _attn_latent(z, sink, cache_k4, cache_v4, att_rope, n_p, nb_s, len_s, j)
            hq_w = ATT_HQ * ATT_DH
            new_k.append(z[:n_p, hq_w:hq_w + kv_w].reshape(nb_p, len_p, ATT_HKV, ATT_DH))
            new_v.append(z[:n_p, hq_w + kv_w:hq_w + 2 * kv_w].reshape(nb_p, len_p, ATT_HKV, ATT_DH))
            o = jnp.concatenate([o_p, o_s], axis=0)
            x = _attn_out(o, x, mod, attn_w_out[j].astype(BF16), ln_g3, ln_b3, i, grp)
        x = _peer(x, mod, peer_wq[i].astype(BF16), peer_keys[i].astype(BF16), peer_u[i].astype(BF16),
                  peer_v[i].T.astype(BF16), ln_g3, ln_b3, i, grp)

    y_p = x[:n_p].reshape(nb_p, len_p, d)
    y_s = x[n_p:].reshape(nb_s, len_s, d)
    return (y_p, y_s, jnp.stack(new_sf, 1), jnp.stack(new_sb, 1), jnp.stack(new_k, 1), jnp.stack(new_v, 1))
```

```python
import functools
import math

import jax
import jax.numpy as jnp
from jax import lax
from jax.experimental import pallas as pl
from jax.experimental.pallas import tpu as pltpu

F32 = jnp.float32
BF16 = jnp.bfloat16

DEPTH = 4
N_MIXERS = 2
GRID_W = 64
RET_HEADS = 4
RET_DK = 256
RET_DV = 512
RET_CHUNK = 128
ATT_HQ = 16
ATT_HKV = 4
ATT_G = ATT_HQ // ATT_HKV
ATT_DH = 64
ATT_BLK = 128
ROPE_BASE = 10000.0
PEER_HEADS = 8
PEER_NKEYS = 128
PEER_DHALF = 128
PEER_TOPK = 16
DEEPNORM_ALPHA = (2.0 * DEPTH) ** 0.25
LN_EPS = 1e-5
GN_EPS = 1e-5
NEG_INF = -1e30
GELU_C = math.sqrt(2.0 / math.pi)

LANES = 128
VMEM_LIMIT = 56 * 1024 * 1024
PROJ_TM = 512
OUT_TM = 256
PEER_T = 512
PEER_NI1 = 8
PEER_ILG = 4
PEER_NSUB = 2


def _dot(a, b):
    return jnp.dot(a, b, preferred_element_type=F32)


def _dot_nt(a, b):
    return lax.dot_general(a, b, (((1,), (1,)), ((), ())), preferred_element_type=F32)


def _params(sem, flags=None):
    return pltpu.CompilerParams(dimension_semantics=sem, vmem_limit_bytes=VMEM_LIMIT, flags=flags)


def _group_index(blk, n_prompt_blocks, blocks_per_seq):
    return jnp.where(blk < n_prompt_blocks, 0, 1 + (blk - n_prompt_blocks) // blocks_per_seq)


def _layer_norm(r, g, b):
    mu = jnp.mean(r, -1, keepdims=True)
    d = r - mu
    var = jnp.mean(d * d, -1, keepdims=True)
    return d * lax.rsqrt(var + LN_EPS) * g + b


def _mod_kernel(c_ref, w_ref, b_ref, o_ref):
    c = c_ref[...]
    a = (c * jax.nn.sigmoid(c)).astype(BF16)
    o_ref[...] = _dot(a, w_ref[...].astype(BF16)) + b_ref[...]


def _modulation(cond, mod_w, mod_b):
    ngp, d = cond.shape
    n_out = mod_w.shape[-1]
    tn = 1536
    return pl.pallas_call(
        _mod_kernel,
        grid=(DEPTH, n_out // tn),
        in_specs=[
            pl.BlockSpec((ngp, d), lambda i, n: (0, 0)),
            pl.BlockSpec((None, d, tn), lambda i, n: (i, 0, n)),
            pl.BlockSpec((None, 1, tn), lambda i, n: (i, 0, n)),
        ],
        out_specs=pl.BlockSpec((None, ngp, tn), lambda i, n: (i, 0, n)),
        out_shape=jax.ShapeDtypeStruct((DEPTH, ngp, n_out), F32),
        compiler_params=_params(("parallel", "parallel")),
        name="modulation",
    )(cond, mod_w, mod_b.reshape(DEPTH, 1, n_out))


def _proj_kernel(x_ref, m_ref, w_ref, o_ref, *, off):
    sh = m_ref[off:off + 1, :]
    sc = m_ref[off + 1:off + 2, :]
    h = (x_ref[...] * (1.0 + sc) + sh).astype(BF16)
    o_ref[...] = _dot(h, w_ref[...])


def _project(x, mod, w_bf, layer, off, grp):
    nt, d = x.shape
    n = w_bf.shape[1]
    tn = 1536
    tm = PROJ_TM
    return pl.pallas_call(
        functools.partial(_proj_kernel, off=off),
        grid=(n // tn, nt // tm),
        in_specs=[
            pl.BlockSpec((tm, d), lambda j, b: (b, 0)),
            pl.BlockSpec((None, None, 6, d), lambda j, b: (layer, grp(b, tm), 0, 0)),
            pl.BlockSpec((d, tn), lambda j, b: (0, j)),
        ],
        out_specs=pl.BlockSpec((tm, tn), lambda j, b: (b, j)),
        out_shape=jax.ShapeDtypeStruct((nt, n), F32),
        compiler_params=_params(("parallel", "parallel")),
        name="mod_proj",
    )(x, mod, w_bf)


def _rope_half(x, cos, sin):
    x1 = x[:, :LANES]
    x2 = x[:, LANES:]
    return jnp.concatenate([x1 * cos - x2 * sin, x1 * sin + x2 * cos], axis=1)


def _ret_kernel(*refs, latent, nc):
    it = iter(refs)
    qa, ka, va, qb, kb, vb = (next(it) for _ in range(6))
    if latent:
        cosa, sina, cosb, sinb = (next(it) for _ in range(4))
    dfb, qdf, qdb, kdf, kdb, cdf, cdb = (next(it) for _ in range(7))
    if latent:
        s0f, s0b = next(it), next(it)
    o_ref = next(it)
    if not latent:
        sf_out, sb_out = next(it), next(it)
    sf, sb = next(it), next(it)

    c = pl.program_id(2)

    @pl.when(c == 0)
    def _init():
        o_ref[...] = jnp.zeros(o_ref.shape, F32)
        if latent:
            sf[...] = s0f[...]
            sb[...] = s0b[...]
        else:
            sf[...] = jnp.zeros(sf.shape, F32)
            sb[...] = jnp.zeros(sb.shape, F32)

    def prep(q_ref, k_ref, cos_ref, sin_ref):
        q = q_ref[...]
        k = k_ref[...] * (RET_DK ** -0.5)
        if latent:
            cos = cos_ref[...]
            sin = sin_ref[...]
            q = _rope_half(q, cos, sin)
            k = _rope_half(k, cos, sin)
        return q, k

    q, k = prep(qa, ka, cosa if latent else None, sina if latent else None)
    v = va[...].astype(BF16)
    att = (_dot_nt(q.astype(BF16), k.astype(BF16)) * dfb[...]).astype(BF16)
    o_c = _dot(att, v) + _dot((q * qdf[...]).astype(BF16), sf[...].astype(BF16))
    rows_c = pl.ds(pl.multiple_of(c * RET_CHUNK, RET_CHUNK), RET_CHUNK)
    o_ref[rows_c, :] += o_c
    kd_t = (k * kdf[...]).T.astype(BF16)
    sf[...] = sf[...] * cdf[...] + _dot(kd_t, v)

    q2, k2 = prep(qb, kb, cosb if latent else None, sinb if latent else None)
    v2 = vb[...].astype(BF16)
    o_b = _dot((q2 * qdb[...]).astype(BF16), sb[...].astype(BF16))
    rows_b = pl.ds(pl.multiple_of((nc - 1 - c) * RET_CHUNK, RET_CHUNK), RET_CHUNK)
    o_ref[rows_b, :] += o_b
    kd2_t = (k2 * kdb[...]).T.astype(BF16)
    sb[...] = sb[...] * cdb[...] + _dot(kd2_t, v2)

    if not latent:
        @pl.when(c == nc - 1)
        def _fin():
            sf_out[...] = sf[...]
            sb_out[...] = sb[...]


def _ret_tables(decay):
    log_g = jax.nn.log_sigmoid(decay.astype(F32))
    lf, lb = log_g[0], log_g[1]
    idx = jnp.arange(RET_CHUNK, dtype=F32)
    diff = idx[:, None] - idx[None, :]
    fmask = diff >= 0
    bmask = diff < 0
    df = jnp.where(fmask[None], jnp.exp(jnp.where(fmask, diff, 0.0)[None] * lf[:, None, None]), 0.0)
    db = jnp.where(bmask[None], jnp.exp(jnp.where(bmask, -diff, 0.0)[None] * lb[:, None, None]), 0.0)
    dfb = df + db
    def rows(e):
        return jnp.broadcast_to(e[:, :, None], (RET_HEADS, RET_CHUNK, RET_DK))
    qdf = rows(jnp.exp((idx + 1.0)[None, :] * lf[:, None]))
    kdf = rows(jnp.exp((RET_CHUNK - 1.0 - idx)[None, :] * lf[:, None]))
    qdb = rows(jnp.exp((RET_CHUNK - idx)[None, :] * lb[:, None]))
    kdb = rows(jnp.exp(idx[None, :] * lb[:, None]))
    cdf = jnp.broadcast_to(jnp.exp(RET_CHUNK * lf)[:, None, None], (RET_HEADS, 1, RET_DV))
    cdb = jnp.broadcast_to(jnp.exp(RET_CHUNK * lb)[:, None, None], (RET_HEADS, 1, RET_DV))
    return dfb, qdf, qdb, kdf, kdb, cdf, cdb


def _retention(z, tables, row0, nseq, seq_len, latent, rope=None, states=None, j=0):
    nc = seq_len // RET_CHUNK
    base = row0 // RET_CHUNK
    h_ = RET_HEADS
    kcol = (h_ * RET_DK) // RET_DK
    vcol = (2 * h_ * RET_DK) // RET_DV

    def fw(b, h, c):
        return base + b * nc + c

    def bw(b, h, c):
        return base + b * nc + (nc - 1 - c)

    in_specs = [
        pl.BlockSpec((RET_CHUNK, RET_DK), lambda b, h, c: (fw(b, h, c), h)),
        pl.BlockSpec((RET_CHUNK, RET_DK), lambda b, h, c: (fw(b, h, c), kcol + h)),
        pl.BlockSpec((RET_CHUNK, RET_DV), lambda b, h, c: (fw(b, h, c), vcol + h)),
        pl.BlockSpec((RET_CHUNK, RET_DK), lambda b, h, c: (bw(b, h, c), h)),
        pl.BlockSpec((RET_CHUNK, RET_DK), lambda b, h, c: (bw(b, h, c), kcol + h)),
        pl.BlockSpec((RET_CHUNK, RET_DV), lambda b, h, c: (bw(b, h, c), vcol + h)),
    ]
    args = [z, z, z, z, z, z]
    if latent:
        cos, sin = rope
        in_specs += [
            pl.BlockSpec((RET_CHUNK, LANES), lambda b, h, c: (c, 0)),
            pl.BlockSpec((RET_CHUNK, LANES), lambda b, h, c: (c, 0)),
            pl.BlockSpec((RET_CHUNK, LANES), lambda b, h, c: (nc - 1 - c, 0)),
            pl.BlockSpec((RET_CHUNK, LANES), lambda b, h, c: (nc - 1 - c, 0)),
        ]
        args += [cos, sin, cos, sin]
    dfb, qdf, qdb, kdf, kdb, cdf, cdb = tables
    in_specs += [pl.BlockSpec((None, RET_CHUNK, RET_CHUNK), lambda b, h, c: (h, 0, 0))]
    in_specs += [pl.BlockSpec((None, RET_CHUNK, RET_DK), lambda b, h, c: (h, 0, 0))] * 4
    in_specs += [pl.BlockSpec((None, 1, RET_DV), lambda b, h, c: (h, 0, 0))] * 2
    args += [dfb, qdf, qdb, kdf, kdb, cdf, cdb]
    if latent:
        s0f, s0b = states
        st_spec = pl.BlockSpec((None, None, None, RET_DK, RET_DV), lambda b, h, c: (b, j, h, 0, 0))
        in_specs += [st_spec, st_spec]
        args += [s0f, s0b]

    o_shape = jax.ShapeDtypeStruct((nseq * seq_len, h_ * RET_DV), F32)
    o_spec = pl.BlockSpec((seq_len, RET_DV), lambda b, h, c: (b, h))
    if latent:
        out_shape, out_specs = o_shape, o_spec
    else:
        s_shape = jax.ShapeDtypeStruct((nseq, h_, RET_DK, RET_DV), F32)
        s_spec = pl.BlockSpec((None, None, RET_DK, RET_DV), lambda b, h, c: (b, h, 0, 0))
        out_shape, out_specs = (o_shape, s_shape, s_shape), (o_spec, s_spec, s_spec)

    return pl.pallas_call(
        functools.partial(_ret_kernel, latent=latent, nc=nc),
        grid=(nseq, h_, nc),
        in_specs=in_specs,
        out_specs=out_specs,
        out_shape=out_shape,
        scratch_shapes=[pltpu.VMEM((RET_DK, RET_DV), F32), pltpu.VMEM((RET_DK, RET_DV), F32)],
        compiler_params=_params(("parallel", "parallel", "arbitrary")),
        name="retention_latent" if latent else "retention_context",
    )(*args)


def _ret_out_kernel(o_ref, g_ref, x_ref, m_ref, w_ref, lg_ref, lb_ref, out_ref):
    y = None
    for h in range(RET_HEADS):
        cols = slice(h * RET_DV, (h + 1) * RET_DV)
        oh = o_ref[:, cols]
        mu = jnp.mean(oh, -1, keepdims=True)
        d = oh - mu
        var = jnp.mean(d * d, -1, keepdims=True)
        on = d * lax.rsqrt(var + GN_EPS)
        gh = g_ref[:, cols]
        act = (gh * jax.nn.sigmoid(gh) * on).astype(BF16)
        part = _dot(act, w_ref[cols, :])
        y = part if y is None else y + part
    gate = m_ref[2:3, :]
    r = DEEPNORM_ALPHA * x_ref[...] + gate * y
    out_ref[...] = _layer_norm(r, lg_ref[...], lb_ref[...])


def _ret_out(o, z, x, mod, w_bf, ln_g, ln_b, layer, grp):
    nt, d = x.shape
    tm = OUT_TM
    hv = RET_HEADS * RET_DV
    gcol = (2 * RET_HEADS * RET_DK + hv) // hv
    return pl.pallas_call(
        _ret_out_kernel,
        grid=(nt // tm,),
        in_specs=[
            pl.BlockSpec((tm, hv), lambda b: (b, 0)),
            pl.BlockSpec((tm, hv), lambda b: (b, gcol)),
            pl.BlockSpec((tm, d), lambda b: (b, 0)),
            pl.BlockSpec((None, None, 6, d), lambda b: (layer, grp(b, tm), 0, 0)),
            pl.BlockSpec((hv, d), lambda b: (0, 0)),
            pl.BlockSpec((None, 1, d), lambda b: (2 * layer, 0, 0)),
            pl.BlockSpec((None, 1, d), lambda b: (2 * layer, 0, 0)),
        ],
        out_specs=pl.BlockSpec((tm, d), lambda b: (b, 0)),
        out_shape=jax.ShapeDtypeStruct((nt, d), F32),
        compiler_params=_params(("parallel",)),
        name="retention_out",
    )(o, z, x, mod, w_bf, ln_g, ln_b)


def _softmax_pv(parts, sink):
    m = sink
    for s, _ in parts:
        m = jnp.maximum(m, jnp.max(s, -1, keepdims=True))
    den = jnp.exp(sink - m)
    o = None
    for s, v in parts:
        p = jnp.exp(s - m)
        den = den + jnp.sum(p, -1, keepdims=True)
        pv = _dot(p.astype(BF16), v)
        o = pv if o is None else o + pv
    return o / den


def _attn_ctx_kernel(sink_ref, q_ref, k_ref, v_ref, o_ref):
    scale = ATT_DH ** -0.5
    for h in range(ATT_HKV):
        kh = k_ref[:, h * ATT_DH:(h + 1) * ATT_DH].astype(BF16)
        vh = v_ref[:, h * ATT_DH:(h + 1) * ATT_DH].astype(BF16)
        for g in range(ATT_G):
            hq = h * ATT_G + g
            cols = slice(hq * ATT_DH, (hq + 1) * ATT_DH)
            qh = q_ref[:, cols].astype(BF16)
            s = _dot_nt(qh, kh) * scale
            o_ref[:, cols] = _softmax_pv([(s, vh)], sink_ref[hq])


def _attn_context(z, sink, nseq, seq_len):
    hq_w = ATT_HQ * ATT_DH
    kv_w = ATT_HKV * ATT_DH
    return pl.pallas_call(
        _attn_ctx_kernel,
        grid=(nseq,),
        in_specs=[
            pl.BlockSpec(memory_space=pltpu.SMEM),
            pl.BlockSpec((seq_len, hq_w), lambda b: (b, 0)),
            pl.BlockSpec((seq_len, kv_w), lambda b: (b, hq_w // kv_w)),
            pl.BlockSpec((seq_len, kv_w), lambda b: (b, hq_w // kv_w + 1)),
        ],
        out_specs=pl.BlockSpec((seq_len, hq_w), lambda b: (b, 0)),
        out_shape=jax.ShapeDtypeStruct((nseq * seq_len, hq_w), F32),
        compiler_params=_params(("parallel",)),
        name="attention_context",
    )(sink, z, z, z)


def _rope_axial(x, cos, sin):
    w = x.shape[1]
    lane = lax.broadcasted_iota(jnp.int32, (x.shape[0], LANES), 1)
    first = (lane % 32) < 16
    outs = []
    for cg in range(w // LANES):
        xg = x[:, cg * LANES:(cg + 1) * LANES]
        up = pltpu.roll(xg, LANES - 16, 1)
        dn = pltpu.roll(xg, 16, 1)
        outs.append(xg * cos + jnp.where(first, up, dn) * sin)
    return jnp.concatenate(outs, axis=1) if len(outs) > 1 else outs[0]


def _attn_lat_kernel(sink_ref, q_ref, kp_ref, kc_ref, kn_ref, vp_ref, vc_ref, vn_ref, ck_ref, cv_ref,
                     cq_ref, sq_ref, cp_ref, sp_ref, cn_ref, sn_ref, o_ref, *, nb):
    scale = ATT_DH ** -0.5
    qb = pl.program_id(1)
    q = _rope_axial(q_ref[...], cq_ref[...], sq_ref[...])
    kw = jnp.concatenate([
        _rope_axial(kp_ref[...], cp_ref[...], sp_ref[...]),
        _rope_axial(kc_ref[...], cq_ref[...], sq_ref[...]),
        _rope_axial(kn_ref[...], cn_ref[...], sn_ref[...]),
    ], axis=0)
    vw = jnp.concatenate([vp_ref[...], vc_ref[...], vn_ref[...]], axis=0)
    ck = ck_ref[...]
    cv = cv_ref[...]

    i = lax.broadcasted_iota(jnp.int32, (ATT_BLK, 3 * ATT_BLK), 0)
    jj = lax.broadcasted_iota(jnp.int32, (ATT_BLK, 3 * ATT_BLK), 1)
    lo = jnp.where(qb > 0, 0, ATT_BLK)
    hi = jnp.where(qb < nb - 1, 3 * ATT_BLK, 2 * ATT_BLK)
    valid = (jj >= jnp.maximum(i, lo)) & (jj < jnp.minimum(i + 2 * ATT_BLK + 1, hi))

    for h in range(ATT_HKV):
        hs = slice(h * ATT_DH, (h + 1) * ATT_DH)
        kh = kw[:, hs].astype(BF16)
        vh = vw[:, hs].astype(BF16)
        ckh = ck[:, hs].astype(BF16)
        cvh = cv[:, hs].astype(BF16)
        for g in range(ATT_G):
            hq = h * ATT_G + g
            cols = slice(hq * ATT_DH, (hq + 1) * ATT_DH)
            qh = q[:, cols].astype(BF16)
            s_loc = jnp.where(valid, _dot_nt(qh, kh) * scale, NEG_INF)
            s_ctx = _dot_nt(qh, ckh) * scale
            o_ref[:, cols] = _softmax_pv([(s_loc, vh), (s_ctx, cvh)], sink_ref[hq])


def _attn_latent(z, sink, cache_k, cache_v, rope, row0, nseq, seq_len, j):
    nb = seq_len // ATT_BLK
    base = row0 // ATT_BLK
    hq_w = ATT_HQ * ATT_DH
    kv_w = ATT_HKV * ATT_DH
    kcol = hq_w // kv_w
    past = cache_k.shape[2]
    cos, sin = rope

    def prv(q):
        return jnp.maximum(q - 1, 0)

    def nxt(q):
        return jnp.minimum(q + 1, nb - 1)

    def kv_spec(sel, col):
        return pl.BlockSpec((ATT_BLK, kv_w), lambda b, q: (base + b * nb + sel(q), col))

    def rope_spec(sel):
        return pl.BlockSpec((ATT_BLK, LANES), lambda b, q: (sel(q), 0))

    same = lambda q: q
    cache_spec = pl.BlockSpec((None, None, past, kv_w), lambda b, q: (b, j, 0, 0))
    return pl.pallas_call(
        functools.partial(_attn_lat_kernel, nb=nb),
        grid=(nseq, nb),
        in_specs=[
            pl.BlockSpec(memory_space=pltpu.SMEM),
            pl.BlockSpec((ATT_BLK, hq_w), lambda b, q: (base + b * nb + q, 0)),
            kv_spec(prv, kcol), kv_spec(same, kcol), kv_spec(nxt, kcol),
            kv_spec(prv, kcol + 1), kv_spec(same, kcol + 1), kv_spec(nxt, kcol + 1),
            cache_spec, cache_spec,
            rope_spec(same), rope_spec(same), rope_spec(prv), rope_spec(prv), rope_spec(nxt), rope_spec(nxt),
        ],
        out_specs=pl.BlockSpec((ATT_BLK, hq_w), lambda b, q: (b * nb + q, 0)),
        out_shape=jax.ShapeDtypeStruct((nseq * seq_len, hq_w), F32),
        compiler_params=_params(("parallel", "parallel")),
        name="attention_latent",
    )(sink, z, z, z, z, z, z, z, cache_k, cache_v, cos, sin, cos, sin, cos, sin)


def _attn_out_kernel(o_ref, x_ref, m_ref, w_ref, lg_ref, lb_ref, out_ref):
    y = _dot(o_ref[...].astype(BF16), w_ref[...])
    gate = m_ref[2:3, :]
    r = DEEPNORM_ALPHA * x_ref[...] + gate * y
    out_ref[...] = _layer_norm(r, lg_ref[...], lb_ref[...])


def _attn_out(o, x, mod, w_bf, ln_g, ln_b, layer, grp):
    nt, d = x.shape
    tm = OUT_TM
    k = o.shape[1]
    return pl.pallas_call(
        _attn_out_kernel,
        grid=(nt // tm,),
        in_specs=[
            pl.BlockSpec((tm, k), lambda b: (b, 0)),
            pl.BlockSpec((tm, d), lambda b: (b, 0)),
            pl.BlockSpec((None, None, 6, d), lambda b: (layer, grp(b, tm), 0, 0)),
            pl.BlockSpec((k, d), lambda b: (0, 0)),
            pl.BlockSpec((None, 1, d), lambda b: (2 * layer, 0, 0)),
            pl.BlockSpec((None, 1, d), lambda b: (2 * layer, 0, 0)),
        ],
        out_specs=pl.BlockSpec((tm, d), lambda b: (b, 0)),
        out_shape=jax.ShapeDtypeStruct((nt, d), F32),
        compiler_params=_params(("parallel",)),
        name="attention_out",
    )(o, x, mod, w_bf, ln_g, ln_b)


def _top_values(s, k):
    vals = []
    work = s
    for r in range(k):
        m = jnp.max(work, axis=0, keepdims=True)
        vals.append(m)
        if r + 1 < k:
            work = jnp.where(work >= m, NEG_INF, work)
    return vals


def _stack_rows(rows_list):
    n = len(rows_list)
    ridx = lax.broadcasted_iota(jnp.int32, (n, rows_list[0].shape[1]), 0)
    out = jnp.zeros((n, rows_list[0].shape[1]), F32)
    for r, row in enumerate(rows_list):
        out = jnp.where(ridx == r, row, out)
    return out


def _peer_kernel(x_ref, m_ref, wq_ref, keys_ref, u0_ref, u_ref, vt_ref, lg_ref, lb_ref, out_ref,
                 h_scr, q_scr, d1_scr, e1_scr, s2_scr, e2_scr, at0_scr, at1_scr, g_scr, acc_scr,
                 *, n_eblk):
    t = x_ref.shape[0]
    nlt = t // LANES
    e = pl.program_id(1)
    k = PEER_TOPK
    nk = PEER_NKEYS
    hk = nk // PEER_NSUB

    @pl.when(e == 0)
    def _scores():
        sh = m_ref[3:4, :]
        sc = m_ref[4:5, :]
        h_scr[...] = (x_ref[...] * (1.0 + sc) + sh).astype(BF16)
        q = _dot(h_scr[...], wq_ref[...])
        for ph in range(2 * PEER_HEADS):
            q_scr[ph] = q[:, ph * PEER_DHALF:(ph + 1) * PEER_DHALF].astype(BF16)
        acc_scr[...] = jnp.zeros(acc_scr.shape, F32)
        at0_scr[...] = _dot_nt(u0_ref[...], h_scr[...])

        def unit(uidx, carry):
            lt = uidx // PEER_HEADS
            p = uidx % PEER_HEADS
            t0 = pl.multiple_of(lt * LANES, LANES)
            s, vals = [], []
            for hh in range(2):
                sk = _dot_nt(keys_ref[p, hh], q_scr[2 * p + hh, pl.ds(t0, LANES), :])
                s.append(sk)
                vals.append(_top_values(sk, k))
            v1m = [v - vals[0][0] for v in vals[0]]
            v2m = [v - vals[1][0] for v in vals[1]]
            v1s = _stack_rows(v1m)
            v2s = _stack_rows(v2m)
            ridx = lax.broadcasted_iota(jnp.int32, (k, LANES), 0)
            cands = [v1m[a] + v2s for a in range(4)]
            cands += [jnp.where(ridx >= 4, v2m[b] + v1s, NEG_INF) for b in range(4)]
            work = cands
            tops = []
            for r in range(k + 1):
                mm = work[0]
                for cnd in work[1:]:
                    mm = jnp.maximum(mm, cnd)
                tops.append(jnp.max(mm, axis=0, keepdims=True))
                if r < k:
                    work = [jnp.where(cnd >= tops[-1], NEG_INF, cnd) for cnd in work]
            thr = 0.5 * (tops[k - 1] + tops[k])
            zsum = jnp.zeros((1, LANES), F32)
            for cnd in cands:
                zsum = zsum + jnp.sum(jnp.where(cnd >= thr, jnp.exp(cnd), 0.0), axis=0, keepdims=True)
            s1m = jnp.where(s[0] >= vals[0][k - 1], s[0] - vals[0][0], NEG_INF)
            s2m = jnp.where(s[1] >= vals[1][k - 1], s[1] - vals[1][0], NEG_INF)
            d1_scr[p, lt] = thr - s1m
            e1_scr[p, lt] = jnp.exp(s1m) * (0.5 / zsum)
            s2_scr[p, lt] = s2m
            e2_scr[p, lt] = jnp.exp(s2m)
            return carry

        lax.fori_loop(0, nlt * PEER_HEADS, unit, 0, unroll=2)

    def step(at_r, at_w):
        i1_0 = e * PEER_NI1
        mxu_w = 2 * LANES
        il_grp = PEER_ILG
        for ts_i in range(t // mxu_w):
            ts = slice(ts_i * mxu_w, (ts_i + 1) * mxu_w)
            at_w[:, ts] = _dot_nt(u_ref[...], h_scr[ts, :])
            for ig in range(PEER_NI1 // il_grp):
                for lt in range(ts_i * mxu_w // LANES, (ts_i + 1) * mxu_w // LANES):
                    tl = slice(lt * LANES, (lt + 1) * LANES)
                    for hf in range(PEER_NSUB):
                        rows2 = slice(hf * hk, (hf + 1) * hk)
                        ils = range(ig * il_grp, (ig + 1) * il_grp)
                        w = {il: jnp.zeros((hk, LANES), F32) for il in ils}
                        for p in range(PEER_HEADS):
                            s2h = s2_scr[p, lt, rows2, :]
                            e2h = e2_scr[p, lt, rows2, :]
                            for il in ils:
                                d1 = d1_scr[p, lt, pl.ds(i1_0 + il, 1), :]
                                e1 = e1_scr[p, lt, pl.ds(i1_0 + il, 1), :]
                                w[il] = w[il] + jnp.where(s2h >= d1, e2h, 0.0) * e1
                        for il in ils:
                            rows = slice(il * nk + hf * hk, il * nk + (hf + 1) * hk)
                            a = at_r[rows, tl]
                            u = a * (a * a * (GELU_C * 0.044715) + GELU_C)
                            g_scr[rows, tl] = (w[il] * (a + a * jnp.tanh(u))).astype(BF16)
                ks = slice(ig * il_grp * nk, (ig + 1) * il_grp * nk)
                acc_scr[:, ts] += _dot(vt_ref[:, ks], g_scr[ks, ts])

    @pl.when(e % 2 == 0)
    def _even():
        step(at0_scr, at1_scr)

    @pl.when(e % 2 == 1)
    def _odd():
        step(at1_scr, at0_scr)

    @pl.when(e == n_eblk - 1)
    def _finish():
        y = acc_scr[...].T
        gate = m_ref[5:6, :]
        r = DEEPNORM_ALPHA * x_ref[...] + gate * y
        out_ref[...] = _layer_norm(r, lg_ref[...], lb_ref[...])


def _peer(x, mod, wq_bf, keys_bf, u_bf, vt_bf, ln_g, ln_b, layer, grp):
    nt, d = x.shape
    t = PEER_T
    nlt = t // LANES
    eblk = PEER_NI1 * PEER_NKEYS
    n_eblk = u_bf.shape[0] // eblk
    nq = wq_bf.shape[1]
    sel_shape = (PEER_HEADS, nlt, PEER_NKEYS, LANES)
    return pl.pallas_call(
        functools.partial(_peer_kernel, n_eblk=n_eblk),
        grid=(nt // t, n_eblk),
        in_specs=[
            pl.BlockSpec((t, d), lambda b, e: (b, 0)),
            pl.BlockSpec((None, None, 6, d), lambda b, e: (layer, grp(b, t), 0, 0)),
            pl.BlockSpec((d, nq), lambda b, e: (0, 0)),
            pl.BlockSpec((PEER_HEADS, 2, PEER_NKEYS, PEER_DHALF), lambda b, e: (0, 0, 0, 0)),
            pl.BlockSpec((eblk, d), lambda b, e: (0, 0)),
            pl.BlockSpec((eblk, d), lambda b, e: (jnp.minimum(e + 1, n_eblk - 1), 0)),
            pl.BlockSpec((d, eblk), lambda b, e: (0, e)),
            pl.BlockSpec((None, 1, d), lambda b, e: (2 * layer + 1, 0, 0)),
            pl.BlockSpec((None, 1, d), lambda b, e: (2 * layer + 1, 0, 0)),
        ],
        out_specs=pl.BlockSpec((t, d), lambda b, e: (b, 0)),
        out_shape=jax.ShapeDtypeStruct((nt, d), F32),
        scratch_shapes=[
            pltpu.VMEM((t, d), BF16),
            pltpu.VMEM((2 * PEER_HEADS, t, PEER_DHALF), BF16),
            pltpu.VMEM(sel_shape, F32), pltpu.VMEM(sel_shape, F32),
            pltpu.VMEM(sel_shape, F32), pltpu.VMEM(sel_shape, F32),
            pltpu.VMEM((eblk, t), F32), pltpu.VMEM((eblk, t), F32),
            pltpu.VMEM((eblk, t), BF16),
            pltpu.VMEM((d, t), F32),
        ],
        compiler_params=_params(("parallel", "arbitrary")),
        name="peer",
    )(x, mod, wq_bf, keys_bf, u_bf, u_bf, vt_bf, ln_g, ln_b)


def _rope_angles(pos, dim):
    inv = ROPE_BASE ** (-jnp.arange(0, dim, 2, dtype=F32) / dim)
    ang = pos.astype(F32)[:, None] * inv[None, :]
    return jnp.cos(ang), jnp.sin(ang)


def _axial_tables(seq_len):
    t = jnp.arange(seq_len)
    half = ATT_DH // 2
    cr, sr = _rope_angles(t // GRID_W, half)
    cc, sc = _rope_angles(t % GRID_W, half)
    cos = jnp.concatenate([cr, cr, cc, cc], -1)
    sin = jnp.concatenate([-sr, sr, -sc, sc], -1)
    reps = LANES // ATT_DH
    return jnp.tile(cos, (1, reps)), jnp.tile(sin, (1, reps))


def kernel(x_prompt, x_sample, state_ret_fwd, state_ret_bwd, cache_k, cache_v, c, c_ctx, mod_w, mod_b, ln_g, ln_b,
           ret_w_in, ret_w_out, ret_decay, attn_w_in, attn_w_out, attn_sink, peer_wq, peer_keys, peer_u, peer_v):
    nb_p, len_p, d = x_prompt.shape
    nb_s, len_s, _ = x_sample.shape
    n_p = nb_p * len_p
    n_s = nb_s * len_s
    for tile in (PROJ_TM, OUT_TM, PEER_T):
        assert n_p % tile == 0 and len_s % tile == 0

    def grp(blk, tile):
        return _group_index(blk, n_p // tile, len_s // tile)

    x = jnp.concatenate([x_prompt.reshape(n_p, d), x_sample.reshape(n_s, d)], axis=0)
    n_grp = 1 + nb_s
    n_grp_pad = -(-n_grp // 8) * 8
    cond = jnp.concatenate([c_ctx[None, :], c, jnp.zeros((n_grp_pad - n_grp, d), F32)], axis=0)
    mod = _modulation(cond, mod_w, mod_b).reshape(DEPTH, n_grp_pad, 6, d)

    ln_g3 = ln_g.reshape(DEPTH * 2, 1, d)
    ln_b3 = ln_b.reshape(DEPTH * 2, 1, d)
    ret_rope = _rope_angles(jnp.arange(len_s), RET_DK)
    att_rope = _axial_tables(len_s)
    kv_w = ATT_HKV * ATT_DH
    cache_k4 = cache_k.reshape(cache_k.shape[0], cache_k.shape[1], cache_k.shape[2], kv_w)
    cache_v4 = cache_v.reshape(cache_v.shape[0], cache_v.shape[1], cache_v.shape[2], kv_w)

    new_sf, new_sb, new_k, new_v = [], [], [], []
    for i in range(DEPTH):
        j = i // N_MIXERS
        if i % N_MIXERS == 0:
            z = _project(x, mod, ret_w_in[j].astype(BF16), i, 0, grp)
            tables = _ret_tables(ret_decay[j])
            o_p, sf, sb = _retention(z, tables, 0, nb_p, len_p, False)
            o_s = _retention(z, tables, n_p, nb_s, len_s, True, rope=ret_rope,
                             states=(state_ret_fwd, state_ret_bwd), j=j)
            new_sf.append(sf)
            new_sb.append(sb)
            o = jnp.concatenate([o_p, o_s], axis=0)
            x = _ret_out(o, z, x, mod, ret_w_out[j].astype(BF16), ln_g3, ln_b3, i, grp)
        else:
            z = _project(x, mod, attn_w_in[j].astype(BF16), i, 0, grp)
            sink = attn_sink[j].astype(F32)
            o_p = _attn_context(z, sink, nb_p, len_p)
            o_s = _attn_latent(z, sink, cache_k4, cache_v4, att_rope, n_p, nb_s, len_s, j)
            hq_w = ATT_HQ * ATT_DH
            new_k.append(z[:n_p, hq_w:hq_w + kv_w].reshape(nb_p, len_p, ATT_HKV, ATT_DH))
            new_v.append(z[:n_p, hq_w + kv_w:hq_w + 2 * kv_w].reshape(nb_p, len_p, ATT_HKV, ATT_DH))
            o = jnp.concatenate([o_p, o_s], axis=0)
            x = _attn_out(o, x, mod, attn_w_out[j].astype(BF16), ln_g3, ln_b3, i, grp)
        x = _peer(x, mod, peer_wq[i].astype(BF16), peer_keys[i].astype(BF16), peer_u[i].astype(BF16),
                  peer_v[i].T.astype(BF16), ln_g3, ln_b3, i, grp)

    y_p = x[:n_p].reshape(nb_p, len_p, d)
    y_s = x[n_p:].reshape(nb_s, len_s, d)
    return (y_p, y_s, jnp.stack(new_sf, 1), jnp.stack(new_sb, 1), jnp.stack(new_k, 1), jnp.stack(new_v, 1))
```

```python
import functools
import math

import jax
import jax.numpy as jnp
from jax import lax
from jax.experimental import pallas as pl
from jax.experimental.pallas import tpu as pltpu

F32 = jnp.float32
BF16 = jnp.bfloat16

DEPTH = 4
N_MIXERS = 2
GRID_W = 64
RET_HEADS = 4
RET_DK = 256
RET_DV = 512
RET_CHUNK = 128
ATT_HQ = 16
ATT_HKV = 4
ATT_G = ATT_HQ // ATT_HKV
ATT_DH = 64
ATT_BLK = 128
ROPE_BASE = 10000.0
PEER_HEADS = 8
PEER_NKEYS = 128
PEER_DHALF = 128
PEER_TOPK = 16
DEEPNORM_ALPHA = (2.0 * DEPTH) ** 0.25
LN_EPS = 1e-5
GN_EPS = 1e-5
NEG_INF = -1e30
GELU_C = math.sqrt(2.0 / math.pi)

LANES = 128
VMEM_LIMIT = 56 * 1024 * 1024
PROJ_TM = 512
OUT_TM = 256
PEER_T = 512
PEER_NI1 = 8
PEER_ILG = 4
PEER_NSUB = 2


def _dot(a, b):
    return jnp.dot(a, b, preferred_element_type=F32)


def _dot_nt(a, b):
    return lax.dot_general(a, b, (((1,), (1,)), ((), ())), preferred_element_type=F32)


def _params(sem, flags=None):
    return pltpu.CompilerParams(dimension_semantics=sem, vmem_limit_bytes=VMEM_LIMIT, flags=flags)


def _group_index(blk, n_prompt_blocks, blocks_per_seq):
    return jnp.where(blk < n_prompt_blocks, 0, 1 + (blk - n_prompt_blocks) // blocks_per_seq)


def _layer_norm(r, g, b):
    mu = jnp.mean(r, -1, keepdims=True)
    d = r - mu
    var = jnp.mean(d * d, -1, keepdims=True)
    return d * lax.rsqrt(var + LN_EPS) * g + b


def _mod_kernel(c_ref, w_ref, b_ref, o_ref):
    c = c_ref[...]
    a = (c * jax.nn.sigmoid(c)).astype(BF16)
    o_ref[...] = _dot(a, w_ref[...].astype(BF16)) + b_ref[...]


def _modulation(cond, mod_w, mod_b):
    ngp, d = cond.shape
    n_out = mod_w.shape[-1]
    tn = 1536
    return pl.pallas_call(
        _mod_kernel,
        grid=(DEPTH, n_out // tn),
        in_specs=[
            pl.BlockSpec((ngp, d), lambda i, n: (0, 0)),
            pl.BlockSpec((None, d, tn), lambda i, n: (i, 0, n)),
            pl.BlockSpec((None, 1, tn), lambda i, n: (i, 0, n)),
        ],
        out_specs=pl.BlockSpec((None, ngp, tn), lambda i, n: (i, 0, n)),
        out_shape=jax.ShapeDtypeStruct((DEPTH, ngp, n_out), F32),
        compiler_params=_params(("parallel", "parallel")),
        name="modulation",
    )(cond, mod_w, mod_b.reshape(DEPTH, 1, n_out))


def _proj_kernel(x_ref, m_ref, w_ref, o_ref, *, off):
    sh = m_ref[off:off + 1, :]
    sc = m_ref[off + 1:off + 2, :]
    h = (x_ref[...] * (1.0 + sc) + sh).astype(BF16)
    o_ref[...] = _dot(h, w_ref[...])


def _project(x, mod, w_bf, layer, off, grp):
    nt, d = x.shape
    n = w_bf.shape[1]
    tn = 1536
    tm = PROJ_TM
    return pl.pallas_call(
        functools.partial(_proj_kernel, off=off),
        grid=(n // tn, nt // tm),
        in_specs=[
            pl.BlockSpec((tm, d), lambda j, b: (b, 0)),
            pl.BlockSpec((None, None, 6, d), lambda j, b: (layer, grp(b, tm), 0, 0)),
            pl.BlockSpec((d, tn), lambda j, b: (0, j)),
        ],
        out_specs=pl.BlockSpec((tm, tn), lambda j, b: (b, j)),
        out_shape=jax.ShapeDtypeStruct((nt, n), F32),
        compiler_params=_params(("parallel", "parallel")),
        name="mod_proj",
    )(x, mod, w_bf)


def _rope_half(x, cos, sin):
    x1 = x[:, :LANES]
    x2 = x[:, LANES:]
    return jnp.concatenate([x1 * cos - x2 * sin, x1 * sin + x2 * cos], axis=1)


def _ret_kernel(*refs, latent, nc):
    it = iter(refs)
    qa, ka, va, qb, kb, vb = (next(it) for _ in range(6))
    if latent:
        cosa, sina, cosb, sinb = (next(it) for _ in range(4))
    dfb, qdf, qdb, kdf, kdb, cdf, cdb = (next(it) for _ in range(7))
    if latent:
        s0f, s0b = next(it), next(it)
    o_ref = next(it)
    if not latent:
        sf_out, sb_out = next(it), next(it)
    sf, sb = next(it), next(it)

    c = pl.program_id(2)

    @pl.when(c == 0)
    def _init():
        o_ref[...] = jnp.zeros(o_ref.shape, F32)
        if latent:
            sf[...] = s0f[...]
            sb[...] = s0b[...]
        else:
            sf[...] = jnp.zeros(sf.shape, F32)
            sb[...] = jnp.zeros(sb.shape, F32)

    def prep(q_ref, k_ref, cos_ref, sin_ref):
        q = q_ref[...]
        k = k_ref[...] * (RET_DK ** -0.5)
        if latent:
            cos = cos_ref[...]
            sin = sin_ref[...]
            q = _rope_half(q, cos, sin)
            k = _rope_half(k, cos, sin)
        return q, k

    q, k = prep(qa, ka, cosa if latent else None, sina if latent else None)
    v = va[...].astype(BF16)
    att = (_dot_nt(q.astype(BF16), k.astype(BF16)) * dfb[...]).astype(BF16)
    o_c = _dot(att, v) + _dot((q * qdf[...]).astype(BF16), sf[...].astype(BF16))
    rows_c = pl.ds(pl.multiple_of(c * RET_CHUNK, RET_CHUNK), RET_CHUNK)
    o_ref[rows_c, :] += o_c
    kd_t = (k * kdf[...]).T.astype(BF16)
    sf[...] = sf[...] * cdf[...] + _dot(kd_t, v)

    q2, k2 = prep(qb, kb, cosb if latent else None, sinb if latent else None)
    v2 = vb[...].astype(BF16)
    o_b = _dot((q2 * qdb[...]).astype(BF16), sb[...].astype(BF16))
    rows_b = pl.ds(pl.multiple_of((nc - 1 - c) * RET_CHUNK, RET_CHUNK), RET_CHUNK)
    o_ref[rows_b, :] += o_b
    kd2_t = (k2 * kdb[...]).T.astype(BF16)
    sb[...] = sb[...] * cdb[...] + _dot(kd2_t, v2)

    if not latent:
        @pl.when(c == nc - 1)
        def _fin():
            sf_out[...] = sf[...]
            sb_out[...] = sb[...]


def _ret_tables(decay):
    log_g = jax.nn.log_sigmoid(decay.astype(F32))
    lf, lb = log_g[0], log_g[1]
    idx = jnp.arange(RET_CHUNK, dtype=F32)
    diff = idx[:, None] - idx[None, :]
    fmask = diff >= 0
    bmask = diff < 0
    df = jnp.where(fmask[None], jnp.exp(jnp.where(fmask, diff, 0.0)[None] * lf[:, None, None]), 0.0)
    db = jnp.where(bmask[None], jnp.exp(jnp.where(bmask, -diff, 0.0)[None] * lb[:, None, None]), 0.0)
    dfb = df + db
    def rows(e):
        return jnp.broadcast_to(e[:, :, None], (RET_HEADS, RET_CHUNK, RET_DK))
    qdf = rows(jnp.exp((idx + 1.0)[None, :] * lf[:, None]))
    kdf = rows(jnp.exp((RET_CHUNK - 1.0 - idx)[None, :] * lf[:, None]))
    qdb = rows(jnp.exp((RET_CHUNK - idx)[None, :] * lb[:, None]))
    kdb = rows(jnp.exp(idx[None, :] * lb[:, None]))
    cdf = jnp.broadcast_to(jnp.exp(RET_CHUNK * lf)[:, None, None], (RET_HEADS, 1, RET_DV))
    cdb = jnp.broadcast_to(jnp.exp(RET_CHUNK * lb)[:, None, None], (RET_HEADS, 1, RET_DV))
    return dfb, qdf, qdb, kdf, kdb, cdf, cdb


def _retention(z, tables, row0, nseq, seq_len, latent, rope=None, states=None, j=0):
    nc = seq_len // RET_CHUNK
    base = row0 // RET_CHUNK
    h_ = RET_HEADS
    kcol = (h_ * RET_DK) // RET_DK
    vcol = (2 * h_ * RET_DK) // RET_DV

    def fw(b, h, c):
        return base + b * nc + c

    def bw(b, h, c):
        return base + b * nc + (nc - 1 - c)

    in_specs = [
        pl.BlockSpec((RET_CHUNK, RET_DK), lambda b, h, c: (fw(b, h, c), h)),
        pl.BlockSpec((RET_CHUNK, RET_DK), lambda b, h, c: (fw(b, h, c), kcol + h)),
        pl.BlockSpec((RET_CHUNK, RET_DV), lambda b, h, c: (fw(b, h, c), vcol + h)),
        pl.BlockSpec((RET_CHUNK, RET_DK), lambda b, h, c: (bw(b, h, c), h)),
        pl.BlockSpec((RET_CHUNK, RET_DK), lambda b, h, c: (bw(b, h, c), kcol + h)),
        pl.BlockSpec((RET_CHUNK, RET_DV), lambda b, h, c: (bw(b, h, c), vcol + h)),
    ]
    args = [z, z, z, z, z, z]
    if latent:
        cos, sin = rope
        in_specs += [
            pl.BlockSpec((RET_CHUNK, LANES), lambda b, h, c: (c, 0)),
            pl.BlockSpec((RET_CHUNK, LANES), lambda b, h, c: (c, 0)),
            pl.BlockSpec((RET_CHUNK, LANES), lambda b, h, c: (nc - 1 - c, 0)),
            pl.BlockSpec((RET_CHUNK, LANES), lambda b, h, c: (nc - 1 - c, 0)),
        ]
        args += [cos, sin, cos, sin]
    dfb, qdf, qdb, kdf, kdb, cdf, cdb = tables
    in_specs += [pl.BlockSpec((None, RET_CHUNK, RET_CHUNK), lambda b, h, c: (h, 0, 0))]
    in_specs += [pl.BlockSpec((None, RET_CHUNK, RET_DK), lambda b, h, c: (h, 0, 0))] * 4
    in_specs += [pl.BlockSpec((None, 1, RET_DV), lambda b, h, c: (h, 0, 0))] * 2
    args += [dfb, qdf, qdb, kdf, kdb, cdf, cdb]
    if latent:
        s0f, s0b = states
        st_spec = pl.BlockSpec((None, None, None, RET_DK, RET_DV), lambda b, h, c: (b, j, h, 0, 0))
        in_specs += [st_spec, st_spec]
        args += [s0f, s0b]

    o_shape = jax.ShapeDtypeStruct((nseq * seq_len, h_ * RET_DV), F32)
    o_spec = pl.BlockSpec((seq_len, RET_DV), lambda b, h, c: (b, h))
    if latent:
        out_shape, out_specs = o_shape, o_spec
    else:
        s_shape = jax.ShapeDtypeStruct((nseq, h_, RET_DK, RET_DV), F32)
        s_spec = pl.BlockSpec((None, None, RET_DK, RET_DV), lambda b, h, c: (b, h, 0, 0))
        out_shape, out_specs = (o_shape, s_shape, s_shape), (o_spec, s_spec, s_spec)

    return pl.pallas_call(
        functools.partial(_ret_kernel, latent=latent, nc=nc),
        grid=(nseq, h_, nc),
        in_specs=in_specs,
        out_specs=out_specs,
        out_shape=out_shape,
        scratch_shapes=[pltpu.VMEM((RET_DK, RET_DV), F32), pltpu.VMEM((RET_DK, RET_DV), F32)],
        compiler_params=_params(("parallel", "parallel", "arbitrary")),
        name="retention_latent" if latent else "retention_context",
    )(*args)


def _ret_out_kernel(o_ref, g_ref, x_ref, m_ref, w_ref, lg_ref, lb_ref, out_ref):
    y = None
    for h in range(RET_HEADS):
        cols = slice(h * RET_DV, (h + 1) * RET_DV)
        oh = o_ref[:, cols]
        mu = jnp.mean(oh, -1, keepdims=True)
        d = oh - mu
        var = jnp.mean(d * d, -1, keepdims=True)
        on = d * lax.rsqrt(var + GN_EPS)
        gh = g_ref[:, cols]
        act = (gh * jax.nn.sigmoid(gh) * on).astype(BF16)
        part = _dot(act, w_ref[cols, :])
        y = part if y is None else y + part
    gate = m_ref[2:3, :]
    r = DEEPNORM_ALPHA * x_ref[...] + gate * y
    out_ref[...] = _layer_norm(r, lg_ref[...], lb_ref[...])


def _ret_out(o, z, x, mod, w_bf, ln_g, ln_b, layer, grp):
    nt, d = x.shape
    tm = OUT_TM
    hv = RET_HEADS * RET_DV
    gcol = (2 * RET_HEADS * RET_DK + hv) // hv
    return pl.pallas_call(
        _ret_out_kernel,
        grid=(nt // tm,),
        in_specs=[
            pl.BlockSpec((tm, hv), lambda b: (b, 0)),
            pl.BlockSpec((tm, hv), lambda b: (b, gcol)),
            pl.BlockSpec((tm, d), lambda b: (b, 0)),
            pl.BlockSpec((None, None, 6, d), lambda b: (layer, grp(b, tm), 0, 0)),
            pl.BlockSpec((hv, d), lambda b: (0, 0)),
            pl.BlockSpec((None, 1, d), lambda b: (2 * layer, 0, 0)),
            pl.BlockSpec((None, 1, d), lambda b: (2 * layer, 0, 0)),
        ],
        out_specs=pl.BlockSpec((tm, d), lambda b: (b, 0)),
        out_shape=jax.ShapeDtypeStruct((nt, d), F32),
        compiler_params=_params(("parallel",)),
        name="retention_out",
    )(o, z, x, mod, w_bf, ln_g, ln_b)


def _softmax_pv(parts, sink):
    m = sink
    for s, _ in parts:
        m = jnp.maximum(m, jnp.max(s, -1, keepdims=True))
    den = jnp.exp(sink - m)
    o = None
    for s, v in parts:
        p = jnp.exp(s - m)
        den = den + jnp.sum(p, -1, keepdims=True)
        pv = _dot(p.astype(BF16), v)
        o = pv if o is None else o + pv
    return o / den


def _attn_ctx_kernel(sink_ref, q_ref, k_ref, v_ref, o_ref):
    scale = ATT_DH ** -0.5
    for h in range(ATT_HKV):
        kh = k_ref[:, h * ATT_DH:(h + 1) * ATT_DH].astype(BF16)
        vh = v_ref[:, h * ATT_DH:(h + 1) * ATT_DH].astype(BF16)
        for g in range(ATT_G):
            hq = h * ATT_G + g
            cols = slice(hq * ATT_DH, (hq + 1) * ATT_DH)
            qh = q_ref[:, cols].astype(BF16)
            s = _dot_nt(qh, kh) * scale
            o_ref[:, cols] = _softmax_pv([(s, vh)], sink_ref[hq])


def _attn_context(z, sink, nseq, seq_len):
    hq_w = ATT_HQ * ATT_DH
    kv_w = ATT_HKV * ATT_DH
    return pl.pallas_call(
        _attn_ctx_kernel,
        grid=(nseq,),
        in_specs=[
            pl.BlockSpec(memory_space=pltpu.SMEM),
            pl.BlockSpec((seq_len, hq_w), lambda b: (b, 0)),
            pl.BlockSpec((seq_len, kv_w), lambda b: (b, hq_w // kv_w)),
            pl.BlockSpec((seq_len, kv_w), lambda b: (b, hq_w // kv_w + 1)),
        ],
        out_specs=pl.BlockSpec((seq_len, hq_w), lambda b: (b, 0)),
        out_shape=jax.ShapeDtypeStruct((nseq * seq_len, hq_w), F32),
        compiler_params=_params(("parallel",)),
        name="attention_context",
    )(sink, z, z, z)


def _rope_axial(x, cos, sin):
    w = x.shape[1]
    lane = lax.broadcasted_iota(jnp.int32, (x.shape[0], LANES), 1)
    first = (lane % 32) < 16
    outs = []
    for cg in range(w // LANES):
        xg = x[:, cg * LANES:(cg + 1) * LANES]
        up = pltpu.roll(xg, LANES - 16, 1)
        dn = pltpu.roll(xg, 16, 1)
        outs.append(xg * cos + jnp.where(first, up, dn) * sin)
    return jnp.concatenate(outs, axis=1) if len(outs) > 1 else outs[0]


def _attn_lat_kernel(sink_ref, q_ref, kp_ref, kc_ref, kn_ref, vp_ref, vc_ref, vn_ref, ck_ref, cv_ref,
                     cq_ref, sq_ref, cp_ref, sp_ref, cn_ref, sn_ref, o_ref, *, nb):
    scale = ATT_DH ** -0.5
    qb = pl.program_id(1)
    q = _rope_axial(q_ref[...], cq_ref[...], sq_ref[...])
    kw = jnp.concatenate([
        _rope_axial(kp_ref[...], cp_ref[...], sp_ref[...]),
        _rope_axial(kc_ref[...], cq_ref[...], sq_ref[...]),
        _rope_axial(kn_ref[...], cn_ref[...], sn_ref[...]),
    ], axis=0)
    vw = jnp.concatenate([vp_ref[...], vc_ref[...], vn_ref[...]], axis=0)
    ck = ck_ref[...]
    cv = cv_ref[...]

    i = lax.broadcasted_iota(jnp.int32, (ATT_BLK, 3 * ATT_BLK), 0)
    jj = lax.broadcasted_iota(jnp.int32, (ATT_BLK, 3 * ATT_BLK), 1)
    lo = jnp.where(qb > 0, 0, ATT_BLK)
    hi = jnp.where(qb < nb - 1, 3 * ATT_BLK, 2 * ATT_BLK)
    valid = (jj >= jnp.maximum(i, lo)) & (jj < jnp.minimum(i + 2 * ATT_BLK + 1, hi))

    for h in range(ATT_HKV):
        hs = slice(h * ATT_DH, (h + 1) * ATT_DH)
        kh = kw[:, hs].astype(BF16)
        vh = vw[:, hs].astype(BF16)
        ckh = ck[:, hs].astype(BF16)
        cvh = cv[:, hs].astype(BF16)
        for g in range(ATT_G):
            hq = h * ATT_G + g
            cols = slice(hq * ATT_DH, (hq + 1) * ATT_DH)
            qh = q[:, cols].astype(BF16)
            s_loc = jnp.where(valid, _dot_nt(qh, kh) * scale, NEG_INF)
            s_ctx = _dot_nt(qh, ckh) * scale
            o_ref[:, cols] = _softmax_pv([(s_loc, vh), (s_ctx, cvh)], sink_ref[hq])


def _attn_latent(z, sink, cache_k, cache_v, rope, row0, nseq, seq_len, j):
    nb = seq_len // ATT_BLK
    base = row0 // ATT_BLK
    hq_w = ATT_HQ * ATT_DH
    kv_w = ATT_HKV * ATT_DH
    kcol = hq_w // kv_w
    past = cache_k.shape[2]
    cos, sin = rope

    def prv(q):
        return jnp.maximum(q - 1, 0)

    def nxt(q):
        return jnp.minimum(q + 1, nb - 1)

    def kv_spec(sel, col):
        return pl.BlockSpec((ATT_BLK, kv_w), lambda b, q: (base + b * nb + sel(q), col))

    def rope_spec(sel):
        return pl.BlockSpec((ATT_BLK, LANES), lambda b, q: (sel(q), 0))

    same = lambda q: q
    cache_spec = pl.BlockSpec((None, None, past, kv_w), lambda b, q: (b, j, 0, 0))
    return pl.pallas_call(
        functools.partial(_attn_lat_kernel, nb=nb),
        grid=(nseq, nb),
        in_specs=[
            pl.BlockSpec(memory_space=pltpu.SMEM),
            pl.BlockSpec((ATT_BLK, hq_w), lambda b, q: (base + b * nb + q, 0)),
            kv_spec(prv, kcol), kv_spec(same, kcol), kv_spec(nxt, kcol),
            kv_spec(prv, kcol + 1), kv_spec(same, kcol + 1), kv_spec(nxt, kcol + 1),
            cache_spec, cache_spec,
            rope_spec(same), rope_spec(same), rope_spec(prv), rope_spec(prv), rope_spec(nxt), rope_spec(nxt),
        ],
        out_specs=pl.BlockSpec((ATT_BLK, hq_w), lambda b, q: (b * nb + q, 0)),
        out_shape=jax.ShapeDtypeStruct((nseq * seq_len, hq_w), F32),
        compiler_params=_params(("parallel", "parallel")),
        name="attention_latent",
    )(sink, z, z, z, z, z, z, z, cache_k, cache_v, cos, sin, cos, sin, cos, sin)


def _attn_out_kernel(o_ref, x_ref, m_ref, w_ref, lg_ref, lb_ref, out_ref):
    y = _dot(o_ref[...].astype(BF16), w_ref[...])
    gate = m_ref[2:3, :]
    r = DEEPNORM_ALPHA * x_ref[...] + gate * y
    out_ref[...] = _layer_norm(r, lg_ref[...], lb_ref[...])


def _attn_out(o, x, mod, w_bf, ln_g, ln_b, layer, grp):
    nt, d = x.shape
    tm = OUT_TM
    k = o.shape[1]
    return pl.pallas_call(
        _attn_out_kernel,
        grid=(nt // tm,),
        in_specs=[
            pl.BlockSpec((tm, k), lambda b: (b, 0)),
            pl.BlockSpec((tm, d), lambda b: (b, 0)),
            pl.BlockSpec((None, None, 6, d), lambda b: (layer, grp(b, tm), 0, 0)),
            pl.BlockSpec((k, d), lambda b: (0, 0)),
            pl.BlockSpec((None, 1, d), lambda b: (2 * layer, 0, 0)),
            pl.BlockSpec((None, 1, d), lambda b: (2 * layer, 0, 0)),
        ],
        out_specs=pl.BlockSpec((tm, d), lambda b: (b, 0)),
        out_shape=jax.ShapeDtypeStruct((nt, d), F32),
        compiler_params=_params(("parallel",)),
        name="attention_out",
    )(o, x, mod, w_bf, ln_g, ln_b)


def _sort_network(n):
    pairs = []

    def merge(lo, hi, r):
        step = r * 2
        if step < hi - lo:
            merge(lo, hi, step)
            merge(lo + r, hi, step)
            for i in range(lo + r, hi - r, step):
                pairs.append((i, i + r))
        else:
            pairs.append((lo, lo + r))

    def sort(lo, hi):
        if hi - lo >= 1:
            mid = lo + (hi - lo) // 2
            sort(lo, mid)
            sort(mid + 1, hi)
            merge(lo, hi, 1)

    sort(0, n - 1)
    return pairs


def _pop_top(lists, k):
    vals = []
    for r in range(k):
        head = lists[0]
        m = jnp.max(head, axis=0, keepdims=True)
        vals.append(m)
        left = k - 1 - r
        if left > 0:
            took = head >= m
            nxt = [lists[i + 1] if i + 1 < len(lists) else NEG_INF for i in range(left)]
            lists = [jnp.where(took, nxt[i], lists[i]) for i in range(left)]
    return vals


def _top_values(s, k):
    tiles = [s[8 * j:8 * (j + 1), :] for j in range(s.shape[0] // 8)]
    for i, j in _sort_network(len(tiles)):
        tiles[i], tiles[j] = jnp.maximum(tiles[i], tiles[j]), jnp.minimum(tiles[i], tiles[j])
    return _pop_top(tiles, k)


def _peer_kernel(x_ref, m_ref, wq_ref, keys_ref, u0_ref, u_ref, vt_ref, lg_ref, lb_ref, out_ref,
                 h_scr, q_scr, d1_scr, e1_scr, s2_scr, e2_scr, at0_scr, at1_scr, g_scr, acc_scr,
                 *, n_eblk):
    t = x_ref.shape[0]
    nlt = t // LANES
    e = pl.program_id(1)
    k = PEER_TOPK
    nk = PEER_NKEYS
    hk = nk // PEER_NSUB

    @pl.when(e == 0)
    def _scores():
        sh = m_ref[3:4, :]
        sc = m_ref[4:5, :]
        h_scr[...] = (x_ref[...] * (1.0 + sc) + sh).astype(BF16)
        q = _dot(h_scr[...], wq_ref[...])
        for ph in range(2 * PEER_HEADS):
            q_scr[ph] = q[:, ph * PEER_DHALF:(ph + 1) * PEER_DHALF].astype(BF16)
        acc_scr[...] = jnp.zeros(acc_scr.shape, F32)
        at0_scr[...] = _dot_nt(u0_ref[...], h_scr[...])

        def unit(uidx, carry):
            lt = uidx // PEER_HEADS
            p = uidx % PEER_HEADS
            t0 = pl.multiple_of(lt * LANES, LANES)
            s, vals = [], []
            for hh in range(2):
                sk = _dot_nt(keys_ref[p, hh], q_scr[2 * p + hh, pl.ds(t0, LANES), :])
                s.append(sk)
                vals.append(_top_values(sk, k))
            v1m = [v - vals[0][0] for v in vals[0]]
            v2m = [v - vals[1][0] for v in vals[1]]
            row = lax.broadcasted_iota(jnp.int32, (8, LANES), 0)
            base = jnp.zeros((8, LANES), F32)
            for r in range(4):
                base = jnp.where(row == r, v1m[r], base)
                base = jnp.where(row == 4 + r, v2m[r], base)
            cands = [base + jnp.where(row < 4, v2m[i], v1m[i + 4] if i + 4 < k else NEG_INF) for i in range(k)]
            tops = _pop_top(cands, k + 1)
            thr = 0.5 * (tops[k - 1] + tops[k])
            zsum = jnp.exp(tops[0])
            for r in range(1, k):
                zsum = zsum + jnp.exp(tops[r])
            s1m = jnp.where(s[0] >= vals[0][k - 1], s[0] - vals[0][0], NEG_INF)
            s2m = jnp.where(s[1] >= vals[1][k - 1], s[1] - vals[1][0], NEG_INF)
            d1_scr[p, lt] = thr - s1m
            e1_scr[p, lt] = jnp.exp(s1m) * (0.5 / zsum)
            s2_scr[p, lt] = s2m
            e2_scr[p, lt] = jnp.exp(s2m)
            return carry

        lax.fori_loop(0, nlt * PEER_HEADS, unit, 0, unroll=2)

    def step(at_r, at_w):
        i1_0 = e * PEER_NI1
        mxu_w = 2 * LANES
        il_grp = PEER_ILG
        for ts_i in range(t // mxu_w):
            ts = slice(ts_i * mxu_w, (ts_i + 1) * mxu_w)
            at_w[:, ts] = _dot_nt(u_ref[...], h_scr[ts, :])
            for ig in range(PEER_NI1 // il_grp):
                for lt in range(ts_i * mxu_w // LANES, (ts_i + 1) * mxu_w // LANES):
                    tl = slice(lt * LANES, (lt + 1) * LANES)
                    for hf in range(PEER_NSUB):
                        rows2 = slice(hf * hk, (hf + 1) * hk)
                        ils = range(ig * il_grp, (ig + 1) * il_grp)
                        w = {il: jnp.zeros((hk, LANES), F32) for il in ils}
                        for p in range(PEER_HEADS):
                            s2h = s2_scr[p, lt, rows2, :]
                            e2h = e2_scr[p, lt, rows2, :]
                            for il in ils:
                                d1 = d1_scr[p, lt, pl.ds(i1_0 + il, 1), :]
                                e1 = e1_scr[p, lt, pl.ds(i1_0 + il, 1), :]
                                w[il] = w[il] + jnp.where(s2h >= d1, e2h, 0.0) * e1
                        for il in ils:
                            rows = slice(il * nk + hf * hk, il * nk + (hf + 1) * hk)
                            a = at_r[rows, tl]
                            u = a * (a * a * (GELU_C * 0.044715) + GELU_C)
                            g_scr[rows, tl] = (w[il] * (a + a * jnp.tanh(u))).astype(BF16)
                ks = slice(ig * il_grp * nk, (ig + 1) * il_grp * nk)
                acc_scr[:, ts] += _dot(vt_ref[:, ks], g_scr[ks, ts])

    @pl.when(e % 2 == 0)
    def _even():
        step(at0_scr, at1_scr)

    @pl.when(e % 2 == 1)
    def _odd():
        step(at1_scr, at0_scr)

    @pl.when(e == n_eblk - 1)
    def _finish():
        y = acc_scr[...].T
        gate = m_ref[5:6, :]
        r = DEEPNORM_ALPHA * x_ref[...] + gate * y
        out_ref[...] = _layer_norm(r, lg_ref[...], lb_ref[...])


def _peer(x, mod, wq_bf, keys_bf, u_bf, vt_bf, ln_g, ln_b, layer, grp):
    nt, d = x.shape
    t = PEER_T
    nlt = t // LANES
    eblk = PEER_NI1 * PEER_NKEYS
    n_eblk = u_bf.shape[0] // eblk
    nq = wq_bf.shape[1]
    sel_shape = (PEER_HEADS, nlt, PEER_NKEYS, LANES)
    return pl.pallas_call(
        functools.partial(_peer_kernel, n_eblk=n_eblk),
        grid=(nt // t, n_eblk),
        in_specs=[
            pl.BlockSpec((t, d), lambda b, e: (b, 0)),
            pl.BlockSpec((None, None, 6, d), lambda b, e: (layer, grp(b, t), 0, 0)),
            pl.BlockSpec((d, nq), lambda b, e: (0, 0)),
            pl.BlockSpec((PEER_HEADS, 2, PEER_NKEYS, PEER_DHALF), lambda b, e: (0, 0, 0, 0)),
            pl.BlockSpec((eblk, d), lambda b, e: (0, 0)),
            pl.BlockSpec((eblk, d), lambda b, e: (jnp.minimum(e + 1, n_eblk - 1), 0)),
            pl.BlockSpec((d, eblk), lambda b, e: (0, e)),
            pl.BlockSpec((None, 1, d), lambda b, e: (2 * layer + 1, 0, 0)),
            pl.BlockSpec((None, 1, d), lambda b, e: (2 * layer + 1, 0, 0)),
        ],
        out_specs=pl.BlockSpec((t, d), lambda b, e: (b, 0)),
        out_shape=jax.ShapeDtypeStruct((nt, d), F32),
        scratch_shapes=[
            pltpu.VMEM((t, d), BF16),
            pltpu.VMEM((2 * PEER_HEADS, t, PEER_DHALF), BF16),
            pltpu.VMEM(sel_shape, F32), pltpu.VMEM(sel_shape, F32),
            pltpu.VMEM(sel_shape, F32), pltpu.VMEM(sel_shape, F32),
            pltpu.VMEM((eblk, t), F32), pltpu.VMEM((eblk, t), F32),
            pltpu.VMEM((eblk, t), BF16),
            pltpu.VMEM((d, t), F32),
        ],
        compiler_params=_params(("parallel", "arbitrary")),
        name="peer",
    )(x, mod, wq_bf, keys_bf, u_bf, u_bf, vt_bf, ln_g, ln_b)


def _rope_angles(pos, dim):
    inv = ROPE_BASE ** (-jnp.arange(0, dim, 2, dtype=F32) / dim)
    ang = pos.astype(F32)[:, None] * inv[None, :]
    return jnp.cos(ang), jnp.sin(ang)


def _axial_tables(seq_len):
    t = jnp.arange(seq_len)
    half = ATT_DH // 2
    cr, sr = _rope_angles(t // GRID_W, half)
    cc, sc = _rope_angles(t % GRID_W, half)
    cos = jnp.concatenate([cr, cr, cc, cc], -1)
    sin = jnp.concatenate([-sr, sr, -sc, sc], -1)
    reps = LANES // ATT_DH
    return jnp.tile(cos, (1, reps)), jnp.tile(sin, (1, reps))


def kernel(x_prompt, x_sample, state_ret_fwd, state_ret_bwd, cache_k, cache_v, c, c_ctx, mod_w, mod_b, ln_g, ln_b,
           ret_w_in, ret_w_out, ret_decay, attn_w_in, attn_w_out, attn_sink, peer_wq, peer_keys, peer_u, peer_v):
    nb_p, len_p, d = x_prompt.shape
    nb_s, len_s, _ = x_sample.shape
    n_p = nb_p * len_p
    n_s = nb_s * len_s
    for tile in (PROJ_TM, OUT_TM, PEER_T):
        assert n_p % tile == 0 and len_s % tile == 0

    def grp(blk, tile):
        return _group_index(blk, n_p // tile, len_s // tile)

    x = jnp.concatenate([x_prompt.reshape(n_p, d), x_sample.reshape(n_s, d)], axis=0)
    n_grp = 1 + nb_s
    n_grp_pad = -(-n_grp // 8) * 8
    cond = jnp.concatenate([c_ctx[None, :], c, jnp.zeros((n_grp_pad - n_grp, d), F32)], axis=0)
    mod = _modulation(cond, mod_w, mod_b).reshape(DEPTH, n_grp_pad, 6, d)

    ln_g3 = ln_g.reshape(DEPTH * 2, 1, d)
    ln_b3 = ln_b.reshape(DEPTH * 2, 1, d)
    ret_rope = _rope_angles(jnp.arange(len_s), RET_DK)
    att_rope = _axial_tables(len_s)
    kv_w = ATT_HKV * ATT_DH
    cache_k4 = cache_k.reshape(cache_k.shape[0], cache_k.shape[1], cache_k.shape[2], kv_w)
    cache_v4 = cache_v.reshape(cache_v.shape[0], cache_v.shape[1], cache_v.shape[2], kv_w)

    new_sf, new_sb, new_k, new_v = [], [], [], []
    for i in range(DEPTH):
        j = i // N_MIXERS
        if i % N_MIXERS == 0:
            z = _project(x, mod, ret_w_in[j].astype(BF16), i, 0, grp)
            tables = _ret_tables(ret_decay[j])
            o_p, sf, sb = _retention(z, tables, 0, nb_p, len_p, False)
            o_s = _retention(z, tables, n_p, nb_s, len_s, True, rope=ret_rope,
                             states=(state_ret_fwd, state_ret_bwd), j=j)
            new_sf.append(sf)
            new_sb.append(sb)
            o = jnp.concatenate([o_p, o_s], axis=0)
            x = _ret_out(o, z, x, mod, ret_w_out[j].astype(BF16), ln_g3, ln_b3, i, grp)
        else:
            z = _project(x, mod, attn_w_in[j].astype(BF16), i, 0, grp)
            sink = attn_sink[j].astype(F32)
            o_p = _attn_context(z, sink, nb_p, len_p)
            o_s = _attn_latent(z, sink, cache_k4, cache_v4, att_rope, n_p, nb_s, len_s, j)
            hq_w = ATT_HQ * ATT_DH
            new_k.append(z[:n_p, hq_w:hq_w + kv_w].reshape(nb_p, len_p, ATT_HKV, ATT_DH))
            new_v.append(z[:n_p, hq_w + kv_w:hq_w + 2 * kv_w].reshape(nb_p, len_p, ATT_HKV, ATT_DH))
            o = jnp.concatenate([o_p, o_s], axis=0)
            x = _attn_out(o, x, mod, attn_w_out[j].astype(BF16), ln_g3, ln_b3, i, grp)
        x = _peer(x, mod, peer_wq[i].astype(BF16), peer_keys[i].astype(BF16), peer_u[i].astype(BF16),
                  peer_v[i].T.astype(BF16), ln_g3, ln_b3, i, grp)

    y_p = x[:n_p].reshape(nb_p, len_p, d)
    y_s = x[n_p:].reshape(nb_s, len_s, d)
    return (y_p, y_s, jnp.stack(new_sf, 1), jnp.stack(new_sb, 1), jnp.stack(new_k, 1), jnp.stack(new_v, 1))
```

```python
import functools
import math

import jax
import jax.numpy as jnp
from jax import lax
from jax.experimental import pallas as pl
from jax.experimental.pallas import tpu as pltpu

F32 = jnp.float32
BF16 = jnp.bfloat16

DEPTH = 4
N_MIXERS = 2
GRID_W = 64
RET_HEADS = 4
RET_DK = 256
RET_DV = 512
RET_CHUNK = 128
ATT_HQ = 16
ATT_HKV = 4
ATT_G = ATT_HQ // ATT_HKV
ATT_DH = 64
ATT_BLK = 128
ROPE_BASE = 10000.0
PEER_HEADS = 8
PEER_NKEYS = 128
PEER_DHALF = 128
PEER_TOPK = 16
DEEPNORM_ALPHA = (2.0 * DEPTH) ** 0.25
LN_EPS = 1e-5
GN_EPS = 1e-5
NEG_INF = -1e30
GELU_C = math.sqrt(2.0 / math.pi)

LANES = 128
VMEM_LIMIT = 56 * 1024 * 1024
PROJ_TM = 512
OUT_TM = 256
PEER_T = 512
PEER_NI1 = 8
PEER_ILG = 4
PEER_NSUB = 2


def _dot(a, b):
    return jnp.dot(a, b, preferred_element_type=F32)


def _dot_nt(a, b):
    return lax.dot_general(a, b, (((1,), (1,)), ((), ())), preferred_element_type=F32)


def _params(sem, flags=None):
    return pltpu.CompilerParams(dimension_semantics=sem, vmem_limit_bytes=VMEM_LIMIT, flags=flags)


def _group_index(blk, n_prompt_blocks, blocks_per_seq):
    return jnp.where(blk < n_prompt_blocks, 0, 1 + (blk - n_prompt_blocks) // blocks_per_seq)


def _layer_norm(r, g, b):
    mu = jnp.mean(r, -1, keepdims=True)
    d = r - mu
    var = jnp.mean(d * d, -1, keepdims=True)
    return d * lax.rsqrt(var + LN_EPS) * g + b


def _mod_kernel(c_ref, w_ref, b_ref, o_ref):
    c = c_ref[...]
    a = (c * jax.nn.sigmoid(c)).astype(BF16)
    o_ref[...] = _dot(a, w_ref[...].astype(BF16)) + b_ref[...]


def _modulation(cond, mod_w, mod_b):
    ngp, d = cond.shape
    n_out = mod_w.shape[-1]
    tn = 1536
    return pl.pallas_call(
        _mod_kernel,
        grid=(DEPTH, n_out // tn),
        in_specs=[
            pl.BlockSpec((ngp, d), lambda i, n: (0, 0)),
            pl.BlockSpec((None, d, tn), lambda i, n: (i, 0, n)),
            pl.BlockSpec((None, 1, tn), lambda i, n: (i, 0, n)),
        ],
        out_specs=pl.BlockSpec((None, ngp, tn), lambda i, n: (i, 0, n)),
        out_shape=jax.ShapeDtypeStruct((DEPTH, ngp, n_out), F32),
        compiler_params=_params(("parallel", "parallel")),
        name="modulation",
    )(cond, mod_w, mod_b.reshape(DEPTH, 1, n_out))


def _proj_kernel(x_ref, m_ref, w_ref, o_ref, *, off):
    sh = m_ref[off:off + 1, :]
    sc = m_ref[off + 1:off + 2, :]
    h = (x_ref[...] * (1.0 + sc) + sh).astype(BF16)
    o_ref[...] = _dot(h, w_ref[...])


def _project(x, mod, w_bf, layer, off, grp):
    nt, d = x.shape
    n = w_bf.shape[1]
    tn = 1536
    tm = PROJ_TM
    return pl.pallas_call(
        functools.partial(_proj_kernel, off=off),
        grid=(n // tn, nt // tm),
        in_specs=[
            pl.BlockSpec((tm, d), lambda j, b: (b, 0)),
            pl.BlockSpec((None, None, 6, d), lambda j, b: (layer, grp(b, tm), 0, 0)),
            pl.BlockSpec((d, tn), lambda j, b: (0, j)),
        ],
        out_specs=pl.BlockSpec((tm, tn), lambda j, b: (b, j)),
        out_shape=jax.ShapeDtypeStruct((nt, n), F32),
        compiler_params=_params(("parallel", "parallel")),
        name="mod_proj",
    )(x, mod, w_bf)


def _rope_half(x, cos, sin):
    x1 = x[:, :LANES]
    x2 = x[:, LANES:]
    return jnp.concatenate([x1 * cos - x2 * sin, x1 * sin + x2 * cos], axis=1)


def _ret_kernel(*refs, latent, nc):
    it = iter(refs)
    qa, ka, va, qb, kb, vb = (next(it) for _ in range(6))
    if latent:
        cosa, sina, cosb, sinb = (next(it) for _ in range(4))
    dfb, qdf, qdb, kdf, kdb, cdf, cdb = (next(it) for _ in range(7))
    if latent:
        s0f, s0b = next(it), next(it)
    o_ref = next(it)
    if not latent:
        sf_out, sb_out = next(it), next(it)
    sf, sb = next(it), next(it)

    c = pl.program_id(2)

    @pl.when(c == 0)
    def _init():
        o_ref[...] = jnp.zeros(o_ref.shape, F32)
        if latent:
            sf[...] = s0f[...]
            sb[...] = s0b[...]
        else:
            sf[...] = jnp.zeros(sf.shape, F32)
            sb[...] = jnp.zeros(sb.shape, F32)

    def prep(q_ref, k_ref, cos_ref, sin_ref):
        q = q_ref[...]
        k = k_ref[...] * (RET_DK ** -0.5)
        if latent:
            cos = cos_ref[...]
            sin = sin_ref[...]
            q = _rope_half(q, cos, sin)
            k = _rope_half(k, cos, sin)
        return q, k

    q, k = prep(qa, ka, cosa if latent else None, sina if latent else None)
    v = va[...].astype(BF16)
    att = (_dot_nt(q.astype(BF16), k.astype(BF16)) * dfb[...]).astype(BF16)
    o_c = _dot(att, v) + _dot((q * qdf[...]).astype(BF16), sf[...].astype(BF16))
    rows_c = pl.ds(pl.multiple_of(c * RET_CHUNK, RET_CHUNK), RET_CHUNK)
    o_ref[rows_c, :] += o_c
    kd_t = (k * kdf[...]).T.astype(BF16)
    sf[...] = sf[...] * cdf[...] + _dot(kd_t, v)

    q2, k2 = prep(qb, kb, cosb if latent else None, sinb if latent else None)
    v2 = vb[...].astype(BF16)
    o_b = _dot((q2 * qdb[...]).astype(BF16), sb[...].astype(BF16))
    rows_b = pl.ds(pl.multiple_of((nc - 1 - c) * RET_CHUNK, RET_CHUNK), RET_CHUNK)
    o_ref[rows_b, :] += o_b
    kd2_t = (k2 * kdb[...]).T.astype(BF16)
    sb[...] = sb[...] * cdb[...] + _dot(kd2_t, v2)

    if not latent:
        @pl.when(c == nc - 1)
        def _fin():
            sf_out[...] = sf[...]
            sb_out[...] = sb[...]


def _ret_tables(decay):
    log_g = jax.nn.log_sigmoid(decay.astype(F32))
    lf, lb = log_g[0], log_g[1]
    idx = jnp.arange(RET_CHUNK, dtype=F32)
    diff = idx[:, None] - idx[None, :]
    fmask = diff >= 0
    bmask = diff < 0
    df = jnp.where(fmask[None], jnp.exp(jnp.where(fmask, diff, 0.0)[None] * lf[:, None, None]), 0.0)
    db = jnp.where(bmask[None], jnp.exp(jnp.where(bmask, -diff, 0.0)[None] * lb[:, None, None]), 0.0)
    dfb = df + db
    def rows(e):
        return jnp.broadcast_to(e[:, :, None], (RET_HEADS, RET_CHUNK, RET_DK))
    qdf = rows(jnp.exp((idx + 1.0)[None, :] * lf[:, None]))
    kdf = rows(jnp.exp((RET_CHUNK - 1.0 - idx)[None, :] * lf[:, None]))
    qdb = rows(jnp.exp((RET_CHUNK - idx)[None, :] * lb[:, None]))
    kdb = rows(jnp.exp(idx[None, :] * lb[:, None]))
    cdf = jnp.broadcast_to(jnp.exp(RET_CHUNK * lf)[:, None, None], (RET_HEADS, 1, RET_DV))
    cdb = jnp.broadcast_to(jnp.exp(RET_CHUNK * lb)[:, None, None], (RET_HEADS, 1, RET_DV))
    return dfb, qdf, qdb, kdf, kdb, cdf, cdb


def _retention(z, tables, row0, nseq, seq_len, latent, rope=None, states=None, j=0):
    nc = seq_len // RET_CHUNK
    base = row0 // RET_CHUNK
    h_ = RET_HEADS
    kcol = (h_ * RET_DK) // RET_DK
    vcol = (2 * h_ * RET_DK) // RET_DV

    def fw(b, h, c):
        return base + b * nc + c

    def bw(b, h, c):
        return base + b * nc + (nc - 1 - c)

    in_specs = [
        pl.BlockSpec((RET_CHUNK, RET_DK), lambda b, h, c: (fw(b, h, c), h)),
        pl.BlockSpec((RET_CHUNK, RET_DK), lambda b, h, c: (fw(b, h, c), kcol + h)),
        pl.BlockSpec((RET_CHUNK, RET_DV), lambda b, h, c: (fw(b, h, c), vcol + h)),
        pl.BlockSpec((RET_CHUNK, RET_DK), lambda b, h, c: (bw(b, h, c), h)),
        pl.BlockSpec((RET_CHUNK, RET_DK), lambda b, h, c: (bw(b, h, c), kcol + h)),
        pl.BlockSpec((RET_CHUNK, RET_DV), lambda b, h, c: (bw(b, h, c), vcol + h)),
    ]
    args = [z, z, z, z, z, z]
    if latent:
        cos, sin = rope
        in_specs += [
            pl.BlockSpec((RET_CHUNK, LANES), lambda b, h, c: (c, 0)),
            pl.BlockSpec((RET_CHUNK, LANES), lambda b, h, c: (c, 0)),
            pl.BlockSpec((RET_CHUNK, LANES), lambda b, h, c: (nc - 1 - c, 0)),
            pl.BlockSpec((RET_CHUNK, LANES), lambda b, h, c: (nc - 1 - c, 0)),
        ]
        args += [cos, sin, cos, sin]
    dfb, qdf, qdb, kdf, kdb, cdf, cdb = tables
    in_specs += [pl.BlockSpec((None, RET_CHUNK, RET_CHUNK), lambda b, h, c: (h, 0, 0))]
    in_specs += [pl.BlockSpec((None, RET_CHUNK, RET_DK), lambda b, h, c: (h, 0, 0))] * 4
    in_specs += [pl.BlockSpec((None, 1, RET_DV), lambda b, h, c: (h, 0, 0))] * 2
    args += [dfb, qdf, qdb, kdf, kdb, cdf, cdb]
    if latent:
        s0f, s0b = states
        st_spec = pl.BlockSpec((None, None, None, RET_DK, RET_DV), lambda b, h, c: (b, j, h, 0, 0))
        in_specs += [st_spec, st_spec]
        args += [s0f, s0b]

    o_shape = jax.ShapeDtypeStruct((nseq * seq_len, h_ * RET_DV), F32)
    o_spec = pl.BlockSpec((seq_len, RET_DV), lambda b, h, c: (b, h))
    if latent:
        out_shape, out_specs = o_shape, o_spec
    else:
        s_shape = jax.ShapeDtypeStruct((nseq, h_, RET_DK, RET_DV), F32)
        s_spec = pl.BlockSpec((None, None, RET_DK, RET_DV), lambda b, h, c: (b, h, 0, 0))
        out_shape, out_specs = (o_shape, s_shape, s_shape), (o_spec, s_spec, s_spec)

    return pl.pallas_call(
        functools.partial(_ret_kernel, latent=latent, nc=nc),
        grid=(nseq, h_, nc),
        in_specs=in_specs,
        out_specs=out_specs,
        out_shape=out_shape,
        scratch_shapes=[pltpu.VMEM((RET_DK, RET_DV), F32), pltpu.VMEM((RET_DK, RET_DV), F32)],
        compiler_params=_params(("parallel", "parallel", "arbitrary")),
        name="retention_latent" if latent else "retention_context",
    )(*args)


def _ret_out_kernel(op_ref, os_ref, g_ref, x_ref, m_ref, w_ref, lg_ref, lb_ref, out_ref, *, n_ctx_tiles):
    is_ctx = pl.program_id(0) < n_ctx_tiles
    y = None
    for h in range(RET_HEADS):
        cols = slice(h * RET_DV, (h + 1) * RET_DV)
        oh = jnp.where(is_ctx, op_ref[:, cols], os_ref[:, cols])
        mu = jnp.mean(oh, -1, keepdims=True)
        d = oh - mu
        var = jnp.mean(d * d, -1, keepdims=True)
        on = d * lax.rsqrt(var + GN_EPS)
        gh = g_ref[:, cols]
        act = (gh * jax.nn.sigmoid(gh) * on).astype(BF16)
        part = _dot(act, w_ref[cols, :])
        y = part if y is None else y + part
    gate = m_ref[2:3, :]
    r = DEEPNORM_ALPHA * x_ref[...] + gate * y
    out_ref[...] = _layer_norm(r, lg_ref[...], lb_ref[...])


def _ret_out(o_ctx, o_lat, z, x, mod, w_bf, ln_g, ln_b, layer, grp):
    nt, d = x.shape
    tm = OUT_TM
    hv = RET_HEADS * RET_DV
    gcol = (2 * RET_HEADS * RET_DK + hv) // hv
    n_ctx = o_ctx.shape[0] // tm
    return pl.pallas_call(
        functools.partial(_ret_out_kernel, n_ctx_tiles=n_ctx),
        grid=(nt // tm,),
        in_specs=[
            pl.BlockSpec((tm, hv), lambda b: (jnp.minimum(b, n_ctx - 1), 0)),
            pl.BlockSpec((tm, hv), lambda b: (jnp.maximum(b - n_ctx, 0), 0)),
            pl.BlockSpec((tm, hv), lambda b: (b, gcol)),
            pl.BlockSpec((tm, d), lambda b: (b, 0)),
            pl.BlockSpec((None, None, 6, d), lambda b: (layer, grp(b, tm), 0, 0)),
            pl.BlockSpec((hv, d), lambda b: (0, 0)),
            pl.BlockSpec((None, 1, d), lambda b: (2 * layer, 0, 0)),
            pl.BlockSpec((None, 1, d), lambda b: (2 * layer, 0, 0)),
        ],
        out_specs=pl.BlockSpec((tm, d), lambda b: (b, 0)),
        out_shape=jax.ShapeDtypeStruct((nt, d), F32),
        compiler_params=_params(("parallel",)),
        name="retention_out",
    )(o_ctx, o_lat, z, x, mod, w_bf, ln_g, ln_b)


def _softmax_pv(parts, sink):
    m = sink
    for s, _ in parts:
        m = jnp.maximum(m, jnp.max(s, -1, keepdims=True))
    den = jnp.exp(sink - m)
    o = None
    for s, v in parts:
        p = jnp.exp(s - m)
        den = den + jnp.sum(p, -1, keepdims=True)
        pv = _dot(p.astype(BF16), v)
        o = pv if o is None else o + pv
    return o / den


def _attn_ctx_kernel(sink_ref, q_ref, k_ref, v_ref, o_ref):
    scale = ATT_DH ** -0.5
    for h in range(ATT_HKV):
        kh = k_ref[:, h * ATT_DH:(h + 1) * ATT_DH].astype(BF16)
        vh = v_ref[:, h * ATT_DH:(h + 1) * ATT_DH].astype(BF16)
        for g in range(ATT_G):
            hq = h * ATT_G + g
            cols = slice(hq * ATT_DH, (hq + 1) * ATT_DH)
            qh = (q_ref[:, cols] * scale).astype(BF16)
            o_ref[:, cols] = _softmax_pv([(_dot_nt(qh, kh), vh)], sink_ref[hq])


def _attn_context(z, sink, nseq, seq_len):
    hq_w = ATT_HQ * ATT_DH
    kv_w = ATT_HKV * ATT_DH
    return pl.pallas_call(
        _attn_ctx_kernel,
        grid=(nseq,),
        in_specs=[
            pl.BlockSpec(memory_space=pltpu.SMEM),
            pl.BlockSpec((seq_len, hq_w), lambda b: (b, 0)),
            pl.BlockSpec((seq_len, kv_w), lambda b: (b, hq_w // kv_w)),
            pl.BlockSpec((seq_len, kv_w), lambda b: (b, hq_w // kv_w + 1)),
        ],
        out_specs=pl.BlockSpec((seq_len, hq_w), lambda b: (b, 0)),
        out_shape=jax.ShapeDtypeStruct((nseq * seq_len, hq_w), F32),
        compiler_params=_params(("parallel",)),
        name="attention_context",
    )(sink, z, z, z)


def _rope_axial(x, cos, sin):
    w = x.shape[1]
    lane = lax.broadcasted_iota(jnp.int32, (x.shape[0], LANES), 1)
    first = (lane % 32) < 16
    outs = []
    for cg in range(w // LANES):
        xg = x[:, cg * LANES:(cg + 1) * LANES]
        up = pltpu.roll(xg, LANES - 16, 1)
        dn = pltpu.roll(xg, 16, 1)
        outs.append(xg * cos + jnp.where(first, up, dn) * sin)
    return jnp.concatenate(outs, axis=1) if len(outs) > 1 else outs[0]


def _attn_lat_kernel(sink_ref, q_ref, kp_ref, kc_ref, kn_ref, vp_ref, vc_ref, vn_ref, ck_ref, cv_ref,
                     cq_ref, sq_ref, cp_ref, sp_ref, cn_ref, sn_ref, o_ref, *, nb):
    scale = ATT_DH ** -0.5
    qb = pl.program_id(1)
    q = _rope_axial(q_ref[...], cq_ref[...], sq_ref[...]) * scale
    kw = jnp.concatenate([
        _rope_axial(kp_ref[...], cp_ref[...], sp_ref[...]),
        _rope_axial(kc_ref[...], cq_ref[...], sq_ref[...]),
        _rope_axial(kn_ref[...], cn_ref[...], sn_ref[...]),
    ], axis=0)
    vw = jnp.concatenate([vp_ref[...], vc_ref[...], vn_ref[...]], axis=0)
    ck = ck_ref[...]
    cv = cv_ref[...]

    rows = ATT_G * ATT_BLK
    i = lax.broadcasted_iota(jnp.int32, (rows, 3 * ATT_BLK), 0) % ATT_BLK
    jj = lax.broadcasted_iota(jnp.int32, (rows, 3 * ATT_BLK), 1)
    lo = jnp.where(qb > 0, 0, ATT_BLK)
    hi = jnp.where(qb < nb - 1, 3 * ATT_BLK, 2 * ATT_BLK)
    valid = (jj >= jnp.maximum(i, lo)) & (jj < jnp.minimum(i + 2 * ATT_BLK + 1, hi))
    gidx = lax.broadcasted_iota(jnp.int32, (rows, 1), 0) // ATT_BLK

    for h in range(ATT_HKV):
        hs = slice(h * ATT_DH, (h + 1) * ATT_DH)
        kh = kw[:, hs].astype(BF16)
        vh = vw[:, hs].astype(BF16)
        ckh = ck[:, hs].astype(BF16)
        cvh = cv[:, hs].astype(BF16)
        heads = [h * ATT_G + g for g in range(ATT_G)]
        qg = jnp.concatenate([q[:, hq * ATT_DH:(hq + 1) * ATT_DH] for hq in heads], axis=0).astype(BF16)
        sink = jnp.zeros((rows, 1), F32)
        for g, hq in enumerate(heads):
            sink = jnp.where(gidx == g, sink_ref[hq], sink)
        s_loc = jnp.where(valid, _dot_nt(qg, kh), NEG_INF)
        s_ctx = _dot_nt(qg, ckh)
        o = _softmax_pv([(s_loc, vh), (s_ctx, cvh)], sink)
        for g, hq in enumerate(heads):
            o_ref[:, hq * ATT_DH:(hq + 1) * ATT_DH] = o[g * ATT_BLK:(g + 1) * ATT_BLK, :]


def _attn_latent(z, sink, cache_k, cache_v, rope, row0, nseq, seq_len, j):
    nb = seq_len // ATT_BLK
    base = row0 // ATT_BLK
    hq_w = ATT_HQ * ATT_DH
    kv_w = ATT_HKV * ATT_DH
    kcol = hq_w // kv_w
    past = cache_k.shape[2]
    cos, sin = rope

    def prv(q):
        return jnp.maximum(q - 1, 0)

    def nxt(q):
        return jnp.minimum(q + 1, nb - 1)

    def kv_spec(sel, col):
        return pl.BlockSpec((ATT_BLK, kv_w), lambda b, q: (base + b * nb + sel(q), col))

    def rope_spec(sel):
        return pl.BlockSpec((ATT_BLK, LANES), lambda b, q: (sel(q), 0))

    same = lambda q: q
    cache_spec = pl.BlockSpec((None, None, past, kv_w), lambda b, q: (b, j, 0, 0))
    return pl.pallas_call(
        functools.partial(_attn_lat_kernel, nb=nb),
        grid=(nseq, nb),
        in_specs=[
            pl.BlockSpec(memory_space=pltpu.SMEM),
            pl.BlockSpec((ATT_BLK, hq_w), lambda b, q: (base + b * nb + q, 0)),
            kv_spec(prv, kcol), kv_spec(same, kcol), kv_spec(nxt, kcol),
            kv_spec(prv, kcol + 1), kv_spec(same, kcol + 1), kv_spec(nxt, kcol + 1),
            cache_spec, cache_spec,
            rope_spec(same), rope_spec(same), rope_spec(prv), rope_spec(prv), rope_spec(nxt), rope_spec(nxt),
        ],
        out_specs=pl.BlockSpec((ATT_BLK, hq_w), lambda b, q: (b * nb + q, 0)),
        out_shape=jax.ShapeDtypeStruct((nseq * seq_len, hq_w), F32),
        compiler_params=_params(("parallel", "parallel")),
        name="attention_latent",
    )(sink, z, z, z, z, z, z, z, cache_k, cache_v, cos, sin, cos, sin, cos, sin)


def _attn_out_kernel(op_ref, os_ref, x_ref, m_ref, w_ref, lg_ref, lb_ref, out_ref, *, n_ctx_tiles):
    is_ctx = pl.program_id(0) < n_ctx_tiles
    o = jnp.where(is_ctx, op_ref[...], os_ref[...])
    y = _dot(o.astype(BF16), w_ref[...])
    gate = m_ref[2:3, :]
    r = DEEPNORM_ALPHA * x_ref[...] + gate * y
    out_ref[...] = _layer_norm(r, lg_ref[...], lb_ref[...])


def _attn_out(o_ctx, o_lat, x, mod, w_bf, ln_g, ln_b, layer, grp):
    nt, d = x.shape
    tm = OUT_TM
    k = o_ctx.shape[1]
    n_ctx = o_ctx.shape[0] // tm
    return pl.pallas_call(
        functools.partial(_attn_out_kernel, n_ctx_tiles=n_ctx),
        grid=(nt // tm,),
        in_specs=[
            pl.BlockSpec((tm, k), lambda b: (jnp.minimum(b, n_ctx - 1), 0)),
            pl.BlockSpec((tm, k), lambda b: (jnp.maximum(b - n_ctx, 0), 0)),
            pl.BlockSpec((tm, d), lambda b: (b, 0)),
            pl.BlockSpec((None, None, 6, d), lambda b: (layer, grp(b, tm), 0, 0)),
            pl.BlockSpec((k, d), lambda b: (0, 0)),
            pl.BlockSpec((None, 1, d), lambda b: (2 * layer, 0, 0)),
            pl.BlockSpec((None, 1, d), lambda b: (2 * layer, 0, 0)),
        ],
        out_specs=pl.BlockSpec((tm, d), lambda b: (b, 0)),
        out_shape=jax.ShapeDtypeStruct((nt, d), F32),
        compiler_params=_params(("parallel",)),
        name="attention_out",
    )(o_ctx, o_lat, x, mod, w_bf, ln_g, ln_b)


def _sort_network(n):
    pairs = []

    def merge(lo, hi, r):
        step = r * 2
        if step < hi - lo:
            merge(lo, hi, step)
            merge(lo + r, hi, step)
            for i in range(lo + r, hi - r, step):
                pairs.append((i, i + r))
        else:
            pairs.append((lo, lo + r))

    def sort(lo, hi):
        if hi - lo >= 1:
            mid = lo + (hi - lo) // 2
            sort(lo, mid)
            sort(mid + 1, hi)
            merge(lo, hi, 1)

    sort(0, n - 1)
    return pairs


def _pop_top(lists, k):
    vals = []
    for r in range(k):
        head = lists[0]
        m = jnp.max(head, axis=0, keepdims=True)
        vals.append(m)
        left = k - 1 - r
        if left > 0:
            took = head >= m
            nxt = [lists[i + 1] if i + 1 < len(lists) else NEG_INF for i in range(left)]
            lists = [jnp.where(took, nxt[i], lists[i]) for i in range(left)]
    return vals


def _top_values(s, k):
    tiles = [s[8 * j:8 * (j + 1), :] for j in range(s.shape[0] // 8)]
    for i, j in _sort_network(len(tiles)):
        tiles[i], tiles[j] = jnp.maximum(tiles[i], tiles[j]), jnp.minimum(tiles[i], tiles[j])
    return _pop_top(tiles, k)


def _peer_kernel(x_ref, m_ref, wq_ref, keys_ref, u0_ref, u_ref, vt_ref, lg_ref, lb_ref, out_ref,
                 h_scr, q_scr, d1_scr, e1_scr, s2_scr, e2_scr, at0_scr, at1_scr, g_scr, acc_scr,
                 *, n_eblk):
    t = x_ref.shape[0]
    nlt = t // LANES
    e = pl.program_id(1)
    k = PEER_TOPK
    nk = PEER_NKEYS
    hk = nk // PEER_NSUB

    @pl.when(e == 0)
    def _scores():
        sh = m_ref[3:4, :]
        sc = m_ref[4:5, :]
        h_scr[...] = (x_ref[...] * (1.0 + sc) + sh).astype(BF16)
        q = _dot(h_scr[...], wq_ref[...])
        for ph in range(2 * PEER_HEADS):
            q_scr[ph] = q[:, ph * PEER_DHALF:(ph + 1) * PEER_DHALF].astype(BF16)
        acc_scr[...] = jnp.zeros(acc_scr.shape, F32)
        at0_scr[...] = _dot_nt(u0_ref[...], h_scr[...])

        def unit(uidx, carry):
            lt = uidx // PEER_HEADS
            p = uidx % PEER_HEADS
            t0 = pl.multiple_of(lt * LANES, LANES)
            s, vals = [], []
            for hh in range(2):
                sk = _dot_nt(keys_ref[p, hh], q_scr[2 * p + hh, pl.ds(t0, LANES), :])
                s.append(sk)
                vals.append(_top_values(sk, k))
            v1m = [v - vals[0][0] for v in vals[0]]
            v2m = [v - vals[1][0] for v in vals[1]]
            row = lax.broadcasted_iota(jnp.int32, (8, LANES), 0)
            base = jnp.zeros((8, LANES), F32)
            for r in range(4):
                base = jnp.where(row == r, v1m[r], base)
                base = jnp.where(row == 4 + r, v2m[r], base)
            cands = [base + jnp.where(row < 4, v2m[i], v1m[i + 4] if i + 4 < k else NEG_INF) for i in range(k)]
            tops = _pop_top(cands, k + 1)
            thr = 0.5 * (tops[k - 1] + tops[k])
            zsum = jnp.exp(tops[0])
            for r in range(1, k):
                zsum = zsum + jnp.exp(tops[r])
            s1m = jnp.where(s[0] >= vals[0][k - 1], s[0] - vals[0][0], NEG_INF)
            s2m = jnp.where(s[1] >= vals[1][k - 1], s[1] - vals[1][0], NEG_INF)
            d1_scr[p, lt] = thr - s1m
            e1_scr[p, lt] = jnp.exp(s1m) * (0.5 / zsum)
            s2_scr[p, lt] = s2m
            e2_scr[p, lt] = jnp.exp(s2m)
            return carry

        lax.fori_loop(0, nlt * PEER_HEADS, unit, 0, unroll=2)

    def step(at_r, at_w):
        i1_0 = e * PEER_NI1
        mxu_w = 2 * LANES
        il_grp = PEER_ILG
        for ts_i in range(t // mxu_w):
            ts = slice(ts_i * mxu_w, (ts_i + 1) * mxu_w)
            at_w[:, ts] = _dot_nt(u_ref[...], h_scr[ts, :])
            for ig in range(PEER_NI1 // il_grp):
                for lt in range(ts_i * mxu_w // LANES, (ts_i + 1) * mxu_w // LANES):
                    tl = slice(lt * LANES, (lt + 1) * LANES)
                    for hf in range(PEER_NSUB):
                        rows2 = slice(hf * hk, (hf + 1) * hk)
                        ils = range(ig * il_grp, (ig + 1) * il_grp)
                        w = {il: jnp.zeros((hk, LANES), F32) for il in ils}
                        for p in range(PEER_HEADS):
                            s2h = s2_scr[p, lt, rows2, :]
                            e2h = e2_scr[p, lt, rows2, :]
                            for il in ils:
                                d1 = d1_scr[p, lt, pl.ds(i1_0 + il, 1), :]
                                e1 = e1_scr[p, lt, pl.ds(i1_0 + il, 1), :]
                                w[il] = w[il] + jnp.where(s2h >= d1, e2h, 0.0) * e1
                        for il in ils:
                            rows = slice(il * nk + hf * hk, il * nk + (hf + 1) * hk)
                            a = at_r[rows, tl]
                            u = a * (a * a * (GELU_C * 0.044715) + GELU_C)
                            g_scr[rows, tl] = (w[il] * (a + a * jnp.tanh(u))).astype(BF16)
                ks = slice(ig * il_grp * nk, (ig + 1) * il_grp * nk)
                acc_scr[:, ts] += _dot(vt_ref[:, ks], g_scr[ks, ts])

    @pl.when(e % 2 == 0)
    def _even():
        step(at0_scr, at1_scr)

    @pl.when(e % 2 == 1)
    def _odd():
        step(at1_scr, at0_scr)

    @pl.when(e == n_eblk - 1)
    def _finish():
        y = acc_scr[...].T
        gate = m_ref[5:6, :]
        r = DEEPNORM_ALPHA * x_ref[...] + gate * y
        out_ref[...] = _layer_norm(r, lg_ref[...], lb_ref[...])


def _peer(x, mod, wq_bf, keys_bf, u_bf, vt_bf, ln_g, ln_b, layer, grp):
    nt, d = x.shape
    t = PEER_T
    nlt = t // LANES
    eblk = PEER_NI1 * PEER_NKEYS
    n_eblk = u_bf.shape[0] // eblk
    nq = wq_bf.shape[1]
    sel_shape = (PEER_HEADS, nlt, PEER_NKEYS, LANES)
    return pl.pallas_call(
        functools.partial(_peer_kernel, n_eblk=n_eblk),
        grid=(nt // t, n_eblk),
        in_specs=[
            pl.BlockSpec((t, d), lambda b, e: (b, 0)),
            pl.BlockSpec((None, None, 6, d), lambda b, e: (layer, grp(b, t), 0, 0)),
            pl.BlockSpec((d, nq), lambda b, e: (0, 0)),
            pl.BlockSpec((PEER_HEADS, 2, PEER_NKEYS, PEER_DHALF), lambda b, e: (0, 0, 0, 0)),
            pl.BlockSpec((eblk, d), lambda b, e: (0, 0)),
            pl.BlockSpec((eblk, d), lambda b, e: (jnp.minimum(e + 1, n_eblk - 1), 0)),
            pl.BlockSpec((d, eblk), lambda b, e: (0, e)),
            pl.BlockSpec((None, 1, d), lambda b, e: (2 * layer + 1, 0, 0)),
            pl.BlockSpec((None, 1, d), lambda b, e: (2 * layer + 1, 0, 0)),
        ],
        out_specs=pl.BlockSpec((t, d), lambda b, e: (b, 0)),
        out_shape=jax.ShapeDtypeStruct((nt, d), F32),
        scratch_shapes=[
            pltpu.VMEM((t, d), BF16),
            pltpu.VMEM((2 * PEER_HEADS, t, PEER_DHALF), BF16),
            pltpu.VMEM(sel_shape, F32), pltpu.VMEM(sel_shape, F32),
            pltpu.VMEM(sel_shape, F32), pltpu.VMEM(sel_shape, F32),
            pltpu.VMEM((eblk, t), F32), pltpu.VMEM((eblk, t), F32),
            pltpu.VMEM((eblk, t), BF16),
            pltpu.VMEM((d, t), F32),
        ],
        compiler_params=_params(("parallel", "arbitrary")),
        name="peer",
    )(x, mod, wq_bf, keys_bf, u_bf, u_bf, vt_bf, ln_g, ln_b)


def _rope_angles(pos, dim):
    inv = ROPE_BASE ** (-jnp.arange(0, dim, 2, dtype=F32) / dim)
    ang = pos.astype(F32)[:, None] * inv[None, :]
    return jnp.cos(ang), jnp.sin(ang)


def _axial_tables(seq_len):
    t = jnp.arange(seq_len)
    half = ATT_DH // 2
    cr, sr = _rope_angles(t // GRID_W, half)
    cc, sc = _rope_angles(t % GRID_W, half)
    cos = jnp.concatenate([cr, cr, cc, cc], -1)
    sin = jnp.concatenate([-sr, sr, -sc, sc], -1)
    reps = LANES // ATT_DH
    return jnp.tile(cos, (1, reps)), jnp.tile(sin, (1, reps))


def kernel(x_prompt, x_sample, state_ret_fwd, state_ret_bwd, cache_k, cache_v, c, c_ctx, mod_w, mod_b, ln_g, ln_b,
           ret_w_in, ret_w_out, ret_decay, attn_w_in, attn_w_out, attn_sink, peer_wq, peer_keys, peer_u, peer_v):
    nb_p, len_p, d = x_prompt.shape
    nb_s, len_s, _ = x_sample.shape
    n_p = nb_p * len_p
    n_s = nb_s * len_s
    for tile in (PROJ_TM, OUT_TM, PEER_T):
        assert n_p % tile == 0 and len_s % tile == 0

    def grp(blk, tile):
        return _group_index(blk, n_p // tile, len_s // tile)

    x = jnp.concatenate([x_prompt.reshape(n_p, d), x_sample.reshape(n_s, d)], axis=0)
    n_grp = 1 + nb_s
    n_grp_pad = -(-n_grp // 8) * 8
    cond = jnp.concatenate([c_ctx[None, :], c, jnp.zeros((n_grp_pad - n_grp, d), F32)], axis=0)
    mod = _modulation(cond, mod_w, mod_b).reshape(DEPTH, n_grp_pad, 6, d)

    ln_g3 = ln_g.reshape(DEPTH * 2, 1, d)
    ln_b3 = ln_b.reshape(DEPTH * 2, 1, d)
    ret_rope = _rope_angles(jnp.arange(len_s), RET_DK)
    att_rope = _axial_tables(len_s)
    kv_w = ATT_HKV * ATT_DH
    cache_k4 = cache_k.reshape(cache_k.shape[0], cache_k.shape[1], cache_k.shape[2], kv_w)
    cache_v4 = cache_v.reshape(cache_v.shape[0], cache_v.shape[1], cache_v.shape[2], kv_w)

    new_sf, new_sb, new_k, new_v = [], [], [], []
    for i in range(DEPTH):
        j = i // N_MIXERS
        if i % N_MIXERS == 0:
            z = _project(x, mod, ret_w_in[j].astype(BF16), i, 0, grp)
            tables = _ret_tables(ret_decay[j])
            o_p, sf, sb = _retention(z, tables, 0, nb_p, len_p, False)
            o_s = _retention(z, tables, n_p, nb_s, len_s, True, rope=ret_rope,
                             states=(state_ret_fwd, state_ret_bwd), j=j)
            new_sf.append(sf)
            new_sb.append(sb)
            x = _ret_out(o_p, o_s, z, x, mod, ret_w_out[j].astype(BF16), ln_g3, ln_b3, i, grp)
        else:
            z = _project(x, mod, attn_w_in[j].astype(BF16), i, 0, grp)
            sink = attn_sink[j].astype(F32)
            o_p = _attn_context(z, sink, nb_p, len_p)
            o_s = _attn_latent(z, sink, cache_k4, cache_v4, att_rope, n_p, nb_s, len_s, j)
            hq_w = ATT_HQ * ATT_DH
            new_k.append(z[:n_p, hq_w:hq_w + kv_w].reshape(nb_p, len_p, ATT_HKV, ATT_DH))
            new_v.append(z[:n_p, hq_w + kv_w:hq_w + 2 * kv_w].reshape(nb_p, len_p, ATT_HKV, ATT_DH))
            x = _attn_out(o_p, o_s, x, mod, attn_w_out[j].astype(BF16), ln_g3, ln_b3, i, grp)
        x = _peer(x, mod, peer_wq[i].astype(BF16), peer_keys[i].astype(BF16), peer_u[i].astype(BF16),
                  peer_v[i].T.astype(BF16), ln_g3, ln_b3, i, grp)

    y_p = x[:n_p].reshape(nb_p, len_p, d)
    y_s = x[n_p:].reshape(nb_s, len_s, d)
    return (y_p, y_s, jnp.stack(new_sf, 1), jnp.stack(new_sb, 1), jnp.stack(new_k, 1), jnp.stack(new_v, 1))
```

```python
import functools
import math

import jax
import jax.numpy as jnp
from jax import lax
from jax.experimental import pallas as pl
from jax.experimental.pallas import tpu as pltpu

F32 = jnp.float32
BF16 = jnp.bfloat16

DEPTH = 4
N_MIXERS = 2
GRID_W = 64
RET_HEADS = 4
RET_DK = 256
RET_DV = 512
RET_CHUNK = 128
ATT_HQ = 16
ATT_HKV = 4
ATT_G = ATT_HQ // ATT_HKV
ATT_DH = 64
ATT_BLK = 128
ROPE_BASE = 10000.0
PEER_HEADS = 8
PEER_NKEYS = 128
PEER_DHALF = 128
PEER_TOPK = 16
DEEPNORM_ALPHA = (2.0 * DEPTH) ** 0.25
LN_EPS = 1e-5
GN_EPS = 1e-5
NEG_INF = -1e30
GELU_C = math.sqrt(2.0 / math.pi)

LANES = 128
VMEM_LIMIT = 56 * 1024 * 1024
PROJ_TM = 512
OUT_TM = 256
PEER_T = 512
PEER_NI1 = 8
PEER_ILG = 4
PEER_NSUB = 2


def _dot(a, b):
    return jnp.dot(a, b, preferred_element_type=F32)


def _dot_nt(a, b):
    return lax.dot_general(a, b, (((1,), (1,)), ((), ())), preferred_element_type=F32)


def _params(sem, flags=None):
    return pltpu.CompilerParams(dimension_semantics=sem, vmem_limit_bytes=VMEM_LIMIT, flags=flags)


def _group_index(blk, n_prompt_blocks, blocks_per_seq):
    return jnp.where(blk < n_prompt_blocks, 0, 1 + (blk - n_prompt_blocks) // blocks_per_seq)


def _layer_norm(r, g, b):
    mu = jnp.mean(r, -1, keepdims=True)
    d = r - mu
    var = jnp.mean(d * d, -1, keepdims=True)
    return d * lax.rsqrt(var + LN_EPS) * g + b


def _mod_kernel(c_ref, w_ref, b_ref, o_ref):
    c = c_ref[...]
    a = (c * jax.nn.sigmoid(c)).astype(BF16)
    o_ref[...] = _dot(a, w_ref[...].astype(BF16)) + b_ref[...]


def _modulation(cond, mod_w, mod_b):
    ngp, d = cond.shape
    n_out = mod_w.shape[-1]
    tn = 1536
    return pl.pallas_call(
        _mod_kernel,
        grid=(DEPTH, n_out // tn),
        in_specs=[
            pl.BlockSpec((ngp, d), lambda i, n: (0, 0)),
            pl.BlockSpec((None, d, tn), lambda i, n: (i, 0, n)),
            pl.BlockSpec((None, 1, tn), lambda i, n: (i, 0, n)),
        ],
        out_specs=pl.BlockSpec((None, ngp, tn), lambda i, n: (i, 0, n)),
        out_shape=jax.ShapeDtypeStruct((DEPTH, ngp, n_out), F32),
        compiler_params=_params(("parallel", "parallel")),
        name="modulation",
    )(cond, mod_w, mod_b.reshape(DEPTH, 1, n_out))


def _proj_kernel(x_ref, m_ref, w_ref, o_ref, *, off):
    sh = m_ref[off:off + 1, :]
    sc = m_ref[off + 1:off + 2, :]
    h = (x_ref[...] * (1.0 + sc) + sh).astype(BF16)
    o_ref[...] = _dot(h, w_ref[...])


def _project(x, mod, w_bf, layer, off, grp):
    nt, d = x.shape
    n = w_bf.shape[1]
    tn = 1536
    tm = PROJ_TM
    return pl.pallas_call(
        functools.partial(_proj_kernel, off=off),
        grid=(n // tn, nt // tm),
        in_specs=[
            pl.BlockSpec((tm, d), lambda j, b: (b, 0)),
            pl.BlockSpec((None, None, 6, d), lambda j, b: (layer, grp(b, tm), 0, 0)),
            pl.BlockSpec((d, tn), lambda j, b: (0, j)),
        ],
        out_specs=pl.BlockSpec((tm, tn), lambda j, b: (b, j)),
        out_shape=jax.ShapeDtypeStruct((nt, n), F32),
        compiler_params=_params(("parallel", "parallel")),
        name="mod_proj",
    )(x, mod, w_bf)


def _rope_half(x, cos, sin):
    x1 = x[:, :LANES]
    x2 = x[:, LANES:]
    return jnp.concatenate([x1 * cos - x2 * sin, x1 * sin + x2 * cos], axis=1)


def _ret_kernel(*refs, latent, nc):
    it = iter(refs)
    qa, ka, va, qb, kb, vb = (next(it) for _ in range(6))
    if latent:
        cosa, sina, cosb, sinb = (next(it) for _ in range(4))
    dfb, qdf, qdb, kdf, kdb, cdf, cdb = (next(it) for _ in range(7))
    if latent:
        s0f, s0b = next(it), next(it)
    o_ref = next(it)
    if not latent:
        sf_out, sb_out = next(it), next(it)
    sf, sb = next(it), next(it)

    c = pl.program_id(2)

    @pl.when(c == 0)
    def _init():
        o_ref[...] = jnp.zeros(o_ref.shape, F32)
        if latent:
            sf[...] = s0f[...]
            sb[...] = s0b[...]
        else:
            sf[...] = jnp.zeros(sf.shape, F32)
            sb[...] = jnp.zeros(sb.shape, F32)

    def prep(q_ref, k_ref, cos_ref, sin_ref):
        q = q_ref[...]
        k = k_ref[...] * (RET_DK ** -0.5)
        if latent:
            cos = cos_ref[...]
            sin = sin_ref[...]
            q = _rope_half(q, cos, sin)
            k = _rope_half(k, cos, sin)
        return q, k

    q, k = prep(qa, ka, cosa if latent else None, sina if latent else None)
    v = va[...].astype(BF16)
    att = (_dot_nt(q.astype(BF16), k.astype(BF16)) * dfb[...]).astype(BF16)
    o_c = _dot(att, v) + _dot((q * qdf[...]).astype(BF16), sf[...].astype(BF16))
    rows_c = pl.ds(pl.multiple_of(c * RET_CHUNK, RET_CHUNK), RET_CHUNK)
    o_ref[rows_c, :] += o_c
    kd_t = (k * kdf[...]).T.astype(BF16)
    sf[...] = sf[...] * cdf[...] + _dot(kd_t, v)

    q2, k2 = prep(qb, kb, cosb if latent else None, sinb if latent else None)
    v2 = vb[...].astype(BF16)
    o_b = _dot((q2 * qdb[...]).astype(BF16), sb[...].astype(BF16))
    rows_b = pl.ds(pl.multiple_of((nc - 1 - c) * RET_CHUNK, RET_CHUNK), RET_CHUNK)
    o_ref[rows_b, :] += o_b
    kd2_t = (k2 * kdb[...]).T.astype(BF16)
    sb[...] = sb[...] * cdb[...] + _dot(kd2_t, v2)

    if not latent:
        @pl.when(c == nc - 1)
        def _fin():
            sf_out[...] = sf[...]
            sb_out[...] = sb[...]


def _ret_tables(decay):
    log_g = jax.nn.log_sigmoid(decay.astype(F32))
    lf, lb = log_g[0], log_g[1]
    idx = jnp.arange(RET_CHUNK, dtype=F32)
    diff = idx[:, None] - idx[None, :]
    fmask = diff >= 0
    bmask = diff < 0
    df = jnp.where(fmask[None], jnp.exp(jnp.where(fmask, diff, 0.0)[None] * lf[:, None, None]), 0.0)
    db = jnp.where(bmask[None], jnp.exp(jnp.where(bmask, -diff, 0.0)[None] * lb[:, None, None]), 0.0)
    dfb = df + db
    def rows(e):
        return jnp.broadcast_to(e[:, :, None], (RET_HEADS, RET_CHUNK, RET_DK))
    qdf = rows(jnp.exp((idx + 1.0)[None, :] * lf[:, None]))
    kdf = rows(jnp.exp((RET_CHUNK - 1.0 - idx)[None, :] * lf[:, None]))
    qdb = rows(jnp.exp((RET_CHUNK - idx)[None, :] * lb[:, None]))
    kdb = rows(jnp.exp(idx[None, :] * lb[:, None]))
    cdf = jnp.broadcast_to(jnp.exp(RET_CHUNK * lf)[:, None, None], (RET_HEADS, 1, RET_DV))
    cdb = jnp.broadcast_to(jnp.exp(RET_CHUNK * lb)[:, None, None], (RET_HEADS, 1, RET_DV))
    return dfb, qdf, qdb, kdf, kdb, cdf, cdb


def _retention(z, tables, row0, nseq, seq_len, latent, rope=None, states=None, j=0):
    nc = seq_len // RET_CHUNK
    base = row0 // RET_CHUNK
    h_ = RET_HEADS
    kcol = (h_ * RET_DK) // RET_DK
    vcol = (2 * h_ * RET_DK) // RET_DV

    def fw(b, h, c):
        return base + b * nc + c

    def bw(b, h, c):
        return base + b * nc + (nc - 1 - c)

    in_specs = [
        pl.BlockSpec((RET_CHUNK, RET_DK), lambda b, h, c: (fw(b, h, c), h)),
        pl.BlockSpec((RET_CHUNK, RET_DK), lambda b, h, c: (fw(b, h, c), kcol + h)),
        pl.BlockSpec((RET_CHUNK, RET_DV), lambda b, h, c: (fw(b, h, c), vcol + h)),
        pl.BlockSpec((RET_CHUNK, RET_DK), lambda b, h, c: (bw(b, h, c), h)),
        pl.BlockSpec((RET_CHUNK, RET_DK), lambda b, h, c: (bw(b, h, c), kcol + h)),
        pl.BlockSpec((RET_CHUNK, RET_DV), lambda b, h, c: (bw(b, h, c), vcol + h)),
    ]
    args = [z, z, z, z, z, z]
    if latent:
        cos, sin = rope
        in_specs += [
            pl.BlockSpec((RET_CHUNK, LANES), lambda b, h, c: (c, 0)),
            pl.BlockSpec((RET_CHUNK, LANES), lambda b, h, c: (c, 0)),
            pl.BlockSpec((RET_CHUNK, LANES), lambda b, h, c: (nc - 1 - c, 0)),
            pl.BlockSpec((RET_CHUNK, LANES), lambda b, h, c: (nc - 1 - c, 0)),
        ]
        args += [cos, sin, cos, sin]
    dfb, qdf, qdb, kdf, kdb, cdf, cdb = tables
    in_specs += [pl.BlockSpec((None, RET_CHUNK, RET_CHUNK), lambda b, h, c: (h, 0, 0))]
    in_specs += [pl.BlockSpec((None, RET_CHUNK, RET_DK), lambda b, h, c: (h, 0, 0))] * 4
    in_specs += [pl.BlockSpec((None, 1, RET_DV), lambda b, h, c: (h, 0, 0))] * 2
    args += [dfb, qdf, qdb, kdf, kdb, cdf, cdb]
    if latent:
        s0f, s0b = states
        st_spec = pl.BlockSpec((None, None, None, RET_DK, RET_DV), lambda b, h, c: (b, j, h, 0, 0))
        in_specs += [st_spec, st_spec]
        args += [s0f, s0b]

    o_shape = jax.ShapeDtypeStruct((nseq * seq_len, h_ * RET_DV), F32)
    o_spec = pl.BlockSpec((seq_len, RET_DV), lambda b, h, c: (b, h))
    if latent:
        out_shape, out_specs = o_shape, o_spec
    else:
        s_shape = jax.ShapeDtypeStruct((nseq, h_, RET_DK, RET_DV), F32)
        s_spec = pl.BlockSpec((None, None, RET_DK, RET_DV), lambda b, h, c: (b, h, 0, 0))
        out_shape, out_specs = (o_shape, s_shape, s_shape), (o_spec, s_spec, s_spec)

    return pl.pallas_call(
        functools.partial(_ret_kernel, latent=latent, nc=nc),
        grid=(nseq, h_, nc),
        in_specs=in_specs,
        out_specs=out_specs,
        out_shape=out_shape,
        scratch_shapes=[pltpu.VMEM((RET_DK, RET_DV), F32), pltpu.VMEM((RET_DK, RET_DV), F32)],
        compiler_params=_params(("parallel", "parallel", "arbitrary")),
        name="retention_latent" if latent else "retention_context",
    )(*args)


def _ret_out_kernel(op_ref, os_ref, g_ref, x_ref, m_ref, w_ref, lg_ref, lb_ref, out_ref, *, n_ctx_tiles):
    is_ctx = pl.program_id(0) < n_ctx_tiles
    y = None
    for h in range(RET_HEADS):
        cols = slice(h * RET_DV, (h + 1) * RET_DV)
        oh = jnp.where(is_ctx, op_ref[:, cols], os_ref[:, cols])
        mu = jnp.mean(oh, -1, keepdims=True)
        d = oh - mu
        var = jnp.mean(d * d, -1, keepdims=True)
        on = d * lax.rsqrt(var + GN_EPS)
        gh = g_ref[:, cols]
        act = (gh * jax.nn.sigmoid(gh) * on).astype(BF16)
        part = _dot(act, w_ref[cols, :])
        y = part if y is None else y + part
    gate = m_ref[2:3, :]
    r = DEEPNORM_ALPHA * x_ref[...] + gate * y
    out_ref[...] = _layer_norm(r, lg_ref[...], lb_ref[...])


def _ret_out(o_ctx, o_lat, z, x, mod, w_bf, ln_g, ln_b, layer, grp):
    nt, d = x.shape
    tm = OUT_TM
    hv = RET_HEADS * RET_DV
    gcol = (2 * RET_HEADS * RET_DK + hv) // hv
    n_ctx = o_ctx.shape[0] // tm
    return pl.pallas_call(
        functools.partial(_ret_out_kernel, n_ctx_tiles=n_ctx),
        grid=(nt // tm,),
        in_specs=[
            pl.BlockSpec((tm, hv), lambda b: (jnp.minimum(b, n_ctx - 1), 0)),
            pl.BlockSpec((tm, hv), lambda b: (jnp.maximum(b - n_ctx, 0), 0)),
            pl.BlockSpec((tm, hv), lambda b: (b, gcol)),
            pl.BlockSpec((tm, d), lambda b: (b, 0)),
            pl.BlockSpec((None, None, 6, d), lambda b: (layer, grp(b, tm), 0, 0)),
            pl.BlockSpec((hv, d), lambda b: (0, 0)),
            pl.BlockSpec((None, 1, d), lambda b: (2 * layer, 0, 0)),
            pl.BlockSpec((None, 1, d), lambda b: (2 * layer, 0, 0)),
        ],
        out_specs=pl.BlockSpec((tm, d), lambda b: (b, 0)),
        out_shape=jax.ShapeDtypeStruct((nt, d), F32),
        compiler_params=_params(("parallel",)),
        name="retention_out",
    )(o_ctx, o_lat, z, x, mod, w_bf, ln_g, ln_b)


def _softmax_pv(parts, sink):
    m = sink
    for s, _ in parts:
        m = jnp.maximum(m, jnp.max(s, -1, keepdims=True))
    den = jnp.exp(sink - m)
    o = None
    for s, v in parts:
        p = jnp.exp(s - m)
        den = den + jnp.sum(p, -1, keepdims=True)
        pv = _dot(p.astype(BF16), v)
        o = pv if o is None else o + pv
    return o / den


def _attn_ctx_kernel(sink_ref, q_ref, k_ref, v_ref, o_ref):
    scale = ATT_DH ** -0.5
    for h in range(ATT_HKV):
        kh = k_ref[:, h * ATT_DH:(h + 1) * ATT_DH].astype(BF16)
        vh = v_ref[:, h * ATT_DH:(h + 1) * ATT_DH].astype(BF16)
        for g in range(ATT_G):
            hq = h * ATT_G + g
            cols = slice(hq * ATT_DH, (hq + 1) * ATT_DH)
            qh = (q_ref[:, cols] * scale).astype(BF16)
            o_ref[:, cols] = _softmax_pv([(_dot_nt(qh, kh), vh)], sink_ref[hq])


def _attn_context(z, sink, nseq, seq_len):
    hq_w = ATT_HQ * ATT_DH
    kv_w = ATT_HKV * ATT_DH
    return pl.pallas_call(
        _attn_ctx_kernel,
        grid=(nseq,),
        in_specs=[
            pl.BlockSpec(memory_space=pltpu.SMEM),
            pl.BlockSpec((seq_len, hq_w), lambda b: (b, 0)),
            pl.BlockSpec((seq_len, kv_w), lambda b: (b, hq_w // kv_w)),
            pl.BlockSpec((seq_len, kv_w), lambda b: (b, hq_w // kv_w + 1)),
        ],
        out_specs=pl.BlockSpec((seq_len, hq_w), lambda b: (b, 0)),
        out_shape=jax.ShapeDtypeStruct((nseq * seq_len, hq_w), F32),
        compiler_params=_params(("parallel",)),
        name="attention_context",
    )(sink, z, z, z)


def _rope_axial(x, cos, sin):
    w = x.shape[1]
    lane = lax.broadcasted_iota(jnp.int32, (x.shape[0], LANES), 1)
    first = (lane % 32) < 16
    outs = []
    for cg in range(w // LANES):
        xg = x[:, cg * LANES:(cg + 1) * LANES]
        up = pltpu.roll(xg, LANES - 16, 1)
        dn = pltpu.roll(xg, 16, 1)
        outs.append(xg * cos + jnp.where(first, up, dn) * sin)
    return jnp.concatenate(outs, axis=1) if len(outs) > 1 else outs[0]


def _attn_lat_kernel(sink_ref, q_ref, kp_ref, kc_ref, kn_ref, vp_ref, vc_ref, vn_ref, ck_ref, cv_ref,
                     cq_ref, sq_ref, cp_ref, sp_ref, cn_ref, sn_ref, o_ref, *, nb):
    scale = ATT_DH ** -0.5
    qb = pl.program_id(1)
    q = _rope_axial(q_ref[...], cq_ref[...], sq_ref[...]) * scale
    kw = jnp.concatenate([
        _rope_axial(kp_ref[...], cp_ref[...], sp_ref[...]),
        _rope_axial(kc_ref[...], cq_ref[...], sq_ref[...]),
        _rope_axial(kn_ref[...], cn_ref[...], sn_ref[...]),
    ], axis=0)
    vw = jnp.concatenate([vp_ref[...], vc_ref[...], vn_ref[...]], axis=0)
    ck = ck_ref[...]
    cv = cv_ref[...]

    rows = ATT_G * ATT_BLK
    i = lax.broadcasted_iota(jnp.int32, (rows, 3 * ATT_BLK), 0) % ATT_BLK
    jj = lax.broadcasted_iota(jnp.int32, (rows, 3 * ATT_BLK), 1)
    lo = jnp.where(qb > 0, 0, ATT_BLK)
    hi = jnp.where(qb < nb - 1, 3 * ATT_BLK, 2 * ATT_BLK)
    valid = (jj >= jnp.maximum(i, lo)) & (jj < jnp.minimum(i + 2 * ATT_BLK + 1, hi))
    gidx = lax.broadcasted_iota(jnp.int32, (rows, 1), 0) // ATT_BLK

    for h in range(ATT_HKV):
        hs = slice(h * ATT_DH, (h + 1) * ATT_DH)
        kh = kw[:, hs].astype(BF16)
        vh = vw[:, hs].astype(BF16)
        ckh = ck[:, hs].astype(BF16)
        cvh = cv[:, hs].astype(BF16)
        heads = [h * ATT_G + g for g in range(ATT_G)]
        qg = jnp.concatenate([q[:, hq * ATT_DH:(hq + 1) * ATT_DH] for hq in heads], axis=0).astype(BF16)
        sink = jnp.zeros((rows, 1), F32)
        for g, hq in enumerate(heads):
            sink = jnp.where(gidx == g, sink_ref[hq], sink)
        s_loc = jnp.where(valid, _dot_nt(qg, kh), NEG_INF)
        s_ctx = _dot_nt(qg, ckh)
        o = _softmax_pv([(s_loc, vh), (s_ctx, cvh)], sink)
        for g, hq in enumerate(heads):
            o_ref[:, hq * ATT_DH:(hq + 1) * ATT_DH] = o[g * ATT_BLK:(g + 1) * ATT_BLK, :]


def _attn_latent(z, sink, cache_k, cache_v, rope, row0, nseq, seq_len, j):
    nb = seq_len // ATT_BLK
    base = row0 // ATT_BLK
    hq_w = ATT_HQ * ATT_DH
    kv_w = ATT_HKV * ATT_DH
    kcol = hq_w // kv_w
    past = cache_k.shape[2]
    cos, sin = rope

    def prv(q):
        return jnp.maximum(q - 1, 0)

    def nxt(q):
        return jnp.minimum(q + 1, nb - 1)

    def kv_spec(sel, col):
        return pl.BlockSpec((ATT_BLK, kv_w), lambda b, q: (base + b * nb + sel(q), col))

    def rope_spec(sel):
        return pl.BlockSpec((ATT_BLK, LANES), lambda b, q: (sel(q), 0))

    same = lambda q: q
    cache_spec = pl.BlockSpec((None, None, past, kv_w), lambda b, q: (b, j, 0, 0))
    return pl.pallas_call(
        functools.partial(_attn_lat_kernel, nb=nb),
        grid=(nseq, nb),
        in_specs=[
            pl.BlockSpec(memory_space=pltpu.SMEM),
            pl.BlockSpec((ATT_BLK, hq_w), lambda b, q: (base + b * nb + q, 0)),
            kv_spec(prv, kcol), kv_spec(same, kcol), kv_spec(nxt, kcol),
            kv_spec(prv, kcol + 1), kv_spec(same, kcol + 1), kv_spec(nxt, kcol + 1),
            cache_spec, cache_spec,
            rope_spec(same), rope_spec(same), rope_spec(prv), rope_spec(prv), rope_spec(nxt), rope_spec(nxt),
        ],
        out_specs=pl.BlockSpec((ATT_BLK, hq_w), lambda b, q: (b * nb + q, 0)),
        out_shape=jax.ShapeDtypeStruct((nseq * seq_len, hq_w), F32),
        compiler_params=_params(("parallel", "parallel")),
        name="attention_latent",
    )(sink, z, z, z, z, z, z, z, cache_k, cache_v, cos, sin, cos, sin, cos, sin)


def _attn_out_kernel(op_ref, os_ref, x_ref, m_ref, w_ref, lg_ref, lb_ref, out_ref, *, n_ctx_tiles):
    is_ctx = pl.program_id(0) < n_ctx_tiles
    o = jnp.where(is_ctx, op_ref[...], os_ref[...])
    y = _dot(o.astype(BF16), w_ref[...])
    gate = m_ref[2:3, :]
    r = DEEPNORM_ALPHA * x_ref[...] + gate * y
    out_ref[...] = _layer_norm(r, lg_ref[...], lb_ref[...])


def _attn_out(o_ctx, o_lat, x, mod, w_bf, ln_g, ln_b, layer, grp):
    nt, d = x.shape
    tm = OUT_TM
    k = o_ctx.shape[1]
    n_ctx = o_ctx.shape[0] // tm
    return pl.pallas_call(
        functools.partial(_attn_out_kernel, n_ctx_tiles=n_ctx),
        grid=(nt // tm,),
        in_specs=[
            pl.BlockSpec((tm, k), lambda b: (jnp.minimum(b, n_ctx - 1), 0)),
            pl.BlockSpec((tm, k), lambda b: (jnp.maximum(b - n_ctx, 0), 0)),
            pl.BlockSpec((tm, d), lambda b: (b, 0)),
            pl.BlockSpec((None, None, 6, d), lambda b: (layer, grp(b, tm), 0, 0)),
            pl.BlockSpec((k, d), lambda b: (0, 0)),
            pl.BlockSpec((None, 1, d), lambda b: (2 * layer, 0, 0)),
            pl.BlockSpec((None, 1, d), lambda b: (2 * layer, 0, 0)),
        ],
        out_specs=pl.BlockSpec((tm, d), lambda b: (b, 0)),
        out_shape=jax.ShapeDtypeStruct((nt, d), F32),
        compiler_params=_params(("parallel",)),
        name="attention_out",
    )(o_ctx, o_lat, x, mod, w_bf, ln_g, ln_b)


def _sort_network(n):
    pairs = []

    def merge(lo, hi, r):
        step = r * 2
        if step < hi - lo:
            merge(lo, hi, step)
            merge(lo + r, hi, step)
            for i in range(lo + r, hi - r, step):
                pairs.append((i, i + r))
        else:
            pairs.append((lo, lo + r))

    def sort(lo, hi):
        if hi - lo >= 1:
            mid = lo + (hi - lo) // 2
            sort(lo, mid)
            sort(mid + 1, hi)
            merge(lo, hi, 1)

    sort(0, n - 1)
    return pairs


def _pop_top(lists, k):
    vals = []
    for r in range(k):
        head = lists[0]
        m = jnp.max(head, axis=0, keepdims=True)
        vals.append(m)
        left = k - 1 - r
        if left > 0:
            took = head >= m
            nxt = [lists[i + 1] if i + 1 < len(lists) else NEG_INF for i in range(left)]
            lists = [jnp.where(took, nxt[i], lists[i]) for i in range(left)]
    return vals


def _top_values(s, k):
    tiles = [s[8 * j:8 * (j + 1), :] for j in range(s.shape[0] // 8)]
    for i, j in _sort_network(len(tiles)):
        tiles[i], tiles[j] = jnp.maximum(tiles[i], tiles[j]), jnp.minimum(tiles[i], tiles[j])
    return _pop_top(tiles, k)


def _peer_kernel(x_ref, m_ref, wq_ref, keys_ref, u0_ref, u_ref, vt_ref, lg_ref, lb_ref, out_ref,
                 h_scr, q_scr, d1_scr, e1_scr, s2_scr, e2_scr, at0_scr, at1_scr, g_scr, acc_scr,
                 *, n_eblk):
    t = x_ref.shape[0]
    nlt = t // LANES
    e = pl.program_id(1)
    k = PEER_TOPK
    nk = PEER_NKEYS
    hk = nk // PEER_NSUB

    @pl.when(e == 0)
    def _scores():
        sh = m_ref[3:4, :]
        sc = m_ref[4:5, :]
        h_scr[...] = (x_ref[...] * (1.0 + sc) + sh).astype(BF16)
        q = _dot(h_scr[...], wq_ref[...])
        for ph in range(2 * PEER_HEADS):
            q_scr[ph] = q[:, ph * PEER_DHALF:(ph + 1) * PEER_DHALF].astype(BF16)
        acc_scr[...] = jnp.zeros(acc_scr.shape, F32)
        at0_scr[...] = _dot_nt(u0_ref[...], h_scr[...])

        def unit(uidx, carry):
            lt = uidx // PEER_HEADS
            p = uidx % PEER_HEADS
            t0 = pl.multiple_of(lt * LANES, LANES)
            s, vals = [], []
            for hh in range(2):
                sk = _dot_nt(keys_ref[p, hh], q_scr[2 * p + hh, pl.ds(t0, LANES), :])
                s.append(sk)
                vals.append(_top_values(sk, k))
            v1m = [v - vals[0][0] for v in vals[0]]
            v2m = [v - vals[1][0] for v in vals[1]]
            row = lax.broadcasted_iota(jnp.int32, (8, LANES), 0)
            base = jnp.zeros((8, LANES), F32)
            for r in range(4):
                base = jnp.where(row == r, v1m[r], base)
                base = jnp.where(row == 4 + r, v2m[r], base)
            cands = [base + jnp.where(row < 4, v2m[i], v1m[i + 4] if i + 4 < k else NEG_INF) for i in range(k)]
            tops = _pop_top(cands, k + 1)
            thr = 0.5 * (tops[k - 1] + tops[k])
            zsum = jnp.exp(tops[0])
            for r in range(1, k):
                zsum = zsum + jnp.exp(tops[r])
            s1m = jnp.where(s[0] >= vals[0][k - 1], s[0] - vals[0][0], NEG_INF)
            s2m = jnp.where(s[1] >= vals[1][k - 1], s[1] - vals[1][0], NEG_INF)
            d1 = thr - s1m
            e1 = jnp.exp(s1m) * (0.5 / zsum)
            for j in range(nk // PEER_NI1):
                d1_scr[j, p, lt] = d1[j * PEER_NI1:(j + 1) * PEER_NI1, :]
                e1_scr[j, p, lt] = e1[j * PEER_NI1:(j + 1) * PEER_NI1, :]
            s2_scr[p, lt] = s2m
            e2_scr[p, lt] = jnp.exp(s2m)
            return carry

        lax.fori_loop(0, nlt * PEER_HEADS, unit, 0, unroll=2)

    def step(at_r, at_w):
        mxu_w = 2 * LANES
        il_grp = PEER_ILG
        for ts_i in range(t // mxu_w):
            ts = slice(ts_i * mxu_w, (ts_i + 1) * mxu_w)
            at_w[:, ts] = _dot_nt(u_ref[...], h_scr[ts, :])
            for ig in range(PEER_NI1 // il_grp):
                for lt in range(ts_i * mxu_w // LANES, (ts_i + 1) * mxu_w // LANES):
                    tl = slice(lt * LANES, (lt + 1) * LANES)
                    for hf in range(PEER_NSUB):
                        rows2 = slice(hf * hk, (hf + 1) * hk)
                        ils = range(ig * il_grp, (ig + 1) * il_grp)
                        w = {il: jnp.zeros((hk, LANES), F32) for il in ils}
                        for p in range(PEER_HEADS):
                            s2h = s2_scr[p, lt, rows2, :]
                            e2h = e2_scr[p, lt, rows2, :]
                            for il in ils:
                                d1 = d1_scr[e, p, lt, il:il + 1, :]
                                e1 = e1_scr[e, p, lt, il:il + 1, :]
                                w[il] = w[il] + jnp.where(s2h >= d1, e2h, 0.0) * e1
                        for il in ils:
                            rows = slice(il * nk + hf * hk, il * nk + (hf + 1) * hk)
                            a = at_r[rows, tl]
                            u = a * (a * a * (GELU_C * 0.044715) + GELU_C)
                            g_scr[rows, tl] = (w[il] * (a + a * jnp.tanh(u))).astype(BF16)
                ks = slice(ig * il_grp * nk, (ig + 1) * il_grp * nk)
                acc_scr[:, ts] += _dot(vt_ref[:, ks], g_scr[ks, ts])

    @pl.when(e % 2 == 0)
    def _even():
        step(at0_scr, at1_scr)

    @pl.when(e % 2 == 1)
    def _odd():
        step(at1_scr, at0_scr)

    @pl.when(e == n_eblk - 1)
    def _finish():
        y = acc_scr[...].T
        gate = m_ref[5:6, :]
        r = DEEPNORM_ALPHA * x_ref[...] + gate * y
        out_ref[...] = _layer_norm(r, lg_ref[...], lb_ref[...])


def _peer(x, mod, wq_bf, keys_bf, u_bf, vt_bf, ln_g, ln_b, layer, grp):
    nt, d = x.shape
    t = PEER_T
    nlt = t // LANES
    eblk = PEER_NI1 * PEER_NKEYS
    n_eblk = u_bf.shape[0] // eblk
    nq = wq_bf.shape[1]
    sel_shape = (PEER_HEADS, nlt, PEER_NKEYS, LANES)
    assert PEER_NI1 % 8 == 0
    blk_shape = (n_eblk, PEER_HEADS, nlt, PEER_NI1, LANES)
    return pl.pallas_call(
        functools.partial(_peer_kernel, n_eblk=n_eblk),
        grid=(nt // t, n_eblk),
        in_specs=[
            pl.BlockSpec((t, d), lambda b, e: (b, 0)),
            pl.BlockSpec((None, None, 6, d), lambda b, e: (layer, grp(b, t), 0, 0)),
            pl.BlockSpec((d, nq), lambda b, e: (0, 0)),
            pl.BlockSpec((PEER_HEADS, 2, PEER_NKEYS, PEER_DHALF), lambda b, e: (0, 0, 0, 0)),
            pl.BlockSpec((eblk, d), lambda b, e: (0, 0)),
            pl.BlockSpec((eblk, d), lambda b, e: (jnp.minimum(e + 1, n_eblk - 1), 0)),
            pl.BlockSpec((d, eblk), lambda b, e: (0, e)),
            pl.BlockSpec((None, 1, d), lambda b, e: (2 * layer + 1, 0, 0)),
            pl.BlockSpec((None, 1, d), lambda b, e: (2 * layer + 1, 0, 0)),
        ],
        out_specs=pl.BlockSpec((t, d), lambda b, e: (b, 0)),
        out_shape=jax.ShapeDtypeStruct((nt, d), F32),
        scratch_shapes=[
            pltpu.VMEM((t, d), BF16),
            pltpu.VMEM((2 * PEER_HEADS, t, PEER_DHALF), BF16),
            pltpu.VMEM(blk_shape, F32), pltpu.VMEM(blk_shape, F32),
            pltpu.VMEM(sel_shape, F32), pltpu.VMEM(sel_shape, F32),
            pltpu.VMEM((eblk, t), F32), pltpu.VMEM((eblk, t), F32),
            pltpu.VMEM((eblk, t), BF16),
            pltpu.VMEM((d, t), F32),
        ],
        compiler_params=_params(("parallel", "arbitrary")),
        name="peer",
    )(x, mod, wq_bf, keys_bf, u_bf, u_bf, vt_bf, ln_g, ln_b)


def _rope_angles(pos, dim):
    inv = ROPE_BASE ** (-jnp.arange(0, dim, 2, dtype=F32) / dim)
    ang = pos.astype(F32)[:, None] * inv[None, :]
    return jnp.cos(ang), jnp.sin(ang)


def _axial_tables(seq_len):
    t = jnp.arange(seq_len)
    half = ATT_DH // 2
    cr, sr = _rope_angles(t // GRID_W, half)
    cc, sc = _rope_angles(t % GRID_W, half)
    cos = jnp.concatenate([cr, cr, cc, cc], -1)
    sin = jnp.concatenate([-sr, sr, -sc, sc], -1)
    reps = LANES // ATT_DH
    return jnp.tile(cos, (1, reps)), jnp.tile(sin, (1, reps))


def kernel(x_prompt, x_sample, state_ret_fwd, state_ret_bwd, cache_k, cache_v, c, c_ctx, mod_w, mod_b, ln_g, ln_b,
           ret_w_in, ret_w_out, ret_decay, attn_w_in, attn_w_out, attn_sink, peer_wq, peer_keys, peer_u, peer_v):
    nb_p, len_p, d = x_prompt.shape
    nb_s, len_s, _ = x_sample.shape
    n_p = nb_p * len_p
    n_s = nb_s * len_s
    for tile in (PROJ_TM, OUT_TM, PEER_T):
        assert n_p % tile == 0 and len_s % tile == 0

    def grp(blk, tile):
        return _group_index(blk, n_p // tile, len_s // tile)

    x = jnp.concatenate([x_prompt.reshape(n_p, d), x_sample.reshape(n_s, d)], axis=0)
    n_grp = 1 + nb_s
    n_grp_pad = -(-n_grp // 8) * 8
    cond = jnp.concatenate([c_ctx[None, :], c, jnp.zeros((n_grp_pad - n_grp, d), F32)], axis=0)
    mod = _modulation(cond, mod_w, mod_b).reshape(DEPTH, n_grp_pad, 6, d)

    ln_g3 = ln_g.reshape(DEPTH * 2, 1, d)
    ln_b3 = ln_b.reshape(DEPTH * 2, 1, d)
    ret_rope = _rope_angles(jnp.arange(len_s), RET_DK)
    att_rope = _axial_tables(len_s)
    kv_w = ATT_HKV * ATT_DH
    cache_k4 = cache_k.reshape(cache_k.shape[0], cache_k.shape[1], cache_k.shape[2], kv_w)
    cache_v4 = cache_v.reshape(cache_v.shape[0], cache_v.shape[1], cache_v.shape[2], kv_w)

    new_sf, new_sb, new_k, new_v = [], [], [], []
    for i in range(DEPTH):
        j = i // N_MIXERS
        if i % N_MIXERS == 0:
            z = _project(x, mod, ret_w_in[j].astype(BF16), i, 0, grp)
            tables = _ret_tables(ret_decay[j])
            o_p, sf, sb = _retention(z, tables, 0, nb_p, len_p, False)
            o_s = _retention(z, tables, n_p, nb_s, len_s, True, rope=ret_rope,
                             states=(state_ret_fwd, state_ret_bwd), j=j)
            new_sf.append(sf)
            new_sb.append(sb)
            x = _ret_out(o_p, o_s, z, x, mod, ret_w_out[j].astype(BF16), ln_g3, ln_b3, i, grp)
        else:
            z = _project(x, mod, attn_w_in[j].astype(BF16), i, 0, grp)
            sink = attn_sink[j].astype(F32)
            o_p = _attn_context(z, sink, nb_p, len_p)
            o_s = _attn_latent(z, sink, cache_k4, cache_v4, att_rope, n_p, nb_s, len_s, j)
            hq_w = ATT_HQ * ATT_DH
            new_k.append(z[:n_p, hq_w:hq_w + kv_w].reshape(nb_p, len_p, ATT_HKV, ATT_DH))
            new_v.append(z[:n_p, hq_w + kv_w:hq_w + 2 * kv_w].reshape(nb_p, len_p, ATT_HKV, ATT_DH))
            x = _attn_out(o_p, o_s, x, mod, attn_w_out[j].astype(BF16), ln_g3, ln_b3, i, grp)
        x = _peer(x, mod, peer_wq[i].astype(BF16), peer_keys[i].astype(BF16), peer_u[i].astype(BF16),
                  peer_v[i].T.astype(BF16), ln_g3, ln_b3, i, grp)

    y_p = x[:n_p].reshape(nb_p, len_p, d)
    y_s = x[n_p:].reshape(nb_s, len_s, d)
    return (y_p, y_s, jnp.stack(new_sf, 1), jnp.stack(new_sb, 1), jnp.stack(new_k, 1), jnp.stack(new_v, 1))
```

```python
import functools
import math

import jax
import jax.numpy as jnp
from jax import lax
from jax.experimental import pallas as pl
from jax.experimental.pallas import tpu as pltpu

F32 = jnp.float32
BF16 = jnp.bfloat16

DEPTH = 4
N_MIXERS = 2
GRID_W = 64
RET_HEADS = 4
RET_DK = 256
RET_DV = 512
RET_CHUNK = 128
ATT_HQ = 16
ATT_HKV = 4
ATT_G = ATT_HQ // ATT_HKV
ATT_DH = 64
ATT_BLK = 128
ROPE_BASE = 10000.0
PEER_HEADS = 8
PEER_NKEYS = 128
PEER_DHALF = 128
PEER_TOPK = 16
DEEPNORM_ALPHA = (2.0 * DEPTH) ** 0.25
LN_EPS = 1e-5
GN_EPS = 1e-5
NEG_INF = -1e30
GELU_C = math.sqrt(2.0 / math.pi)

LANES = 128
VMEM_LIMIT = 56 * 1024 * 1024
PROJ_TM = 512
OUT_TM = 256
PEER_T = 512
PEER_NI1 = 8
PEER_ILG = 4
PEER_NSUB = 2


def _dot(a, b):
    return jnp.dot(a, b, preferred_element_type=F32)


def _dot_nt(a, b):
    return lax.dot_general(a, b, (((1,), (1,)), ((), ())), preferred_element_type=F32)


def _params(sem, flags=None):
    return pltpu.CompilerParams(dimension_semantics=sem, vmem_limit_bytes=VMEM_LIMIT, flags=flags)


def _group_index(blk, n_prompt_blocks, blocks_per_seq):
    return jnp.where(blk < n_prompt_blocks, 0, 1 + (blk - n_prompt_blocks) // blocks_per_seq)


def _layer_norm(r, g, b):
    mu = jnp.mean(r, -1, keepdims=True)
    d = r - mu
    var = jnp.mean(d * d, -1, keepdims=True)
    return d * lax.rsqrt(var + LN_EPS) * g + b


def _mod_kernel(c_ref, w_ref, b_ref, o_ref):
    c = c_ref[...]
    a = (c * jax.nn.sigmoid(c)).astype(BF16)
    o_ref[...] = _dot(a, w_ref[...].astype(BF16)) + b_ref[...]


def _modulation(cond, mod_w, mod_b):
    ngp, d = cond.shape
    n_out = mod_w.shape[-1]
    tn = 1536
    return pl.pallas_call(
        _mod_kernel,
        grid=(DEPTH, n_out // tn),
        in_specs=[
            pl.BlockSpec((ngp, d), lambda i, n: (0, 0)),
            pl.BlockSpec((None, d, tn), lambda i, n: (i, 0, n)),
            pl.BlockSpec((None, 1, tn), lambda i, n: (i, 0, n)),
        ],
        out_specs=pl.BlockSpec((None, ngp, tn), lambda i, n: (i, 0, n)),
        out_shape=jax.ShapeDtypeStruct((DEPTH, ngp, n_out), F32),
        compiler_params=_params(("parallel", "parallel")),
        name="modulation",
    )(cond, mod_w, mod_b.reshape(DEPTH, 1, n_out))


def _proj_kernel(x_ref, m_ref, w_ref, o_ref, *, off):
    sh = m_ref[off:off + 1, :]
    sc = m_ref[off + 1:off + 2, :]
    h = (x_ref[...] * (1.0 + sc) + sh).astype(BF16)
    o_ref[...] = _dot(h, w_ref[...])


def _project(x, mod, w_bf, layer, off, grp):
    nt, d = x.shape
    n = w_bf.shape[1]
    tn = 1536
    tm = PROJ_TM
    return pl.pallas_call(
        functools.partial(_proj_kernel, off=off),
        grid=(n // tn, nt // tm),
        in_specs=[
            pl.BlockSpec((tm, d), lambda j, b: (b, 0)),
            pl.BlockSpec((None, None, 6, d), lambda j, b: (layer, grp(b, tm), 0, 0)),
            pl.BlockSpec((d, tn), lambda j, b: (0, j)),
        ],
        out_specs=pl.BlockSpec((tm, tn), lambda j, b: (b, j)),
        out_shape=jax.ShapeDtypeStruct((nt, n), F32),
        compiler_params=_params(("parallel", "parallel")),
        name="mod_proj",
    )(x, mod, w_bf)


def _rope_half(x, cos, sin):
    x1 = x[:, :LANES]
    x2 = x[:, LANES:]
    return jnp.concatenate([x1 * cos - x2 * sin, x1 * sin + x2 * cos], axis=1)


def _ret_kernel(*refs, latent, nc):
    it = iter(refs)
    qa, ka, va, qb, kb, vb = (next(it) for _ in range(6))
    if latent:
        cosa, sina, cosb, sinb = (next(it) for _ in range(4))
    dfb, qdf, qdb, kdf, kdb, cdf, cdb = (next(it) for _ in range(7))
    if latent:
        s0f, s0b = next(it), next(it)
    o_ref = next(it)
    if not latent:
        sf_out, sb_out = next(it), next(it)
    sf, sb = next(it), next(it)

    c = pl.program_id(2)

    @pl.when(c == 0)
    def _init():
        o_ref[...] = jnp.zeros(o_ref.shape, F32)
        if latent:
            sf[...] = s0f[...]
            sb[...] = s0b[...]
        else:
            sf[...] = jnp.zeros(sf.shape, F32)
            sb[...] = jnp.zeros(sb.shape, F32)

    def prep(q_ref, k_ref, cos_ref, sin_ref):
        q = q_ref[...]
        k = k_ref[...] * (RET_DK ** -0.5)
        if latent:
            cos = cos_ref[...]
            sin = sin_ref[...]
            q = _rope_half(q, cos, sin)
            k = _rope_half(k, cos, sin)
        return q, k

    q, k = prep(qa, ka, cosa if latent else None, sina if latent else None)
    v = va[...].astype(BF16)
    att = (_dot_nt(q.astype(BF16), k.astype(BF16)) * dfb[...]).astype(BF16)
    o_c = _dot(att, v) + _dot((q * qdf[...]).astype(BF16), sf[...].astype(BF16))
    rows_c = pl.ds(pl.multiple_of(c * RET_CHUNK, RET_CHUNK), RET_CHUNK)
    o_ref[rows_c, :] += o_c
    kd_t = (k * kdf[...]).T.astype(BF16)
    sf[...] = sf[...] * cdf[...] + _dot(kd_t, v)

    q2, k2 = prep(qb, kb, cosb if latent else None, sinb if latent else None)
    v2 = vb[...].astype(BF16)
    o_b = _dot((q2 * qdb[...]).astype(BF16), sb[...].astype(BF16))
    rows_b = pl.ds(pl.multiple_of((nc - 1 - c) * RET_CHUNK, RET_CHUNK), RET_CHUNK)
    o_ref[rows_b, :] += o_b
    kd2_t = (k2 * kdb[...]).T.astype(BF16)
    sb[...] = sb[...] * cdb[...] + _dot(kd2_t, v2)

    if not latent:
        @pl.when(c == nc - 1)
        def _fin():
            sf_out[...] = sf[...]
            sb_out[...] = sb[...]


def _ret_tables(decay):
    log_g = jax.nn.log_sigmoid(decay.astype(F32))
    lf, lb = log_g[0], log_g[1]
    idx = jnp.arange(RET_CHUNK, dtype=F32)
    diff = idx[:, None] - idx[None, :]
    fmask = diff >= 0
    bmask = diff < 0
    df = jnp.where(fmask[None], jnp.exp(jnp.where(fmask, diff, 0.0)[None] * lf[:, None, None]), 0.0)
    db = jnp.where(bmask[None], jnp.exp(jnp.where(bmask, -diff, 0.0)[None] * lb[:, None, None]), 0.0)
    dfb = df + db
    def rows(e):
        return jnp.broadcast_to(e[:, :, None], (RET_HEADS, RET_CHUNK, RET_DK))
    qdf = rows(jnp.exp((idx + 1.0)[None, :] * lf[:, None]))
    kdf = rows(jnp.exp((RET_CHUNK - 1.0 - idx)[None, :] * lf[:, None]))
    qdb = rows(jnp.exp((RET_CHUNK - idx)[None, :] * lb[:, None]))
    kdb = rows(jnp.exp(idx[None, :] * lb[:, None]))
    cdf = jnp.broadcast_to(jnp.exp(RET_CHUNK * lf)[:, None, None], (RET_HEADS, 1, RET_DV))
    cdb = jnp.broadcast_to(jnp.exp(RET_CHUNK * lb)[:, None, None], (RET_HEADS, 1, RET_DV))
    return dfb, qdf, qdb, kdf, kdb, cdf, cdb


def _retention(z, tables, row0, nseq, seq_len, latent, rope=None, states=None, j=0):
    nc = seq_len // RET_CHUNK
    base = row0 // RET_CHUNK
    h_ = RET_HEADS
    kcol = (h_ * RET_DK) // RET_DK
    vcol = (2 * h_ * RET_DK) // RET_DV

    def fw(b, h, c):
        return base + b * nc + c

    def bw(b, h, c):
        return base + b * nc + (nc - 1 - c)

    in_specs = [
        pl.BlockSpec((RET_CHUNK, RET_DK), lambda b, h, c: (fw(b, h, c), h)),
        pl.BlockSpec((RET_CHUNK, RET_DK), lambda b, h, c: (fw(b, h, c), kcol + h)),
        pl.BlockSpec((RET_CHUNK, RET_DV), lambda b, h, c: (fw(b, h, c), vcol + h)),
        pl.BlockSpec((RET_CHUNK, RET_DK), lambda b, h, c: (bw(b, h, c), h)),
        pl.BlockSpec((RET_CHUNK, RET_DK), lambda b, h, c: (bw(b, h, c), kcol + h)),
        pl.BlockSpec((RET_CHUNK, RET_DV), lambda b, h, c: (bw(b, h, c), vcol + h)),
    ]
    args = [z, z, z, z, z, z]
    if latent:
        cos, sin = rope
        in_specs += [
            pl.BlockSpec((RET_CHUNK, LANES), lambda b, h, c: (c, 0)),
            pl.BlockSpec((RET_CHUNK, LANES), lambda b, h, c: (c, 0)),
            pl.BlockSpec((RET_CHUNK, LANES), lambda b, h, c: (nc - 1 - c, 0)),
            pl.BlockSpec((RET_CHUNK, LANES), lambda b, h, c: (nc - 1 - c, 0)),
        ]
        args += [cos, sin, cos, sin]
    dfb, qdf, qdb, kdf, kdb, cdf, cdb = tables
    in_specs += [pl.BlockSpec((None, RET_CHUNK, RET_CHUNK), lambda b, h, c: (h, 0, 0))]
    in_specs += [pl.BlockSpec((None, RET_CHUNK, RET_DK), lambda b, h, c: (h, 0, 0))] * 4
    in_specs += [pl.BlockSpec((None, 1, RET_DV), lambda b, h, c: (h, 0, 0))] * 2
    args += [dfb, qdf, qdb, kdf, kdb, cdf, cdb]
    if latent:
        s0f, s0b = states
        st_spec = pl.BlockSpec((None, None, None, RET_DK, RET_DV), lambda b, h, c: (b, j, h, 0, 0))
        in_specs += [st_spec, st_spec]
        args += [s0f, s0b]

    o_shape = jax.ShapeDtypeStruct((nseq * seq_len, h_ * RET_DV), F32)
    o_spec = pl.BlockSpec((seq_len, RET_DV), lambda b, h, c: (b, h))
    if latent:
        out_shape, out_specs = o_shape, o_spec
    else:
        s_shape = jax.ShapeDtypeStruct((nseq, h_, RET_DK, RET_DV), F32)
        s_spec = pl.BlockSpec((None, None, RET_DK, RET_DV), lambda b, h, c: (b, h, 0, 0))
        out_shape, out_specs = (o_shape, s_shape, s_shape), (o_spec, s_spec, s_spec)

    return pl.pallas_call(
        functools.partial(_ret_kernel, latent=latent, nc=nc),
        grid=(nseq, h_, nc),
        in_specs=in_specs,
        out_specs=out_specs,
        out_shape=out_shape,
        scratch_shapes=[pltpu.VMEM((RET_DK, RET_DV), F32), pltpu.VMEM((RET_DK, RET_DV), F32)],
        compiler_params=_params(("parallel", "parallel", "arbitrary")),
        name="retention_latent" if latent else "retention_context",
    )(*args)


def _ret_out_kernel(op_ref, os_ref, g_ref, x_ref, m_ref, w_ref, lg_ref, lb_ref, out_ref, *, n_ctx_tiles):
    is_ctx = pl.program_id(0) < n_ctx_tiles
    y = None
    for h in range(RET_HEADS):
        cols = slice(h * RET_DV, (h + 1) * RET_DV)
        oh = jnp.where(is_ctx, op_ref[:, cols], os_ref[:, cols])
        mu = jnp.mean(oh, -1, keepdims=True)
        d = oh - mu
        var = jnp.mean(d * d, -1, keepdims=True)
        on = d * lax.rsqrt(var + GN_EPS)
        gh = g_ref[:, cols]
        act = (gh * jax.nn.sigmoid(gh) * on).astype(BF16)
        part = _dot(act, w_ref[cols, :])
        y = part if y is None else y + part
    gate = m_ref[2:3, :]
    r = DEEPNORM_ALPHA * x_ref[...] + gate * y
    out_ref[...] = _layer_norm(r, lg_ref[...], lb_ref[...])


def _ret_out(o_ctx, o_lat, z, x, mod, w_bf, ln_g, ln_b, layer, grp):
    nt, d = x.shape
    tm = OUT_TM
    hv = RET_HEADS * RET_DV
    gcol = (2 * RET_HEADS * RET_DK + hv) // hv
    n_ctx = o_ctx.shape[0] // tm
    return pl.pallas_call(
        functools.partial(_ret_out_kernel, n_ctx_tiles=n_ctx),
        grid=(nt // tm,),
        in_specs=[
            pl.BlockSpec((tm, hv), lambda b: (jnp.minimum(b, n_ctx - 1), 0)),
            pl.BlockSpec((tm, hv), lambda b: (jnp.maximum(b - n_ctx, 0), 0)),
            pl.BlockSpec((tm, hv), lambda b: (b, gcol)),
            pl.BlockSpec((tm, d), lambda b: (b, 0)),
            pl.BlockSpec((None, None, 6, d), lambda b: (layer, grp(b, tm), 0, 0)),
            pl.BlockSpec((hv, d), lambda b: (0, 0)),
            pl.BlockSpec((None, 1, d), lambda b: (2 * layer, 0, 0)),
            pl.BlockSpec((None, 1, d), lambda b: (2 * layer, 0, 0)),
        ],
        out_specs=pl.BlockSpec((tm, d), lambda b: (b, 0)),
        out_shape=jax.ShapeDtypeStruct((nt, d), F32),
        compiler_params=_params(("parallel",)),
        name="retention_out",
    )(o_ctx, o_lat, z, x, mod, w_bf, ln_g, ln_b)


def _softmax_pv(parts, sink):
    m = sink
    for s, _ in parts:
        m = jnp.maximum(m, jnp.max(s, -1, keepdims=True))
    den = jnp.exp(sink - m)
    o = None
    for s, v in parts:
        p = jnp.exp(s - m)
        den = den + jnp.sum(p, -1, keepdims=True)
        pv = _dot(p.astype(BF16), v)
        o = pv if o is None else o + pv
    return o / den


def _attn_ctx_kernel(sink_ref, q_ref, k_ref, v_ref, o_ref):
    scale = ATT_DH ** -0.5
    for h in range(ATT_HKV):
        kh = k_ref[:, h * ATT_DH:(h + 1) * ATT_DH].astype(BF16)
        vh = v_ref[:, h * ATT_DH:(h + 1) * ATT_DH].astype(BF16)
        for g in range(ATT_G):
            hq = h * ATT_G + g
            cols = slice(hq * ATT_DH, (hq + 1) * ATT_DH)
            qh = (q_ref[:, cols] * scale).astype(BF16)
            o_ref[:, cols] = _softmax_pv([(_dot_nt(qh, kh), vh)], sink_ref[hq])


def _attn_context(z, sink, nseq, seq_len):
    hq_w = ATT_HQ * ATT_DH
    kv_w = ATT_HKV * ATT_DH
    return pl.pallas_call(
        _attn_ctx_kernel,
        grid=(nseq,),
        in_specs=[
            pl.BlockSpec(memory_space=pltpu.SMEM),
            pl.BlockSpec((seq_len, hq_w), lambda b: (b, 0)),
            pl.BlockSpec((seq_len, kv_w), lambda b: (b, hq_w // kv_w)),
            pl.BlockSpec((seq_len, kv_w), lambda b: (b, hq_w // kv_w + 1)),
        ],
        out_specs=pl.BlockSpec((seq_len, hq_w), lambda b: (b, 0)),
        out_shape=jax.ShapeDtypeStruct((nseq * seq_len, hq_w), F32),
        compiler_params=_params(("parallel",)),
        name="attention_context",
    )(sink, z, z, z)


def _rope_axial(x, cos, sin):
    w = x.shape[1]
    lane = lax.broadcasted_iota(jnp.int32, (x.shape[0], LANES), 1)
    first = (lane % 32) < 16
    outs = []
    for cg in range(w // LANES):
        xg = x[:, cg * LANES:(cg + 1) * LANES]
        up = pltpu.roll(xg, LANES - 16, 1)
        dn = pltpu.roll(xg, 16, 1)
        outs.append(xg * cos + jnp.where(first, up, dn) * sin)
    return jnp.concatenate(outs, axis=1) if len(outs) > 1 else outs[0]


def _attn_lat_kernel(sink_ref, q_ref, kp_ref, kc_ref, kn_ref, vp_ref, vc_ref, vn_ref, ck_ref, cv_ref,
                     cq_ref, sq_ref, cp_ref, sp_ref, cn_ref, sn_ref, o_ref, *, nb):
    scale = ATT_DH ** -0.5
    qb = pl.program_id(1)
    q = _rope_axial(q_ref[...], cq_ref[...], sq_ref[...]) * scale
    kw = jnp.concatenate([
        _rope_axial(kp_ref[...], cp_ref[...], sp_ref[...]),
        _rope_axial(kc_ref[...], cq_ref[...], sq_ref[...]),
        _rope_axial(kn_ref[...], cn_ref[...], sn_ref[...]),
    ], axis=0)
    vw = jnp.concatenate([vp_ref[...], vc_ref[...], vn_ref[...]], axis=0)
    ck = ck_ref[...]
    cv = cv_ref[...]

    rows = ATT_G * ATT_BLK
    i = lax.broadcasted_iota(jnp.int32, (rows, 3 * ATT_BLK), 0) % ATT_BLK
    jj = lax.broadcasted_iota(jnp.int32, (rows, 3 * ATT_BLK), 1)
    lo = jnp.where(qb > 0, 0, ATT_BLK)
    hi = jnp.where(qb < nb - 1, 3 * ATT_BLK, 2 * ATT_BLK)
    valid = (jj >= jnp.maximum(i, lo)) & (jj < jnp.minimum(i + 2 * ATT_BLK + 1, hi))
    gidx = lax.broadcasted_iota(jnp.int32, (rows, 1), 0) // ATT_BLK

    for h in range(ATT_HKV):
        hs = slice(h * ATT_DH, (h + 1) * ATT_DH)
        kh = kw[:, hs].astype(BF16)
        vh = vw[:, hs].astype(BF16)
        ckh = ck[:, hs].astype(BF16)
        cvh = cv[:, hs].astype(BF16)
        heads = [h * ATT_G + g for g in range(ATT_G)]
        qg = jnp.concatenate([q[:, hq * ATT_DH:(hq + 1) * ATT_DH] for hq in heads], axis=0).astype(BF16)
        sink = jnp.zeros((rows, 1), F32)
        for g, hq in enumerate(heads):
            sink = jnp.where(gidx == g, sink_ref[hq], sink)
        s_loc = jnp.where(valid, _dot_nt(qg, kh), NEG_INF)
        s_ctx = _dot_nt(qg, ckh)
        o = _softmax_pv([(s_loc, vh), (s_ctx, cvh)], sink)
        for g, hq in enumerate(heads):
            o_ref[:, hq * ATT_DH:(hq + 1) * ATT_DH] = o[g * ATT_BLK:(g + 1) * ATT_BLK, :]


def _attn_latent(z, sink, cache_k, cache_v, rope, row0, nseq, seq_len, j):
    nb = seq_len // ATT_BLK
    base = row0 // ATT_BLK
    hq_w = ATT_HQ * ATT_DH
    kv_w = ATT_HKV * ATT_DH
    kcol = hq_w // kv_w
    past = cache_k.shape[2]
    cos, sin = rope

    def prv(q):
        return jnp.maximum(q - 1, 0)

    def nxt(q):
        return jnp.minimum(q + 1, nb - 1)

    def kv_spec(sel, col):
        return pl.BlockSpec((ATT_BLK, kv_w), lambda b, q: (base + b * nb + sel(q), col))

    def rope_spec(sel):
        return pl.BlockSpec((ATT_BLK, LANES), lambda b, q: (sel(q), 0))

    same = lambda q: q
    cache_spec = pl.BlockSpec((None, None, past, kv_w), lambda b, q: (b, j, 0, 0))
    return pl.pallas_call(
        functools.partial(_attn_lat_kernel, nb=nb),
        grid=(nseq, nb),
        in_specs=[
            pl.BlockSpec(memory_space=pltpu.SMEM),
            pl.BlockSpec((ATT_BLK, hq_w), lambda b, q: (base + b * nb + q, 0)),
            kv_spec(prv, kcol), kv_spec(same, kcol), kv_spec(nxt, kcol),
            kv_spec(prv, kcol + 1), kv_spec(same, kcol + 1), kv_spec(nxt, kcol + 1),
            cache_spec, cache_spec,
            rope_spec(same), rope_spec(same), rope_spec(prv), rope_spec(prv), rope_spec(nxt), rope_spec(nxt),
        ],
        out_specs=pl.BlockSpec((ATT_BLK, hq_w), lambda b, q: (b * nb + q, 0)),
        out_shape=jax.ShapeDtypeStruct((nseq * seq_len, hq_w), F32),
        compiler_params=_params(("parallel", "parallel")),
        name="attention_latent",
    )(sink, z, z, z, z, z, z, z, cache_k, cache_v, cos, sin, cos, sin, cos, sin)


def _attn_out_kernel(op_ref, os_ref, x_ref, m_ref, w_ref, lg_ref, lb_ref, out_ref, *, n_ctx_tiles):
    is_ctx = pl.program_id(0) < n_ctx_tiles
    o = jnp.where(is_ctx, op_ref[...], os_ref[...])
    y = _dot(o.astype(BF16), w_ref[...])
    gate = m_ref[2:3, :]
    r = DEEPNORM_ALPHA * x_ref[...] + gate * y
    out_ref[...] = _layer_norm(r, lg_ref[...], lb_ref[...])


def _attn_out(o_ctx, o_lat, x, mod, w_bf, ln_g, ln_b, layer, grp):
    nt, d = x.shape
    tm = OUT_TM
    k = o_ctx.shape[1]
    n_ctx = o_ctx.shape[0] // tm
    return pl.pallas_call(
        functools.partial(_attn_out_kernel, n_ctx_tiles=n_ctx),
        grid=(nt // tm,),
        in_specs=[
            pl.BlockSpec((tm, k), lambda b: (jnp.minimum(b, n_ctx - 1), 0)),
            pl.BlockSpec((tm, k), lambda b: (jnp.maximum(b - n_ctx, 0), 0)),
            pl.BlockSpec((tm, d), lambda b: (b, 0)),
            pl.BlockSpec((None, None, 6, d), lambda b: (layer, grp(b, tm), 0, 0)),
            pl.BlockSpec((k, d), lambda b: (0, 0)),
            pl.BlockSpec((None, 1, d), lambda b: (2 * layer, 0, 0)),
            pl.BlockSpec((None, 1, d), lambda b: (2 * layer, 0, 0)),
        ],
        out_specs=pl.BlockSpec((tm, d), lambda b: (b, 0)),
        out_shape=jax.ShapeDtypeStruct((nt, d), F32),
        compiler_params=_params(("parallel",)),
        name="attention_out",
    )(o_ctx, o_lat, x, mod, w_bf, ln_g, ln_b)


def _sort_network(n):
    pairs = []

    def merge(lo, hi, r):
        step = r * 2
        if step < hi - lo:
            merge(lo, hi, step)
            merge(lo + r, hi, step)
            for i in range(lo + r, hi - r, step):
                pairs.append((i, i + r))
        else:
            pairs.append((lo, lo + r))

    def sort(lo, hi):
        if hi - lo >= 1:
            mid = lo + (hi - lo) // 2
            sort(lo, mid)
            sort(mid + 1, hi)
            merge(lo, hi, 1)

    sort(0, n - 1)
    return pairs


def _pop_top(lists, k):
    vals = []
    for r in range(k):
        head = lists[0]
        m = jnp.max(head, axis=0, keepdims=True)
        vals.append(m)
        left = k - 1 - r
        if left > 0:
            took = head >= m
            nxt = [lists[i + 1] if i + 1 < len(lists) else NEG_INF for i in range(left)]
            lists = [jnp.where(took, nxt[i], lists[i]) for i in range(left)]
    return vals


def _top_values(s, k):
    tiles = [s[8 * j:8 * (j + 1), :] for j in range(s.shape[0] // 8)]
    for i, j in _sort_network(len(tiles)):
        tiles[i], tiles[j] = jnp.maximum(tiles[i], tiles[j]), jnp.minimum(tiles[i], tiles[j])
    return _pop_top(tiles, k)


def _peer_kernel(x_ref, m_ref, wq_ref, keys_ref, u0_ref, u_ref, vt_ref, lg_ref, lb_ref, out_ref,
                 h_scr, q_scr, d1_scr, e1_scr, s2_scr, e2_scr, at0_scr, at1_scr, g_scr, acc_scr,
                 *, n_eblk):
    t = x_ref.shape[0]
    nlt = t // LANES
    e = pl.program_id(1)
    k = PEER_TOPK
    nk = PEER_NKEYS
    hk = nk // PEER_NSUB

    @pl.when(e == 0)
    def _scores():
        sh = m_ref[3:4, :]
        sc = m_ref[4:5, :]
        h_scr[...] = (x_ref[...] * (1.0 + sc) + sh).astype(BF16)
        q = _dot(h_scr[...], wq_ref[...])
        for ph in range(2 * PEER_HEADS):
            q_scr[ph] = q[:, ph * PEER_DHALF:(ph + 1) * PEER_DHALF].astype(BF16)
        acc_scr[...] = jnp.zeros(acc_scr.shape, F32)
        at0_scr[...] = _dot_nt(u0_ref[...], h_scr[...])

        def unit(uidx, carry):
            lt = uidx // PEER_HEADS
            p = uidx % PEER_HEADS
            t0 = pl.multiple_of(lt * LANES, LANES)
            s, vals = [], []
            for hh in range(2):
                sk = _dot_nt(keys_ref[p, hh], q_scr[2 * p + hh, pl.ds(t0, LANES), :])
                s.append(sk)
                vals.append(_top_values(sk, k))
            v1m = [v - vals[0][0] for v in vals[0]]
            v2m = [v - vals[1][0] for v in vals[1]]
            row = lax.broadcasted_iota(jnp.int32, (8, LANES), 0)
            base = jnp.zeros((8, LANES), F32)
            for r in range(4):
                base = jnp.where(row == r, v1m[r], base)
                base = jnp.where(row == 4 + r, v2m[r], base)
            cands = [base + jnp.where(row < 4, v2m[i], v1m[i + 4] if i + 4 < k else NEG_INF) for i in range(k)]
            tops = _pop_top(cands, k + 1)
            thr = 0.5 * (tops[k - 1] + tops[k])
            zsum = jnp.exp(tops[0])
            for r in range(1, k):
                zsum = zsum + jnp.exp(tops[r])
            s1m = jnp.where(s[0] >= vals[0][k - 1], s[0] - vals[0][0], NEG_INF)
            s2m = jnp.where(s[1] >= vals[1][k - 1], s[1] - vals[1][0], NEG_INF)
            d1 = thr - s1m
            e1 = jnp.exp(s1m) * (0.5 / zsum)
            for j in range(nk // PEER_NI1):
                d1_scr[j, p, lt] = d1[j * PEER_NI1:(j + 1) * PEER_NI1, :]
                e1_scr[j, p, lt] = e1[j * PEER_NI1:(j + 1) * PEER_NI1, :]
            s2_scr[p, lt] = s2m
            e2_scr[p, lt] = jnp.exp(s2m)
            return carry

        lax.fori_loop(0, nlt * PEER_HEADS, unit, 0, unroll=8)

    def step(at_r, at_w):
        mxu_w = 2 * LANES
        il_grp = PEER_ILG
        for ts_i in range(t // mxu_w):
            ts = slice(ts_i * mxu_w, (ts_i + 1) * mxu_w)
            at_w[:, ts] = _dot_nt(u_ref[...], h_scr[ts, :])
            for ig in range(PEER_NI1 // il_grp):
                for lt in range(ts_i * mxu_w // LANES, (ts_i + 1) * mxu_w // LANES):
                    tl = slice(lt * LANES, (lt + 1) * LANES)
                    for hf in range(PEER_NSUB):
                        rows2 = slice(hf * hk, (hf + 1) * hk)
                        ils = range(ig * il_grp, (ig + 1) * il_grp)
                        w = {il: jnp.zeros((hk, LANES), F32) for il in ils}
                        for p in range(PEER_HEADS):
                            s2h = s2_scr[p, lt, rows2, :]
                            e2h = e2_scr[p, lt, rows2, :]
                            for il in ils:
                                d1 = d1_scr[e, p, lt, il:il + 1, :]
                                e1 = e1_scr[e, p, lt, il:il + 1, :]
                                w[il] = w[il] + jnp.where(s2h >= d1, e2h, 0.0) * e1
                        for il in ils:
                            rows = slice(il * nk + hf * hk, il * nk + (hf + 1) * hk)
                            a = at_r[rows, tl]
                            u = a * (a * a * (GELU_C * 0.044715) + GELU_C)
                            g_scr[rows, tl] = (w[il] * (a + a * jnp.tanh(u))).astype(BF16)
                ks = slice(ig * il_grp * nk, (ig + 1) * il_grp * nk)
                acc_scr[:, ts] += _dot(vt_ref[:, ks], g_scr[ks, ts])

    @pl.when(e % 2 == 0)
    def _even():
        step(at0_scr, at1_scr)

    @pl.when(e % 2 == 1)
    def _odd():
        step(at1_scr, at0_scr)

    @pl.when(e == n_eblk - 1)
    def _finish():
        y = acc_scr[...].T
        gate = m_ref[5:6, :]
        r = DEEPNORM_ALPHA * x_ref[...] + gate * y
        out_ref[...] = _layer_norm(r, lg_ref[...], lb_ref[...])


def _peer(x, mod, wq_bf, keys_bf, u_bf, vt_bf, ln_g, ln_b, layer, grp):
    nt, d = x.shape
    t = PEER_T
    nlt = t // LANES
    eblk = PEER_NI1 * PEER_NKEYS
    n_eblk = u_bf.shape[0] // eblk
    nq = wq_bf.shape[1]
    sel_shape = (PEER_HEADS, nlt, PEER_NKEYS, LANES)
    assert PEER_NI1 % 8 == 0
    blk_shape = (n_eblk, PEER_HEADS, nlt, PEER_NI1, LANES)
    return pl.pallas_call(
        functools.partial(_peer_kernel, n_eblk=n_eblk),
        grid=(nt // t, n_eblk),
        in_specs=[
            pl.BlockSpec((t, d), lambda b, e: (b, 0)),
            pl.BlockSpec((None, None, 6, d), lambda b, e: (layer, grp(b, t), 0, 0)),
            pl.BlockSpec((d, nq), lambda b, e: (0, 0)),
            pl.BlockSpec((PEER_HEADS, 2, PEER_NKEYS, PEER_DHALF), lambda b, e: (0, 0, 0, 0)),
            pl.BlockSpec((eblk, d), lambda b, e: (0, 0)),
            pl.BlockSpec((eblk, d), lambda b, e: (jnp.minimum(e + 1, n_eblk - 1), 0)),
            pl.BlockSpec((d, eblk), lambda b, e: (0, e)),
            pl.BlockSpec((None, 1, d), lambda b, e: (2 * layer + 1, 0, 0)),
            pl.BlockSpec((None, 1, d), lambda b, e: (2 * layer + 1, 0, 0)),
        ],
        out_specs=pl.BlockSpec((t, d), lambda b, e: (b, 0)),
        out_shape=jax.ShapeDtypeStruct((nt, d), F32),
        scratch_shapes=[
            pltpu.VMEM((t, d), BF16),
            pltpu.VMEM((2 * PEER_HEADS, t, PEER_DHALF), BF16),
            pltpu.VMEM(blk_shape, F32), pltpu.VMEM(blk_shape, F32),
            pltpu.VMEM(sel_shape, F32), pltpu.VMEM(sel_shape, F32),
            pltpu.VMEM((eblk, t), F32), pltpu.VMEM((eblk, t), F32),
            pltpu.VMEM((eblk, t), BF16),
            pltpu.VMEM((d, t), F32),
        ],
        compiler_params=_params(("parallel", "arbitrary")),
        name="peer",
    )(x, mod, wq_bf, keys_bf, u_bf, u_bf, vt_bf, ln_g, ln_b)


def _rope_angles(pos, dim):
    inv = ROPE_BASE ** (-jnp.arange(0, dim, 2, dtype=F32) / dim)
    ang = pos.astype(F32)[:, None] * inv[None, :]
    return jnp.cos(ang), jnp.sin(ang)


def _axial_tables(seq_len):
    t = jnp.arange(seq_len)
    half = ATT_DH // 2
    cr, sr = _rope_angles(t // GRID_W, half)
    cc, sc = _rope_angles(t % GRID_W, half)
    cos = jnp.concatenate([cr, cr, cc, cc], -1)
    sin = jnp.concatenate([-sr, sr, -sc, sc], -1)
    reps = LANES // ATT_DH
    return jnp.tile(cos, (1, reps)), jnp.tile(sin, (1, reps))


def kernel(x_prompt, x_sample, state_ret_fwd, state_ret_bwd, cache_k, cache_v, c, c_ctx, mod_w, mod_b, ln_g, ln_b,
           ret_w_in, ret_w_out, ret_decay, attn_w_in, attn_w_out, attn_sink, peer_wq, peer_keys, peer_u, peer_v):
    nb_p, len_p, d = x_prompt.shape
    nb_s, len_s, _ = x_sample.shape
    n_p = nb_p * len_p
    n_s = nb_s * len_s
    for tile in (PROJ_TM, OUT_TM, PEER_T):
        assert n_p % tile == 0 and len_s % tile == 0

    def grp(blk, tile):
        return _group_index(blk, n_p // tile, len_s // tile)

    x = jnp.concatenate([x_prompt.reshape(n_p, d), x_sample.reshape(n_s, d)], axis=0)
    n_grp = 1 + nb_s
    n_grp_pad = -(-n_grp // 8) * 8
    cond = jnp.concatenate([c_ctx[None, :], c, jnp.zeros((n_grp_pad - n_grp, d), F32)], axis=0)
    mod = _modulation(cond, mod_w, mod_b).reshape(DEPTH, n_grp_pad, 6, d)

    ln_g3 = ln_g.reshape(DEPTH * 2, 1, d)
    ln_b3 = ln_b.reshape(DEPTH * 2, 1, d)
    ret_rope = _rope_angles(jnp.arange(len_s), RET_DK)
    att_rope = _axial_tables(len_s)
    kv_w = ATT_HKV * ATT_DH
    cache_k4 = cache_k.reshape(cache_k.shape[0], cache_k.shape[1], cache_k.shape[2], kv_w)
    cache_v4 = cache_v.reshape(cache_v.shape[0], cache_v.shape[1], cache_v.shape[2], kv_w)

    new_sf, new_sb, new_k, new_v = [], [], [], []
    for i in range(DEPTH):
        j = i // N_MIXERS
        if i % N_MIXERS == 0:
            z = _project(x, mod, ret_w_in[j].astype(BF16), i, 0, grp)
            tables = _ret_tables(ret_decay[j])
            o_p, sf, sb = _retention(z, tables, 0, nb_p, len_p, False)
            o_s = _retention(z, tables, n_p, nb_s, len_s, True, rope=ret_rope,
                             states=(state_ret_fwd, state_ret_bwd), j=j)
            new_sf.append(sf)
            new_sb.append(sb)
            x = _ret_out(o_p, o_s, z, x, mod, ret_w_out[j].astype(BF16), ln_g3, ln_b3, i, grp)
        else:
            z = _project(x, mod, attn_w_in[j].astype(BF16), i, 0, grp)
            sink = attn_sink[j].astype(F32)
            o_p = _attn_context(z, sink, nb_p, len_p)
            o_s = _attn_latent(z, sink, cache_k4, cache_v4, att_rope, n_p, nb_s, len_s, j)
            hq_w = ATT_HQ * ATT_DH
            new_k.append(z[:n_p, hq_w:hq_w + kv_w].reshape(nb_p, len_p, ATT_HKV, ATT_DH))
            new_v.append(z[:n_p, hq_w + kv_w:hq_w + 2 * kv_w].reshape(nb_p, len_p, ATT_HKV, ATT_DH))
            x = _attn_out(o_p, o_s, x, mod, attn_w_out[j].astype(BF16), ln_g3, ln_b3, i, grp)
        x = _peer(x, mod, peer_wq[i].astype(BF16), peer_keys[i].astype(BF16), peer_u[i].astype(BF16),
                  peer_v[i].T.astype(BF16), ln_g3, ln_b3, i, grp)

    y_p = x[:n_p].reshape(nb_p, len_p, d)
    y_s = x[n_p:].reshape(nb_s, len_s, d)
    return (y_p, y_s, jnp.stack(new_sf, 1), jnp.stack(new_sb, 1), jnp.stack(new_k, 1), jnp.stack(new_v, 1))
```

```python
import functools
import math

import jax
import jax.numpy as jnp
from jax import lax
from jax.experimental import pallas as pl
from jax.experimental.pallas import tpu as pltpu

F32 = jnp.float32
BF16 = jnp.bfloat16

DEPTH = 4
N_MIXERS = 2
GRID_W = 64
RET_HEADS = 4
RET_DK = 256
RET_DV = 512
RET_CHUNK = 128
RET_SUB = 2
ATT_HQ = 16
ATT_HKV = 4
ATT_G = ATT_HQ // ATT_HKV
ATT_DH = 64
ATT_BLK = 128
ROPE_BASE = 10000.0
PEER_HEADS = 8
PEER_NKEYS = 128
PEER_DHALF = 128
PEER_TOPK = 16
DEEPNORM_ALPHA = (2.0 * DEPTH) ** 0.25
LN_EPS = 1e-5
GN_EPS = 1e-5
NEG_INF = -1e30
GELU_C = math.sqrt(2.0 / math.pi)

LANES = 128
VMEM_LIMIT = 56 * 1024 * 1024
PROJ_TM = 512
OUT_TM = 256
PEER_T = 512
PEER_NI1 = 8
PEER_ILG = (4, 2, 2)
PEER_NSUB = 2


def _dot(a, b):
    return jnp.dot(a, b, preferred_element_type=F32)


def _dot_nt(a, b):
    return lax.dot_general(a, b, (((1,), (1,)), ((), ())), preferred_element_type=F32)


def _params(sem, flags=None):
    return pltpu.CompilerParams(dimension_semantics=sem, vmem_limit_bytes=VMEM_LIMIT, flags=flags)


def _group_index(blk, n_prompt_blocks, blocks_per_seq):
    return jnp.where(blk < n_prompt_blocks, 0, 1 + (blk - n_prompt_blocks) // blocks_per_seq)


def _layer_norm(r, g, b):
    mu = jnp.mean(r, -1, keepdims=True)
    d = r - mu
    var = jnp.mean(d * d, -1, keepdims=True)
    return d * lax.rsqrt(var + LN_EPS) * g + b


def _mod_kernel(c_ref, w_ref, b_ref, o_ref):
    c = c_ref[...]
    a = (c * jax.nn.sigmoid(c)).astype(BF16)
    o_ref[...] = _dot(a, w_ref[...].astype(BF16)) + b_ref[...]


def _modulation(cond, mod_w, mod_b):
    ngp, d = cond.shape
    n_out = mod_w.shape[-1]
    tn = 1536
    return pl.pallas_call(
        _mod_kernel,
        grid=(DEPTH, n_out // tn),
        in_specs=[
            pl.BlockSpec((ngp, d), lambda i, n: (0, 0)),
            pl.BlockSpec((None, d, tn), lambda i, n: (i, 0, n)),
            pl.BlockSpec((None, 1, tn), lambda i, n: (i, 0, n)),
        ],
        out_specs=pl.BlockSpec((None, ngp, tn), lambda i, n: (i, 0, n)),
        out_shape=jax.ShapeDtypeStruct((DEPTH, ngp, n_out), F32),
        compiler_params=_params(("parallel", "parallel")),
        name="modulation",
    )(cond, mod_w, mod_b.reshape(DEPTH, 1, n_out))


def _proj_kernel(x_ref, m_ref, w_ref, o_ref, *, off):
    sh = m_ref[off:off + 1, :]
    sc = m_ref[off + 1:off + 2, :]
    h = (x_ref[...] * (1.0 + sc) + sh).astype(BF16)
    o_ref[...] = _dot(h, w_ref[...])


def _project(x, mod, w_bf, layer, off, grp):
    nt, d = x.shape
    n = w_bf.shape[1]
    tn = 1536
    tm = PROJ_TM
    return pl.pallas_call(
        functools.partial(_proj_kernel, off=off),
        grid=(n // tn, nt // tm),
        in_specs=[
            pl.BlockSpec((tm, d), lambda j, b: (b, 0)),
            pl.BlockSpec((None, None, 6, d), lambda j, b: (layer, grp(b, tm), 0, 0)),
            pl.BlockSpec((d, tn), lambda j, b: (0, j)),
        ],
        out_specs=pl.BlockSpec((tm, tn), lambda j, b: (b, j)),
        out_shape=jax.ShapeDtypeStruct((nt, n), F32),
        compiler_params=_params(("parallel", "parallel")),
        name="mod_proj",
    )(x, mod, w_bf)


def _rope_half(x, cos, sin):
    x1 = x[:, :LANES]
    x2 = x[:, LANES:]
    return jnp.concatenate([x1 * cos - x2 * sin, x1 * sin + x2 * cos], axis=1)


def _ret_kernel(*refs, latent, nc):
    it = iter(refs)
    qa, ka, va, qb, kb, vb = (next(it) for _ in range(6))
    if latent:
        cosa, sina, cosb, sinb = (next(it) for _ in range(4))
    dfb, qdf, qdb, kdf, kdb, cdf, cdb = (next(it) for _ in range(7))
    if latent:
        s0f, s0b = next(it), next(it)
    o_ref = next(it)
    if not latent:
        sf_out, sb_out = next(it), next(it)
    sf, sb = next(it), next(it)

    c = pl.program_id(2)

    @pl.when(c == 0)
    def _init():
        o_ref[...] = jnp.zeros(o_ref.shape, F32)
        if latent:
            sf[...] = s0f[...]
            sb[...] = s0b[...]
        else:
            sf[...] = jnp.zeros(sf.shape, F32)
            sb[...] = jnp.zeros(sb.shape, F32)

    def prep(q, k, cos_ref, sin_ref, rows):
        k = k * (RET_DK ** -0.5)
        if latent:
            cos = cos_ref[rows, :]
            sin = sin_ref[rows, :]
            q = _rope_half(q, cos, sin)
            k = _rope_half(k, cos, sin)
        return q, k

    for sub in range(RET_SUB):
        ca = c * RET_SUB + sub
        ra = slice(sub * RET_CHUNK, (sub + 1) * RET_CHUNK)
        rb = slice((RET_SUB - 1 - sub) * RET_CHUNK, (RET_SUB - sub) * RET_CHUNK)

        q, k = prep(qa[ra, :], ka[ra, :], cosa if latent else None, sina if latent else None, ra)
        v = va[ra, :].astype(BF16)
        att = (_dot_nt(q.astype(BF16), k.astype(BF16)) * dfb[...]).astype(BF16)
        o_c = _dot(att, v) + _dot((q * qdf[...]).astype(BF16), sf[...].astype(BF16))
        rows_c = pl.ds(pl.multiple_of(ca * RET_CHUNK, RET_CHUNK), RET_CHUNK)
        o_ref[rows_c, :] += o_c
        kd_t = (k * kdf[...]).T.astype(BF16)
        sf[...] = sf[...] * cdf[...] + _dot(kd_t, v)

        q2, k2 = prep(qb[rb, :], kb[rb, :], cosb if latent else None, sinb if latent else None, rb)
        v2 = vb[rb, :].astype(BF16)
        o_b = _dot((q2 * qdb[...]).astype(BF16), sb[...].astype(BF16))
        rows_b = pl.ds(pl.multiple_of((nc - 1 - ca) * RET_CHUNK, RET_CHUNK), RET_CHUNK)
        o_ref[rows_b, :] += o_b
        kd2_t = (k2 * kdb[...]).T.astype(BF16)
        sb[...] = sb[...] * cdb[...] + _dot(kd2_t, v2)

    if not latent:
        @pl.when(c == nc // RET_SUB - 1)
        def _fin():
            sf_out[...] = sf[...]
            sb_out[...] = sb[...]


def _ret_tables(decay):
    log_g = jax.nn.log_sigmoid(decay.astype(F32))
    lf, lb = log_g[0], log_g[1]
    idx = jnp.arange(RET_CHUNK, dtype=F32)
    diff = idx[:, None] - idx[None, :]
    fmask = diff >= 0
    bmask = diff < 0
    df = jnp.where(fmask[None], jnp.exp(jnp.where(fmask, diff, 0.0)[None] * lf[:, None, None]), 0.0)
    db = jnp.where(bmask[None], jnp.exp(jnp.where(bmask, -diff, 0.0)[None] * lb[:, None, None]), 0.0)
    dfb = df + db
    def rows(e):
        return jnp.broadcast_to(e[:, :, None], (RET_HEADS, RET_CHUNK, RET_DK))
    qdf = rows(jnp.exp((idx + 1.0)[None, :] * lf[:, None]))
    kdf = rows(jnp.exp((RET_CHUNK - 1.0 - idx)[None, :] * lf[:, None]))
    qdb = rows(jnp.exp((RET_CHUNK - idx)[None, :] * lb[:, None]))
    kdb = rows(jnp.exp(idx[None, :] * lb[:, None]))
    cdf = jnp.broadcast_to(jnp.exp(RET_CHUNK * lf)[:, None, None], (RET_HEADS, 1, RET_DV))
    cdb = jnp.broadcast_to(jnp.exp(RET_CHUNK * lb)[:, None, None], (RET_HEADS, 1, RET_DV))
    return dfb, qdf, qdb, kdf, kdb, cdf, cdb


def _retention(z, tables, row0, nseq, seq_len, latent, rope=None, states=None, j=0):
    nc = seq_len // RET_CHUNK
    blk = RET_SUB * RET_CHUNK
    ns = nc // RET_SUB
    base = row0 // blk
    h_ = RET_HEADS
    kcol = (h_ * RET_DK) // RET_DK
    vcol = (2 * h_ * RET_DK) // RET_DV

    def fw(b, h, c):
        return base + b * ns + c

    def bw(b, h, c):
        return base + b * ns + (ns - 1 - c)

    in_specs = [
        pl.BlockSpec((blk, RET_DK), lambda b, h, c: (fw(b, h, c), h)),
        pl.BlockSpec((blk, RET_DK), lambda b, h, c: (fw(b, h, c), kcol + h)),
        pl.BlockSpec((blk, RET_DV), lambda b, h, c: (fw(b, h, c), vcol + h)),
        pl.BlockSpec((blk, RET_DK), lambda b, h, c: (bw(b, h, c), h)),
        pl.BlockSpec((blk, RET_DK), lambda b, h, c: (bw(b, h, c), kcol + h)),
        pl.BlockSpec((blk, RET_DV), lambda b, h, c: (bw(b, h, c), vcol + h)),
    ]
    args = [z, z, z, z, z, z]
    if latent:
        cos, sin = rope
        in_specs += [
            pl.BlockSpec((blk, LANES), lambda b, h, c: (c, 0)),
            pl.BlockSpec((blk, LANES), lambda b, h, c: (c, 0)),
            pl.BlockSpec((blk, LANES), lambda b, h, c: (ns - 1 - c, 0)),
            pl.BlockSpec((blk, LANES), lambda b, h, c: (ns - 1 - c, 0)),
        ]
        args += [cos, sin, cos, sin]
    dfb, qdf, qdb, kdf, kdb, cdf, cdb = tables
    in_specs += [pl.BlockSpec((None, RET_CHUNK, RET_CHUNK), lambda b, h, c: (h, 0, 0))]
    in_specs += [pl.BlockSpec((None, RET_CHUNK, RET_DK), lambda b, h, c: (h, 0, 0))] * 4
    in_specs += [pl.BlockSpec((None, 1, RET_DV), lambda b, h, c: (h, 0, 0))] * 2
    args += [dfb, qdf, qdb, kdf, kdb, cdf, cdb]
    if latent:
        s0f, s0b = states
        st_spec = pl.BlockSpec((None, None, None, RET_DK, RET_DV), lambda b, h, c: (b, j, h, 0, 0))
        in_specs += [st_spec, st_spec]
        args += [s0f, s0b]

    o_shape = jax.ShapeDtypeStruct((nseq * seq_len, h_ * RET_DV), F32)
    o_spec = pl.BlockSpec((seq_len, RET_DV), lambda b, h, c: (b, h))
    if latent:
        out_shape, out_specs = o_shape, o_spec
    else:
        s_shape = jax.ShapeDtypeStruct((nseq, h_, RET_DK, RET_DV), F32)
        s_spec = pl.BlockSpec((None, None, RET_DK, RET_DV), lambda b, h, c: (b, h, 0, 0))
        out_shape, out_specs = (o_shape, s_shape, s_shape), (o_spec, s_spec, s_spec)

    return pl.pallas_call(
        functools.partial(_ret_kernel, latent=latent, nc=nc),
        grid=(nseq, h_, ns),
        in_specs=in_specs,
        out_specs=out_specs,
        out_shape=out_shape,
        scratch_shapes=[pltpu.VMEM((RET_DK, RET_DV), F32), pltpu.VMEM((RET_DK, RET_DV), F32)],
        compiler_params=_params(("parallel", "parallel", "arbitrary")),
        name="retention_latent" if latent else "retention_context",
    )(*args)


def _ret_out_kernel(op_ref, os_ref, g_ref, x_ref, m_ref, w_ref, lg_ref, lb_ref, out_ref, *, n_ctx_tiles):
    is_ctx = pl.program_id(0) < n_ctx_tiles
    y = None
    for h in range(RET_HEADS):
        cols = slice(h * RET_DV, (h + 1) * RET_DV)
        oh = jnp.where(is_ctx, op_ref[:, cols], os_ref[:, cols])
        mu = jnp.mean(oh, -1, keepdims=True)
        d = oh - mu
        var = jnp.mean(d * d, -1, keepdims=True)
        on = d * lax.rsqrt(var + GN_EPS)
        gh = g_ref[:, cols]
        act = (gh * jax.nn.sigmoid(gh) * on).astype(BF16)
        part = _dot(act, w_ref[cols, :])
        y = part if y is None else y + part
    gate = m_ref[2:3, :]
    r = DEEPNORM_ALPHA * x_ref[...] + gate * y
    out_ref[...] = _layer_norm(r, lg_ref[...], lb_ref[...])


def _ret_out(o_ctx, o_lat, z, x, mod, w_bf, ln_g, ln_b, layer, grp):
    nt, d = x.shape
    tm = OUT_TM
    hv = RET_HEADS * RET_DV
    gcol = (2 * RET_HEADS * RET_DK + hv) // hv
    n_ctx = o_ctx.shape[0] // tm
    return pl.pallas_call(
        functools.partial(_ret_out_kernel, n_ctx_tiles=n_ctx),
        grid=(nt // tm,),
        in_specs=[
            pl.BlockSpec((tm, hv), lambda b: (jnp.minimum(b, n_ctx - 1), 0)),
            pl.BlockSpec((tm, hv), lambda b: (jnp.maximum(b - n_ctx, 0), 0)),
            pl.BlockSpec((tm, hv), lambda b: (b, gcol)),
            pl.BlockSpec((tm, d), lambda b: (b, 0)),
            pl.BlockSpec((None, None, 6, d), lambda b: (layer, grp(b, tm), 0, 0)),
            pl.BlockSpec((hv, d), lambda b: (0, 0)),
            pl.BlockSpec((None, 1, d), lambda b: (2 * layer, 0, 0)),
            pl.BlockSpec((None, 1, d), lambda b: (2 * layer, 0, 0)),
        ],
        out_specs=pl.BlockSpec((tm, d), lambda b: (b, 0)),
        out_shape=jax.ShapeDtypeStruct((nt, d), F32),
        compiler_params=_params(("parallel",)),
        name="retention_out",
    )(o_ctx, o_lat, z, x, mod, w_bf, ln_g, ln_b)


def _softmax_pv(parts, sink):
    m = sink
    for s, _ in parts:
        m = jnp.maximum(m, jnp.max(s, -1, keepdims=True))
    den = jnp.exp(sink - m)
    o = None
    for s, v in parts:
        p = jnp.exp(s - m)
        den = den + jnp.sum(p, -1, keepdims=True)
        pv = _dot(p.astype(BF16), v)
        o = pv if o is None else o + pv
    return o / den


def _attn_ctx_kernel(sink_ref, q_ref, k_ref, v_ref, o_ref):
    scale = ATT_DH ** -0.5
    for h in range(ATT_HKV):
        kh = k_ref[:, h * ATT_DH:(h + 1) * ATT_DH].astype(BF16)
        vh = v_ref[:, h * ATT_DH:(h + 1) * ATT_DH].astype(BF16)
        for g in range(ATT_G):
            hq = h * ATT_G + g
            cols = slice(hq * ATT_DH, (hq + 1) * ATT_DH)
            qh = (q_ref[:, cols] * scale).astype(BF16)
            o_ref[:, cols] = _softmax_pv([(_dot_nt(qh, kh), vh)], sink_ref[hq])


def _attn_context(z, sink, nseq, seq_len):
    hq_w = ATT_HQ * ATT_DH
    kv_w = ATT_HKV * ATT_DH
    return pl.pallas_call(
        _attn_ctx_kernel,
        grid=(nseq,),
        in_specs=[
            pl.BlockSpec(memory_space=pltpu.SMEM),
            pl.BlockSpec((seq_len, hq_w), lambda b: (b, 0)),
            pl.BlockSpec((seq_len, kv_w), lambda b: (b, hq_w // kv_w)),
            pl.BlockSpec((seq_len, kv_w), lambda b: (b, hq_w // kv_w + 1)),
        ],
        out_specs=pl.BlockSpec((seq_len, hq_w), lambda b: (b, 0)),
        out_shape=jax.ShapeDtypeStruct((nseq * seq_len, hq_w), F32),
        compiler_params=_params(("parallel",)),
        name="attention_context",
    )(sink, z, z, z)


def _rope_axial(x, cos, sin):
    w = x.shape[1]
    lane = lax.broadcasted_iota(jnp.int32, (x.shape[0], LANES), 1)
    first = (lane % 32) < 16
    outs = []
    for cg in range(w // LANES):
        xg = x[:, cg * LANES:(cg + 1) * LANES]
        up = pltpu.roll(xg, LANES - 16, 1)
        dn = pltpu.roll(xg, 16, 1)
        outs.append(xg * cos + jnp.where(first, up, dn) * sin)
    return jnp.concatenate(outs, axis=1) if len(outs) > 1 else outs[0]


def _attn_lat_kernel(sink_ref, q_ref, kp_ref, kc_ref, kn_ref, vp_ref, vc_ref, vn_ref, ck_ref, cv_ref,
                     cq_ref, sq_ref, cp_ref, sp_ref, cn_ref, sn_ref, o_ref, *, nb):
    scale = ATT_DH ** -0.5
    qb = pl.program_id(1)
    q = _rope_axial(q_ref[...], cq_ref[...], sq_ref[...]) * scale
    kw = jnp.concatenate([
        _rope_axial(kp_ref[...], cp_ref[...], sp_ref[...]),
        _rope_axial(kc_ref[...], cq_ref[...], sq_ref[...]),
        _rope_axial(kn_ref[...], cn_ref[...], sn_ref[...]),
    ], axis=0)
    vw = jnp.concatenate([vp_ref[...], vc_ref[...], vn_ref[...]], axis=0)
    ck = ck_ref[...]
    cv = cv_ref[...]

    rows = ATT_G * ATT_BLK
    i = lax.broadcasted_iota(jnp.int32, (rows, 3 * ATT_BLK), 0) % ATT_BLK
    jj = lax.broadcasted_iota(jnp.int32, (rows, 3 * ATT_BLK), 1)
    lo = jnp.where(qb > 0, 0, ATT_BLK)
    hi = jnp.where(qb < nb - 1, 3 * ATT_BLK, 2 * ATT_BLK)
    valid = (jj >= jnp.maximum(i, lo)) & (jj < jnp.minimum(i + 2 * ATT_BLK + 1, hi))
    gidx = lax.broadcasted_iota(jnp.int32, (rows, 1), 0) // ATT_BLK

    for h in range(ATT_HKV):
        hs = slice(h * ATT_DH, (h + 1) * ATT_DH)
        kh = kw[:, hs].astype(BF16)
        vh = vw[:, hs].astype(BF16)
        ckh = ck[:, hs].astype(BF16)
        cvh = cv[:, hs].astype(BF16)
        heads = [h * ATT_G + g for g in range(ATT_G)]
        qg = jnp.concatenate([q[:, hq * ATT_DH:(hq + 1) * ATT_DH] for hq in heads], axis=0).astype(BF16)
        sink = jnp.zeros((rows, 1), F32)
        for g, hq in enumerate(heads):
            sink = jnp.where(gidx == g, sink_ref[hq], sink)
        s_loc = jnp.where(valid, _dot_nt(qg, kh), NEG_INF)
        s_ctx = _dot_nt(qg, ckh)
        o = _softmax_pv([(s_loc, vh), (s_ctx, cvh)], sink)
        for g, hq in enumerate(heads):
            o_ref[:, hq * ATT_DH:(hq + 1) * ATT_DH] = o[g * ATT_BLK:(g + 1) * ATT_BLK, :]


def _attn_latent(z, sink, cache_k, cache_v, rope, row0, nseq, seq_len, j):
    nb = seq_len // ATT_BLK
    base = row0 // ATT_BLK
    hq_w = ATT_HQ * ATT_DH
    kv_w = ATT_HKV * ATT_DH
    kcol = hq_w // kv_w
    past = cache_k.shape[2]
    cos, sin = rope

    def prv(q):
        return jnp.maximum(q - 1, 0)

    def nxt(q):
        return jnp.minimum(q + 1, nb - 1)

    def kv_spec(sel, col):
        return pl.BlockSpec((ATT_BLK, kv_w), lambda b, q: (base + b * nb + sel(q), col))

    def rope_spec(sel):
        return pl.BlockSpec((ATT_BLK, LANES), lambda b, q: (sel(q), 0))

    same = lambda q: q
    cache_spec = pl.BlockSpec((None, None, past, kv_w), lambda b, q: (b, j, 0, 0))
    return pl.pallas_call(
        functools.partial(_attn_lat_kernel, nb=nb),
        grid=(nseq, nb),
        in_specs=[
            pl.BlockSpec(memory_space=pltpu.SMEM),
            pl.BlockSpec((ATT_BLK, hq_w), lambda b, q: (base + b * nb + q, 0)),
            kv_spec(prv, kcol), kv_spec(same, kcol), kv_spec(nxt, kcol),
            kv_spec(prv, kcol + 1), kv_spec(same, kcol + 1), kv_spec(nxt, kcol + 1),
            cache_spec, cache_spec,
            rope_spec(same), rope_spec(same), rope_spec(prv), rope_spec(prv), rope_spec(nxt), rope_spec(nxt),
        ],
        out_specs=pl.BlockSpec((ATT_BLK, hq_w), lambda b, q: (b * nb + q, 0)),
        out_shape=jax.ShapeDtypeStruct((nseq * seq_len, hq_w), F32),
        compiler_params=_params(("parallel", "parallel")),
        name="attention_latent",
    )(sink, z, z, z, z, z, z, z, cache_k, cache_v, cos, sin, cos, sin, cos, sin)


def _attn_out_kernel(op_ref, os_ref, x_ref, m_ref, w_ref, lg_ref, lb_ref, out_ref, *, n_ctx_tiles):
    is_ctx = pl.program_id(0) < n_ctx_tiles
    o = jnp.where(is_ctx, op_ref[...], os_ref[...])
    y = _dot(o.astype(BF16), w_ref[...])
    gate = m_ref[2:3, :]
    r = DEEPNORM_ALPHA * x_ref[...] + gate * y
    out_ref[...] = _layer_norm(r, lg_ref[...], lb_ref[...])


def _attn_out(o_ctx, o_lat, x, mod, w_bf, ln_g, ln_b, layer, grp):
    nt, d = x.shape
    tm = OUT_TM
    k = o_ctx.shape[1]
    n_ctx = o_ctx.shape[0] // tm
    return pl.pallas_call(
        functools.partial(_attn_out_kernel, n_ctx_tiles=n_ctx),
        grid=(nt // tm,),
        in_specs=[
            pl.BlockSpec((tm, k), lambda b: (jnp.minimum(b, n_ctx - 1), 0)),
            pl.BlockSpec((tm, k), lambda b: (jnp.maximum(b - n_ctx, 0), 0)),
            pl.BlockSpec((tm, d), lambda b: (b, 0)),
            pl.BlockSpec((None, None, 6, d), lambda b: (layer, grp(b, tm), 0, 0)),
            pl.BlockSpec((k, d), lambda b: (0, 0)),
            pl.BlockSpec((None, 1, d), lambda b: (2 * layer, 0, 0)),
            pl.BlockSpec((None, 1, d), lambda b: (2 * layer, 0, 0)),
        ],
        out_specs=pl.BlockSpec((tm, d), lambda b: (b, 0)),
        out_shape=jax.ShapeDtypeStruct((nt, d), F32),
        compiler_params=_params(("parallel",)),
        name="attention_out",
    )(o_ctx, o_lat, x, mod, w_bf, ln_g, ln_b)


def _sort_network(n):
    pairs = []

    def merge(lo, hi, r):
        step = r * 2
        if step < hi - lo:
            merge(lo, hi, step)
            merge(lo + r, hi, step)
            for i in range(lo + r, hi - r, step):
                pairs.append((i, i + r))
        else:
            pairs.append((lo, lo + r))

    def sort(lo, hi):
        if hi - lo >= 1:
            mid = lo + (hi - lo) // 2
            sort(lo, mid)
            sort(mid + 1, hi)
            merge(lo, hi, 1)

    sort(0, n - 1)
    return pairs


def _pop_top(lists, k):
    vals = []
    for r in range(k):
        head = lists[0]
        m = jnp.max(head, axis=0, keepdims=True)
        vals.append(m)
        left = k - 1 - r
        if left > 0:
            took = head >= m
            nxt = [lists[i + 1] if i + 1 < len(lists) else NEG_INF for i in range(left)]
            lists = [jnp.where(took, nxt[i], lists[i]) for i in range(left)]
    return vals


def _top_values(s, k):
    tiles = [s[8 * j:8 * (j + 1), :] for j in range(s.shape[0] // 8)]
    for i, j in _sort_network(len(tiles)):
        tiles[i], tiles[j] = jnp.maximum(tiles[i], tiles[j]), jnp.minimum(tiles[i], tiles[j])
    return _pop_top(tiles, k)


def _peer_kernel(x_ref, m_ref, wq_ref, keys_ref, u0_ref, u_ref, vt_ref, lg_ref, lb_ref, out_ref,
                 h_scr, q_scr, d1_scr, e1_scr, s2_scr, e2_scr, at0_scr, at1_scr, g_scr, acc_scr,
                 *, n_eblk):
    t = x_ref.shape[0]
    nlt = t // LANES
    e = pl.program_id(1)
    k = PEER_TOPK
    nk = PEER_NKEYS
    hk = nk // PEER_NSUB

    @pl.when(e == 0)
    def _scores():
        sh = m_ref[3:4, :]
        sc = m_ref[4:5, :]
        h_scr[...] = (x_ref[...] * (1.0 + sc) + sh).astype(BF16)
        q = _dot(h_scr[...], wq_ref[...])
        for ph in range(2 * PEER_HEADS):
            q_scr[ph] = q[:, ph * PEER_DHALF:(ph + 1) * PEER_DHALF].astype(BF16)
        acc_scr[...] = jnp.zeros(acc_scr.shape, F32)
        at0_scr[...] = _dot_nt(u0_ref[...], h_scr[...])

        def unit(uidx, carry):
            lt = uidx // PEER_HEADS
            p = uidx % PEER_HEADS
            t0 = pl.multiple_of(lt * LANES, LANES)
            s, vals = [], []
            for hh in range(2):
                sk = _dot_nt(keys_ref[p, hh], q_scr[2 * p + hh, pl.ds(t0, LANES), :])
                s.append(sk)
                vals.append(_top_values(sk, k))
            v1m = [v - vals[0][0] for v in vals[0]]
            v2m = [v - vals[1][0] for v in vals[1]]
            row = lax.broadcasted_iota(jnp.int32, (8, LANES), 0)
            base = jnp.zeros((8, LANES), F32)
            for r in range(4):
                base = jnp.where(row == r, v1m[r], base)
                base = jnp.where(row == 4 + r, v2m[r], base)
            cands = [base + jnp.where(row < 4, v2m[i], v1m[i + 4] if i + 4 < k else NEG_INF) for i in range(k)]
            tops = _pop_top(cands, k + 1)
            thr = 0.5 * (tops[k - 1] + tops[k])
            zsum = jnp.exp(tops[0])
            for r in range(1, k):
                zsum = zsum + jnp.exp(tops[r])
            s1m = jnp.where(s[0] >= vals[0][k - 1], s[0] - vals[0][0], NEG_INF)
            s2m = jnp.where(s[1] >= vals[1][k - 1], s[1] - vals[1][0], NEG_INF)
            d1 = thr - s1m
            e1 = jnp.exp(s1m) * (0.5 / zsum)
            for j in range(nk // PEER_NI1):
                d1_scr[j, p, lt] = d1[j * PEER_NI1:(j + 1) * PEER_NI1, :]
                e1_scr[j, p, lt] = e1[j * PEER_NI1:(j + 1) * PEER_NI1, :]
            s2_scr[p, lt] = s2m
            e2_scr[p, lt] = jnp.exp(s2m)
            return carry

        lax.fori_loop(0, nlt * PEER_HEADS, unit, 0, unroll=8)

    def step(at_r, at_w):
        mxu_w = 2 * LANES
        for ts_i in range(t // mxu_w):
            ts = slice(ts_i * mxu_w, (ts_i + 1) * mxu_w)
            at_w[:, ts] = _dot_nt(u_ref[...], h_scr[ts, :])
            for ig in range(len(PEER_ILG)):
                il_lo = sum(PEER_ILG[:ig])
                il_hi = il_lo + PEER_ILG[ig]
                for lt in range(ts_i * mxu_w // LANES, (ts_i + 1) * mxu_w // LANES):
                    tl = slice(lt * LANES, (lt + 1) * LANES)
                    for hf in range(PEER_NSUB):
                        rows2 = slice(hf * hk, (hf + 1) * hk)
                        ils = range(il_lo, il_hi)
                        w = {il: jnp.zeros((hk, LANES), F32) for il in ils}
                        for p in range(PEER_HEADS):
                            s2h = s2_scr[p, lt, rows2, :]
                            e2h = e2_scr[p, lt, rows2, :]
                            for il in ils:
                                d1 = d1_scr[e, p, lt, il:il + 1, :]
                                e1 = e1_scr[e, p, lt, il:il + 1, :]
                                w[il] = w[il] + jnp.where(s2h >= d1, e2h, 0.0) * e1
                        for il in ils:
                            rows = slice(il * nk + hf * hk, il * nk + (hf + 1) * hk)
                            a = at_r[rows, tl]
                            u = a * (a * a * (GELU_C * 0.044715) + GELU_C)
                            g_scr[rows, tl] = (w[il] * (a + a * jnp.tanh(u))).astype(BF16)
                ks = slice(il_lo * nk, il_hi * nk)
                acc_scr[:, ts] += _dot(vt_ref[:, ks], g_scr[ks, ts])

    @pl.when(e % 2 == 0)
    def _even():
        step(at0_scr, at1_scr)

    @pl.when(e % 2 == 1)
    def _odd():
        step(at1_scr, at0_scr)

    @pl.when(e == n_eblk - 1)
    def _finish():
        y = acc_scr[...].T
        gate = m_ref[5:6, :]
        r = DEEPNORM_ALPHA * x_ref[...] + gate * y
        out_ref[...] = _layer_norm(r, lg_ref[...], lb_ref[...])


def _peer(x, mod, wq_bf, keys_bf, u_bf, vt_bf, ln_g, ln_b, layer, grp):
    nt, d = x.shape
    t = PEER_T
    nlt = t // LANES
    eblk = PEER_NI1 * PEER_NKEYS
    n_eblk = u_bf.shape[0] // eblk
    nq = wq_bf.shape[1]
    sel_shape = (PEER_HEADS, nlt, PEER_NKEYS, LANES)
    assert PEER_NI1 % 8 == 0
    blk_shape = (n_eblk, PEER_HEADS, nlt, PEER_NI1, LANES)
    return pl.pallas_call(
        functools.partial(_peer_kernel, n_eblk=n_eblk),
        grid=(nt // t, n_eblk),
        in_specs=[
            pl.BlockSpec((t, d), lambda b, e: (b, 0)),
            pl.BlockSpec((None, None, 6, d), lambda b, e: (layer, grp(b, t), 0, 0)),
            pl.BlockSpec((d, nq), lambda b, e: (0, 0)),
            pl.BlockSpec((PEER_HEADS, 2, PEER_NKEYS, PEER_DHALF), lambda b, e: (0, 0, 0, 0)),
            pl.BlockSpec((eblk, d), lambda b, e: (0, 0)),
            pl.BlockSpec((eblk, d), lambda b, e: (jnp.minimum(e + 1, n_eblk - 1), 0)),
            pl.BlockSpec((d, eblk), lambda b, e: (0, e)),
            pl.BlockSpec((None, 1, d), lambda b, e: (2 * layer + 1, 0, 0)),
            pl.BlockSpec((None, 1, d), lambda b, e: (2 * layer + 1, 0, 0)),
        ],
        out_specs=pl.BlockSpec((t, d), lambda b, e: (b, 0)),
        out_shape=jax.ShapeDtypeStruct((nt, d), F32),
        scratch_shapes=[
            pltpu.VMEM((t, d), BF16),
            pltpu.VMEM((2 * PEER_HEADS, t, PEER_DHALF), BF16),
            pltpu.VMEM(blk_shape, F32), pltpu.VMEM(blk_shape, F32),
            pltpu.VMEM(sel_shape, F32), pltpu.VMEM(sel_shape, F32),
            pltpu.VMEM((eblk, t), F32), pltpu.VMEM((eblk, t), F32),
            pltpu.VMEM((eblk, t), BF16),
            pltpu.VMEM((d, t), F32),
        ],
        compiler_params=_params(("parallel", "arbitrary")),
        name="peer",
    )(x, mod, wq_bf, keys_bf, u_bf, u_bf, vt_bf, ln_g, ln_b)


def _rope_angles(pos, dim):
    inv = ROPE_BASE ** (-jnp.arange(0, dim, 2, dtype=F32) / dim)
    ang = pos.astype(F32)[:, None] * inv[None, :]
    return jnp.cos(ang), jnp.sin(ang)


def _axial_tables(seq_len):
    t = jnp.arange(seq_len)
    half = ATT_DH // 2
    cr, sr = _rope_angles(t // GRID_W, half)
    cc, sc = _rope_angles(t % GRID_W, half)
    cos = jnp.concatenate([cr, cr, cc, cc], -1)
    sin = jnp.concatenate([-sr, sr, -sc, sc], -1)
    reps = LANES // ATT_DH
    return jnp.tile(cos, (1, reps)), jnp.tile(sin, (1, reps))


def kernel(x_prompt, x_sample, state_ret_fwd, state_ret_bwd, cache_k, cache_v, c, c_ctx, mod_w, mod_b, ln_g, ln_b,
           ret_w_in, ret_w_out, ret_decay, attn_w_in, attn_w_out, attn_sink, peer_wq, peer_keys, peer_u, peer_v):
    nb_p, len_p, d = x_prompt.shape
    nb_s, len_s, _ = x_sample.shape
    n_p = nb_p * len_p
    n_s = nb_s * len_s
    for tile in (PROJ_TM, OUT_TM, PEER_T):
        assert n_p % tile == 0 and len_s % tile == 0

    def grp(blk, tile):
        return _group_index(blk, n_p // tile, len_s // tile)

    x = jnp.concatenate([x_prompt.reshape(n_p, d), x_sample.reshape(n_s, d)], axis=0)
    n_grp = 1 + nb_s
    n_grp_pad = -(-n_grp // 8) * 8
    cond = jnp.concatenate([c_ctx[None, :], c, jnp.zeros((n_grp_pad - n_grp, d), F32)], axis=0)
    mod = _modulation(cond, mod_w, mod_b).reshape(DEPTH, n_grp_pad, 6, d)

    ln_g3 = ln_g.reshape(DEPTH * 2, 1, d)
    ln_b3 = ln_b.reshape(DEPTH * 2, 1, d)
    ret_rope = _rope_angles(jnp.arange(len_s), RET_DK)
    att_rope = _axial_tables(len_s)
    kv_w = ATT_HKV * ATT_DH
    cache_k4 = cache_k.reshape(cache_k.shape[0], cache_k.shape[1], cache_k.shape[2], kv_w)
    cache_v4 = cache_v.reshape(cache_v.shape[0], cache_v.shape[1], cache_v.shape[2], kv_w)

    new_sf, new_sb, new_k, new_v = [], [], [], []
    for i in range(DEPTH):
        j = i // N_MIXERS
        if i % N_MIXERS == 0:
            z = _project(x, mod, ret_w_in[j].astype(BF16), i, 0, grp)
            tables = _ret_tables(ret_decay[j])
            o_p, sf, sb = _retention(z, tables, 0, nb_p, len_p, False)
            o_s = _retention(z, tables, n_p, nb_s, len_s, True, rope=ret_rope,
                             states=(state_ret_fwd, state_ret_bwd), j=j)
            new_sf.append(sf)
            new_sb.append(sb)
            x = _ret_out(o_p, o_s, z, x, mod, ret_w_out[j].astype(BF16), ln_g3, ln_b3, i, grp)
        else:
            z = _project(x, mod, attn_w_in[j].astype(BF16), i, 0, grp)
            sink = attn_sink[j].astype(F32)
            o_p = _attn_context(z, sink, nb_p, len_p)
            o_s = _attn_latent(z, sink, cache_k4, cache_v4, att_rope, n_p, nb_s, len_s, j)
            hq_w = ATT_HQ * ATT_DH
            new_k.append(z[:n_p, hq_w:hq_w + kv_w].reshape(nb_p, len_p, ATT_HKV, ATT_DH))
            new_v.append(z[:n_p, hq_w + kv_w:hq_w + 2 * kv_w].reshape(nb_p, len_p, ATT_HKV, ATT_DH))
            x = _attn_out(o_p, o_s, x, mod, attn_w_out[j].astype(BF16), ln_g3, ln_b3, i, grp)
        x = _peer(x, mod, peer_wq[i].astype(BF16), peer_keys[i].astype(BF16), peer_u[i].astype(BF16),
                  peer_v[i].T.astype(BF16), ln_g3, ln_b3, i, grp)

    y_p = x[:n_p].reshape(nb_p, len_p, d)
    y_s = x[n_p:].reshape(nb_s, len_s, d)
    return (y_p, y_s, jnp.stack(new_sf, 1), jnp.stack(new_sb, 1), jnp.stack(new_k, 1), jnp.stack(new_v, 1))
```

```python
import functools
import math

import jax
import jax.numpy as jnp
from jax import lax
from jax.experimental import pallas as pl
from jax.experimental.pallas import tpu as pltpu

F32 = jnp.float32
BF16 = jnp.bfloat16

DEPTH = 4
N_MIXERS = 2
GRID_W = 64
RET_HEADS = 4
RET_DK = 256
RET_DV = 512
RET_CHUNK = 128
RET_SUB = 4
ATT_HQ = 16
ATT_HKV = 4
ATT_G = ATT_HQ // ATT_HKV
ATT_DH = 64
ATT_BLK = 128
ROPE_BASE = 10000.0
PEER_HEADS = 8
PEER_NKEYS = 128
PEER_DHALF = 128
PEER_TOPK = 16
DEEPNORM_ALPHA = (2.0 * DEPTH) ** 0.25
LN_EPS = 1e-5
GN_EPS = 1e-5
NEG_INF = -1e30
GELU_C = math.sqrt(2.0 / math.pi)

LANES = 128
VMEM_LIMIT = 56 * 1024 * 1024
PROJ_TM = 512
OUT_TM = 256
PEER_T = 512
PEER_NI1 = 8
PEER_ILG = (4, 2, 2)
PEER_NSUB = 2


def _dot(a, b):
    return jnp.dot(a, b, preferred_element_type=F32)


def _dot_nt(a, b):
    return lax.dot_general(a, b, (((1,), (1,)), ((), ())), preferred_element_type=F32)


def _params(sem, flags=None):
    return pltpu.CompilerParams(dimension_semantics=sem, vmem_limit_bytes=VMEM_LIMIT, flags=flags)


def _group_index(blk, n_prompt_blocks, blocks_per_seq):
    return jnp.where(blk < n_prompt_blocks, 0, 1 + (blk - n_prompt_blocks) // blocks_per_seq)


def _layer_norm(r, g, b):
    mu = jnp.mean(r, -1, keepdims=True)
    d = r - mu
    var = jnp.mean(d * d, -1, keepdims=True)
    return d * lax.rsqrt(var + LN_EPS) * g + b


def _mod_kernel(c_ref, w_ref, b_ref, o_ref):
    c = c_ref[...]
    a = (c * jax.nn.sigmoid(c)).astype(BF16)
    o_ref[...] = _dot(a, w_ref[...].astype(BF16)) + b_ref[...]


def _modulation(cond, mod_w, mod_b):
    ngp, d = cond.shape
    n_out = mod_w.shape[-1]
    tn = 1536
    return pl.pallas_call(
        _mod_kernel,
        grid=(DEPTH, n_out // tn),
        in_specs=[
            pl.BlockSpec((ngp, d), lambda i, n: (0, 0)),
            pl.BlockSpec((None, d, tn), lambda i, n: (i, 0, n)),
            pl.BlockSpec((None, 1, tn), lambda i, n: (i, 0, n)),
        ],
        out_specs=pl.BlockSpec((None, ngp, tn), lambda i, n: (i, 0, n)),
        out_shape=jax.ShapeDtypeStruct((DEPTH, ngp, n_out), F32),
        compiler_params=_params(("parallel", "parallel")),
        name="modulation",
    )(cond, mod_w, mod_b.reshape(DEPTH, 1, n_out))


def _proj_kernel(x_ref, m_ref, w_ref, o_ref, *, off):
    sh = m_ref[off:off + 1, :]
    sc = m_ref[off + 1:off + 2, :]
    h = (x_ref[...] * (1.0 + sc) + sh).astype(BF16)
    o_ref[...] = _dot(h, w_ref[...])


def _project(x, mod, w_bf, layer, off, grp):
    nt, d = x.shape
    n = w_bf.shape[1]
    tn = 1536
    tm = PROJ_TM
    return pl.pallas_call(
        functools.partial(_proj_kernel, off=off),
        grid=(n // tn, nt // tm),
        in_specs=[
            pl.BlockSpec((tm, d), lambda j, b: (b, 0)),
            pl.BlockSpec((None, None, 6, d), lambda j, b: (layer, grp(b, tm), 0, 0)),
            pl.BlockSpec((d, tn), lambda j, b: (0, j)),
        ],
        out_specs=pl.BlockSpec((tm, tn), lambda j, b: (b, j)),
        out_shape=jax.ShapeDtypeStruct((nt, n), F32),
        compiler_params=_params(("parallel", "parallel")),
        name="mod_proj",
    )(x, mod, w_bf)


def _rope_half(x, cos, sin):
    x1 = x[:, :LANES]
    x2 = x[:, LANES:]
    return jnp.concatenate([x1 * cos - x2 * sin, x1 * sin + x2 * cos], axis=1)


def _ret_kernel(*refs, latent, nc, n_sub):
    it = iter(refs)
    qa, ka, va, qb, kb, vb = (next(it) for _ in range(6))
    if latent:
        cosa, sina, cosb, sinb = (next(it) for _ in range(4))
    dfb, qdf, qdb, kdf, kdb, cdf, cdb = (next(it) for _ in range(7))
    if latent:
        s0f, s0b = next(it), next(it)
    o_ref = next(it)
    if not latent:
        sf_out, sb_out = next(it), next(it)
    sf, sb = next(it), next(it)

    c = pl.program_id(2)

    @pl.when(c == 0)
    def _init():
        o_ref[...] = jnp.zeros(o_ref.shape, F32)
        if latent:
            sf[...] = s0f[...]
            sb[...] = s0b[...]
        else:
            sf[...] = jnp.zeros(sf.shape, F32)
            sb[...] = jnp.zeros(sb.shape, F32)

    def prep(q, k, cos_ref, sin_ref, rows):
        k = k * (RET_DK ** -0.5)
        if latent:
            cos = cos_ref[rows, :]
            sin = sin_ref[rows, :]
            q = _rope_half(q, cos, sin)
            k = _rope_half(k, cos, sin)
        return q, k

    for sub in range(n_sub):
        ca = c * n_sub + sub
        ra = slice(sub * RET_CHUNK, (sub + 1) * RET_CHUNK)
        rb = slice((n_sub - 1 - sub) * RET_CHUNK, (n_sub - sub) * RET_CHUNK)

        q, k = prep(qa[ra, :], ka[ra, :], cosa if latent else None, sina if latent else None, ra)
        v = va[ra, :].astype(BF16)
        att = (_dot_nt(q.astype(BF16), k.astype(BF16)) * dfb[...]).astype(BF16)
        o_c = _dot(att, v) + _dot((q * qdf[...]).astype(BF16), sf[...].astype(BF16))
        rows_c = pl.ds(pl.multiple_of(ca * RET_CHUNK, RET_CHUNK), RET_CHUNK)
        o_ref[rows_c, :] += o_c
        kd_t = (k * kdf[...]).T.astype(BF16)
        sf[...] = sf[...] * cdf[...] + _dot(kd_t, v)

        q2, k2 = prep(qb[rb, :], kb[rb, :], cosb if latent else None, sinb if latent else None, rb)
        v2 = vb[rb, :].astype(BF16)
        o_b = _dot((q2 * qdb[...]).astype(BF16), sb[...].astype(BF16))
        rows_b = pl.ds(pl.multiple_of((nc - 1 - ca) * RET_CHUNK, RET_CHUNK), RET_CHUNK)
        o_ref[rows_b, :] += o_b
        kd2_t = (k2 * kdb[...]).T.astype(BF16)
        sb[...] = sb[...] * cdb[...] + _dot(kd2_t, v2)

    if not latent:
        @pl.when(c == nc // n_sub - 1)
        def _fin():
            sf_out[...] = sf[...]
            sb_out[...] = sb[...]


def _ret_tables(decay):
    log_g = jax.nn.log_sigmoid(decay.astype(F32))
    lf, lb = log_g[0], log_g[1]
    idx = jnp.arange(RET_CHUNK, dtype=F32)
    diff = idx[:, None] - idx[None, :]
    fmask = diff >= 0
    bmask = diff < 0
    df = jnp.where(fmask[None], jnp.exp(jnp.where(fmask, diff, 0.0)[None] * lf[:, None, None]), 0.0)
    db = jnp.where(bmask[None], jnp.exp(jnp.where(bmask, -diff, 0.0)[None] * lb[:, None, None]), 0.0)
    dfb = df + db
    def rows(e):
        return jnp.broadcast_to(e[:, :, None], (RET_HEADS, RET_CHUNK, RET_DK))
    qdf = rows(jnp.exp((idx + 1.0)[None, :] * lf[:, None]))
    kdf = rows(jnp.exp((RET_CHUNK - 1.0 - idx)[None, :] * lf[:, None]))
    qdb = rows(jnp.exp((RET_CHUNK - idx)[None, :] * lb[:, None]))
    kdb = rows(jnp.exp(idx[None, :] * lb[:, None]))
    cdf = jnp.broadcast_to(jnp.exp(RET_CHUNK * lf)[:, None, None], (RET_HEADS, 1, RET_DV))
    cdb = jnp.broadcast_to(jnp.exp(RET_CHUNK * lb)[:, None, None], (RET_HEADS, 1, RET_DV))
    return dfb, qdf, qdb, kdf, kdb, cdf, cdb


def _retention(z, tables, row0, nseq, seq_len, latent, rope=None, states=None, j=0):
    nc = seq_len // RET_CHUNK
    n_sub = math.gcd(RET_SUB, nc)
    blk = n_sub * RET_CHUNK
    ns = nc // n_sub
    base = row0 // blk
    h_ = RET_HEADS
    kcol = (h_ * RET_DK) // RET_DK
    vcol = (2 * h_ * RET_DK) // RET_DV

    def fw(b, h, c):
        return base + b * ns + c

    def bw(b, h, c):
        return base + b * ns + (ns - 1 - c)

    in_specs = [
        pl.BlockSpec((blk, RET_DK), lambda b, h, c: (fw(b, h, c), h)),
        pl.BlockSpec((blk, RET_DK), lambda b, h, c: (fw(b, h, c), kcol + h)),
        pl.BlockSpec((blk, RET_DV), lambda b, h, c: (fw(b, h, c), vcol + h)),
        pl.BlockSpec((blk, RET_DK), lambda b, h, c: (bw(b, h, c), h)),
        pl.BlockSpec((blk, RET_DK), lambda b, h, c: (bw(b, h, c), kcol + h)),
        pl.BlockSpec((blk, RET_DV), lambda b, h, c: (bw(b, h, c), vcol + h)),
    ]
    args = [z, z, z, z, z, z]
    if latent:
        cos, sin = rope
        in_specs += [
            pl.BlockSpec((blk, LANES), lambda b, h, c: (c, 0)),
            pl.BlockSpec((blk, LANES), lambda b, h, c: (c, 0)),
            pl.BlockSpec((blk, LANES), lambda b, h, c: (ns - 1 - c, 0)),
            pl.BlockSpec((blk, LANES), lambda b, h, c: (ns - 1 - c, 0)),
        ]
        args += [cos, sin, cos, sin]
    dfb, qdf, qdb, kdf, kdb, cdf, cdb = tables
    in_specs += [pl.BlockSpec((None, RET_CHUNK, RET_CHUNK), lambda b, h, c: (h, 0, 0))]
    in_specs += [pl.BlockSpec((None, RET_CHUNK, RET_DK), lambda b, h, c: (h, 0, 0))] * 4
    in_specs += [pl.BlockSpec((None, 1, RET_DV), lambda b, h, c: (h, 0, 0))] * 2
    args += [dfb, qdf, qdb, kdf, kdb, cdf, cdb]
    if latent:
        s0f, s0b = states
        st_spec = pl.BlockSpec((None, None, None, RET_DK, RET_DV), lambda b, h, c: (b, j, h, 0, 0))
        in_specs += [st_spec, st_spec]
        args += [s0f, s0b]

    o_shape = jax.ShapeDtypeStruct((nseq * seq_len, h_ * RET_DV), F32)
    o_spec = pl.BlockSpec((seq_len, RET_DV), lambda b, h, c: (b, h))
    if latent:
        out_shape, out_specs = o_shape, o_spec
    else:
        s_shape = jax.ShapeDtypeStruct((nseq, h_, RET_DK, RET_DV), F32)
        s_spec = pl.BlockSpec((None, None, RET_DK, RET_DV), lambda b, h, c: (b, h, 0, 0))
        out_shape, out_specs = (o_shape, s_shape, s_shape), (o_spec, s_spec, s_spec)

    return pl.pallas_call(
        functools.partial(_ret_kernel, latent=latent, nc=nc, n_sub=n_sub),
        grid=(nseq, h_, ns),
        in_specs=in_specs,
        out_specs=out_specs,
        out_shape=out_shape,
        scratch_shapes=[pltpu.VMEM((RET_DK, RET_DV), F32), pltpu.VMEM((RET_DK, RET_DV), F32)],
        compiler_params=_params(("parallel", "parallel", "arbitrary")),
        name="retention_latent" if latent else "retention_context",
    )(*args)


def _ret_out_kernel(op_ref, os_ref, g_ref, x_ref, m_ref, w_ref, lg_ref, lb_ref, out_ref, *, n_ctx_tiles):
    is_ctx = pl.program_id(0) < n_ctx_tiles
    y = None
    for h in range(RET_HEADS):
        cols = slice(h * RET_DV, (h + 1) * RET_DV)
        oh = jnp.where(is_ctx, op_ref[:, cols], os_ref[:, cols])
        mu = jnp.mean(oh, -1, keepdims=True)
        d = oh - mu
        var = jnp.mean(d * d, -1, keepdims=True)
        on = d * lax.rsqrt(var + GN_EPS)
        gh = g_ref[:, cols]
        act = (gh * jax.nn.sigmoid(gh) * on).astype(BF16)
        part = _dot(act, w_ref[cols, :])
        y = part if y is None else y + part
    gate = m_ref[2:3, :]
    r = DEEPNORM_ALPHA * x_ref[...] + gate * y
    out_ref[...] = _layer_norm(r, lg_ref[...], lb_ref[...])


def _ret_out(o_ctx, o_lat, z, x, mod, w_bf, ln_g, ln_b, layer, grp):
    nt, d = x.shape
    tm = OUT_TM
    hv = RET_HEADS * RET_DV
    gcol = (2 * RET_HEADS * RET_DK + hv) // hv
    n_ctx = o_ctx.shape[0] // tm
    return pl.pallas_call(
        functools.partial(_ret_out_kernel, n_ctx_tiles=n_ctx),
        grid=(nt // tm,),
        in_specs=[
            pl.BlockSpec((tm, hv), lambda b: (jnp.minimum(b, n_ctx - 1), 0)),
            pl.BlockSpec((tm, hv), lambda b: (jnp.maximum(b - n_ctx, 0), 0)),
            pl.BlockSpec((tm, hv), lambda b: (b, gcol)),
            pl.BlockSpec((tm, d), lambda b: (b, 0)),
            pl.BlockSpec((None, None, 6, d), lambda b: (layer, grp(b, tm), 0, 0)),
            pl.BlockSpec((hv, d), lambda b: (0, 0)),
            pl.BlockSpec((None, 1, d), lambda b: (2 * layer, 0, 0)),
            pl.BlockSpec((None, 1, d), lambda b: (2 * layer, 0, 0)),
        ],
        out_specs=pl.BlockSpec((tm, d), lambda b: (b, 0)),
        out_shape=jax.ShapeDtypeStruct((nt, d), F32),
        compiler_params=_params(("parallel",)),
        name="retention_out",
    )(o_ctx, o_lat, z, x, mod, w_bf, ln_g, ln_b)


def _softmax_pv(parts, sink):
    m = sink
    for s, _ in parts:
        m = jnp.maximum(m, jnp.max(s, -1, keepdims=True))
    den = jnp.exp(sink - m)
    o = None
    for s, v in parts:
        p = jnp.exp(s - m)
        den = den + jnp.sum(p, -1, keepdims=True)
        pv = _dot(p.astype(BF16), v)
        o = pv if o is None else o + pv
    return o / den


def _attn_ctx_kernel(sink_ref, q_ref, k_ref, v_ref, o_ref):
    scale = ATT_DH ** -0.5
    n = q_ref.shape[0]
    gidx = lax.broadcasted_iota(jnp.int32, (ATT_G * n, 1), 0) // n
    for h in range(ATT_HKV):
        kh = k_ref[:, h * ATT_DH:(h + 1) * ATT_DH].astype(BF16)
        vh = v_ref[:, h * ATT_DH:(h + 1) * ATT_DH].astype(BF16)
        heads = [h * ATT_G + g for g in range(ATT_G)]
        qg = jnp.concatenate([q_ref[:, hq * ATT_DH:(hq + 1) * ATT_DH] for hq in heads], axis=0)
        qg = (qg * scale).astype(BF16)
        sink = jnp.zeros((ATT_G * n, 1), F32)
        for g, hq in enumerate(heads):
            sink = jnp.where(gidx == g, sink_ref[hq], sink)
        o = _softmax_pv([(_dot_nt(qg, kh), vh)], sink)
        for g, hq in enumerate(heads):
            o_ref[:, hq * ATT_DH:(hq + 1) * ATT_DH] = o[g * n:(g + 1) * n, :]


def _attn_context(z, sink, nseq, seq_len):
    hq_w = ATT_HQ * ATT_DH
    kv_w = ATT_HKV * ATT_DH
    return pl.pallas_call(
        _attn_ctx_kernel,
        grid=(nseq,),
        in_specs=[
            pl.BlockSpec(memory_space=pltpu.SMEM),
            pl.BlockSpec((seq_len, hq_w), lambda b: (b, 0)),
            pl.BlockSpec((seq_len, kv_w), lambda b: (b, hq_w // kv_w)),
            pl.BlockSpec((seq_len, kv_w), lambda b: (b, hq_w // kv_w + 1)),
        ],
        out_specs=pl.BlockSpec((seq_len, hq_w), lambda b: (b, 0)),
        out_shape=jax.ShapeDtypeStruct((nseq * seq_len, hq_w), F32),
        compiler_params=_params(("parallel",)),
        name="attention_context",
    )(sink, z, z, z)


def _rope_axial(x, cos, sin):
    w = x.shape[1]
    lane = lax.broadcasted_iota(jnp.int32, (x.shape[0], LANES), 1)
    first = (lane % 32) < 16
    outs = []
    for cg in range(w // LANES):
        xg = x[:, cg * LANES:(cg + 1) * LANES]
        up = pltpu.roll(xg, LANES - 16, 1)
        dn = pltpu.roll(xg, 16, 1)
        outs.append(xg * cos + jnp.where(first, up, dn) * sin)
    return jnp.concatenate(outs, axis=1) if len(outs) > 1 else outs[0]


def _attn_lat_kernel(sink_ref, q_ref, kp_ref, kc_ref, kn_ref, vp_ref, vc_ref, vn_ref, ck_ref, cv_ref,
                     cq_ref, sq_ref, cp_ref, sp_ref, cn_ref, sn_ref, o_ref, *, nb):
    scale = ATT_DH ** -0.5
    qb = pl.program_id(1)
    q = _rope_axial(q_ref[...], cq_ref[...], sq_ref[...]) * scale
    kw = jnp.concatenate([
        _rope_axial(kp_ref[...], cp_ref[...], sp_ref[...]),
        _rope_axial(kc_ref[...], cq_ref[...], sq_ref[...]),
        _rope_axial(kn_ref[...], cn_ref[...], sn_ref[...]),
    ], axis=0)
    vw = jnp.concatenate([vp_ref[...], vc_ref[...], vn_ref[...]], axis=0)
    ck = ck_ref[...]
    cv = cv_ref[...]

    rows = ATT_G * ATT_BLK
    i = lax.broadcasted_iota(jnp.int32, (rows, 3 * ATT_BLK), 0) % ATT_BLK
    jj = lax.broadcasted_iota(jnp.int32, (rows, 3 * ATT_BLK), 1)
    lo = jnp.where(qb > 0, 0, ATT_BLK)
    hi = jnp.where(qb < nb - 1, 3 * ATT_BLK, 2 * ATT_BLK)
    valid = (jj >= jnp.maximum(i, lo)) & (jj < jnp.minimum(i + 2 * ATT_BLK + 1, hi))
    gidx = lax.broadcasted_iota(jnp.int32, (rows, 1), 0) // ATT_BLK

    for h in range(ATT_HKV):
        hs = slice(h * ATT_DH, (h + 1) * ATT_DH)
        kh = kw[:, hs].astype(BF16)
        vh = vw[:, hs].astype(BF16)
        ckh = ck[:, hs].astype(BF16)
        cvh = cv[:, hs].astype(BF16)
        heads = [h * ATT_G + g for g in range(ATT_G)]
        qg = jnp.concatenate([q[:, hq * ATT_DH:(hq + 1) * ATT_DH] for hq in heads], axis=0).astype(BF16)
        sink = jnp.zeros((rows, 1), F32)
        for g, hq in enumerate(heads):
            sink = jnp.where(gidx == g, sink_ref[hq], sink)
        s_loc = jnp.where(valid, _dot_nt(qg, kh), NEG_INF)
        s_ctx = _dot_nt(qg, ckh)
        o = _softmax_pv([(s_loc, vh), (s_ctx, cvh)], sink)
        for g, hq in enumerate(heads):
            o_ref[:, hq * ATT_DH:(hq + 1) * ATT_DH] = o[g * ATT_BLK:(g + 1) * ATT_BLK, :]


def _attn_latent(z, sink, cache_k, cache_v, rope, row0, nseq, seq_len, j):
    nb = seq_len // ATT_BLK
    base = row0 // ATT_BLK
    hq_w = ATT_HQ * ATT_DH
    kv_w = ATT_HKV * ATT_DH
    kcol = hq_w // kv_w
    past = cache_k.shape[2]
    cos, sin = rope

    def prv(q):
        return jnp.maximum(q - 1, 0)

    def nxt(q):
        return jnp.minimum(q + 1, nb - 1)

    def kv_spec(sel, col):
        return pl.BlockSpec((ATT_BLK, kv_w), lambda b, q: (base + b * nb + sel(q), col))

    def rope_spec(sel):
        return pl.BlockSpec((ATT_BLK, LANES), lambda b, q: (sel(q), 0))

    same = lambda q: q
    cache_spec = pl.BlockSpec((None, None, past, kv_w), lambda b, q: (b, j, 0, 0))
    return pl.pallas_call(
        functools.partial(_attn_lat_kernel, nb=nb),
        grid=(nseq, nb),
        in_specs=[
            pl.BlockSpec(memory_space=pltpu.SMEM),
            pl.BlockSpec((ATT_BLK, hq_w), lambda b, q: (base + b * nb + q, 0)),
            kv_spec(prv, kcol), kv_spec(same, kcol), kv_spec(nxt, kcol),
            kv_spec(prv, kcol + 1), kv_spec(same, kcol + 1), kv_spec(nxt, kcol + 1),
            cache_spec, cache_spec,
            rope_spec(same), rope_spec(same), rope_spec(prv), rope_spec(prv), rope_spec(nxt), rope_spec(nxt),
        ],
        out_specs=pl.BlockSpec((ATT_BLK, hq_w), lambda b, q: (b * nb + q, 0)),
        out_shape=jax.ShapeDtypeStruct((nseq * seq_len, hq_w), F32),
        compiler_params=_params(("parallel", "parallel")),
        name="attention_latent",
    )(sink, z, z, z, z, z, z, z, cache_k, cache_v, cos, sin, cos, sin, cos, sin)


def _attn_out_kernel(op_ref, os_ref, x_ref, m_ref, w_ref, lg_ref, lb_ref, out_ref, *, n_ctx_tiles):
    is_ctx = pl.program_id(0) < n_ctx_tiles
    o = jnp.where(is_ctx, op_ref[...], os_ref[...])
    y = _dot(o.astype(BF16), w_ref[...])
    gate = m_ref[2:3, :]
    r = DEEPNORM_ALPHA * x_ref[...] + gate * y
    out_ref[...] = _layer_norm(r, lg_ref[...], lb_ref[...])


def _attn_out(o_ctx, o_lat, x, mod, w_bf, ln_g, ln_b, layer, grp):
    nt, d = x.shape
    tm = OUT_TM
    k = o_ctx.shape[1]
    n_ctx = o_ctx.shape[0] // tm
    return pl.pallas_call(
        functools.partial(_attn_out_kernel, n_ctx_tiles=n_ctx),
        grid=(nt // tm,),
        in_specs=[
            pl.BlockSpec((tm, k), lambda b: (jnp.minimum(b, n_ctx - 1), 0)),
            pl.BlockSpec((tm, k), lambda b: (jnp.maximum(b - n_ctx, 0), 0)),
            pl.BlockSpec((tm, d), lambda b: (b, 0)),
            pl.BlockSpec((None, None, 6, d), lambda b: (layer, grp(b, tm), 0, 0)),
            pl.BlockSpec((k, d), lambda b: (0, 0)),
            pl.BlockSpec((None, 1, d), lambda b: (2 * layer, 0, 0)),
            pl.BlockSpec((None, 1, d), lambda b: (2 * layer, 0, 0)),
        ],
        out_specs=pl.BlockSpec((tm, d), lambda b: (b, 0)),
        out_shape=jax.ShapeDtypeStruct((nt, d), F32),
        compiler_params=_params(("parallel",)),
        name="attention_out",
    )(o_ctx, o_lat, x, mod, w_bf, ln_g, ln_b)


def _sort_network(n):
    pairs = []

    def merge(lo, hi, r):
        step = r * 2
        if step < hi - lo:
            merge(lo, hi, step)
            merge(lo + r, hi, step)
            for i in range(lo + r, hi - r, step):
                pairs.append((i, i + r))
        else:
            pairs.append((lo, lo + r))

    def sort(lo, hi):
        if hi - lo >= 1:
            mid = lo + (hi - lo) // 2
            sort(lo, mid)
            sort(mid + 1, hi)
            merge(lo, hi, 1)

    sort(0, n - 1)
    return pairs


def _pop_top(lists, k):
    vals = []
    for r in range(k):
        head = lists[0]
        m = jnp.max(head, axis=0, keepdims=True)
        vals.append(m)
        left = k - 1 - r
        if left > 0:
            took = head >= m
            nxt = [lists[i + 1] if i + 1 < len(lists) else NEG_INF for i in range(left)]
            lists = [jnp.where(took, nxt[i], lists[i]) for i in range(left)]
    return vals


def _top_values(s, k):
    tiles = [s[8 * j:8 * (j + 1), :] for j in range(s.shape[0] // 8)]
    for i, j in _sort_network(len(tiles)):
        tiles[i], tiles[j] = jnp.maximum(tiles[i], tiles[j]), jnp.minimum(tiles[i], tiles[j])
    return _pop_top(tiles, k)


def _peer_kernel(x_ref, m_ref, wq_ref, keys_ref, u0_ref, u_ref, vt_ref, lg_ref, lb_ref, out_ref,
                 h_scr, q_scr, d1_scr, e1_scr, s2_scr, e2_scr, at0_scr, at1_scr, g_scr, acc_scr,
                 *, n_eblk):
    t = x_ref.shape[0]
    nlt = t // LANES
    e = pl.program_id(1)
    k = PEER_TOPK
    nk = PEER_NKEYS
    hk = nk // PEER_NSUB

    @pl.when(e == 0)
    def _scores():
        sh = m_ref[3:4, :]
        sc = m_ref[4:5, :]
        h_scr[...] = (x_ref[...] * (1.0 + sc) + sh).astype(BF16)
        q = _dot(h_scr[...], wq_ref[...])
        for ph in range(2 * PEER_HEADS):
            q_scr[ph] = q[:, ph * PEER_DHALF:(ph + 1) * PEER_DHALF].astype(BF16)
        acc_scr[...] = jnp.zeros(acc_scr.shape, F32)
        at0_scr[...] = _dot_nt(u0_ref[...], h_scr[...])

        def unit(uidx, carry):
            lt = uidx // PEER_HEADS
            p = uidx % PEER_HEADS
            t0 = pl.multiple_of(lt * LANES, LANES)
            s, vals = [], []
            for hh in range(2):
                sk = _dot_nt(keys_ref[p, hh], q_scr[2 * p + hh, pl.ds(t0, LANES), :])
                s.append(sk)
                vals.append(_top_values(sk, k))
            v1m = [v - vals[0][0] for v in vals[0]]
            v2m = [v - vals[1][0] for v in vals[1]]
            row = lax.broadcasted_iota(jnp.int32, (8, LANES), 0)
            base = jnp.zeros((8, LANES), F32)
            for r in range(4):
                base = jnp.where(row == r, v1m[r], base)
                base = jnp.where(row == 4 + r, v2m[r], base)
            cands = [base + jnp.where(row < 4, v2m[i], v1m[i + 4] if i + 4 < k else NEG_INF) for i in range(k)]
            tops = _pop_top(cands, k + 1)
            thr = 0.5 * (tops[k - 1] + tops[k])
            zsum = jnp.exp(tops[0])
            for r in range(1, k):
                zsum = zsum + jnp.exp(tops[r])
            s1m = jnp.where(s[0] >= vals[0][k - 1], s[0] - vals[0][0], NEG_INF)
            s2m = jnp.where(s[1] >= vals[1][k - 1], s[1] - vals[1][0], NEG_INF)
            d1 = thr - s1m
            e1 = jnp.exp(s1m) * (0.5 / zsum)
            for j in range(nk // PEER_NI1):
                d1_scr[j, p, lt] = d1[j * PEER_NI1:(j + 1) * PEER_NI1, :]
                e1_scr[j, p, lt] = e1[j * PEER_NI1:(j + 1) * PEER_NI1, :]
            s2_scr[p, lt] = s2m
            e2_scr[p, lt] = jnp.exp(s2m)
            return carry

        lax.fori_loop(0, nlt * PEER_HEADS, unit, 0, unroll=8)

    def step(at_r, at_w):
        mxu_w = 2 * LANES
        for ts_i in range(t // mxu_w):
            ts = slice(ts_i * mxu_w, (ts_i + 1) * mxu_w)
            at_w[:, ts] = _dot_nt(u_ref[...], h_scr[ts, :])
            for ig in range(len(PEER_ILG)):
                il_lo = sum(PEER_ILG[:ig])
                il_hi = il_lo + PEER_ILG[ig]
                for lt in range(ts_i * mxu_w // LANES, (ts_i + 1) * mxu_w // LANES):
                    tl = slice(lt * LANES, (lt + 1) * LANES)
                    for hf in range(PEER_NSUB):
                        rows2 = slice(hf * hk, (hf + 1) * hk)
                        ils = range(il_lo, il_hi)
                        w = {il: jnp.zeros((hk, LANES), F32) for il in ils}
                        for p in range(PEER_HEADS):
                            s2h = s2_scr[p, lt, rows2, :]
                            e2h = e2_scr[p, lt, rows2, :]
                            for il in ils:
                                d1 = d1_scr[e, p, lt, il:il + 1, :]
                                e1 = e1_scr[e, p, lt, il:il + 1, :]
                                w[il] = w[il] + jnp.where(s2h >= d1, e2h, 0.0) * e1
                        for il in ils:
                            rows = slice(il * nk + hf * hk, il * nk + (hf + 1) * hk)
                            a = at_r[rows, tl]
                            u = a * (a * a * (GELU_C * 0.044715) + GELU_C)
                            g_scr[rows, tl] = (w[il] * (a + a * jnp.tanh(u))).astype(BF16)
                ks = slice(il_lo * nk, il_hi * nk)
                acc_scr[:, ts] += _dot(vt_ref[:, ks], g_scr[ks, ts])

    @pl.when(e % 2 == 0)
    def _even():
        step(at0_scr, at1_scr)

    @pl.when(e % 2 == 1)
    def _odd():
        step(at1_scr, at0_scr)

    @pl.when(e == n_eblk - 1)
    def _finish():
        y = acc_scr[...].T
        gate = m_ref[5:6, :]
        r = DEEPNORM_ALPHA * x_ref[...] + gate * y
        out_ref[...] = _layer_norm(r, lg_ref[...], lb_ref[...])


def _peer(x, mod, wq_bf, keys_bf, u_bf, vt_bf, ln_g, ln_b, layer, grp):
    nt, d = x.shape
    t = PEER_T
    nlt = t // LANES
    eblk = PEER_NI1 * PEER_NKEYS
    n_eblk = u_bf.shape[0] // eblk
    nq = wq_bf.shape[1]
    sel_shape = (PEER_HEADS, nlt, PEER_NKEYS, LANES)
    assert PEER_NI1 % 8 == 0
    blk_shape = (n_eblk, PEER_HEADS, nlt, PEER_NI1, LANES)
    return pl.pallas_call(
        functools.partial(_peer_kernel, n_eblk=n_eblk),
        grid=(nt // t, n_eblk),
        in_specs=[
            pl.BlockSpec((t, d), lambda b, e: (b, 0)),
            pl.BlockSpec((None, None, 6, d), lambda b, e: (layer, grp(b, t), 0, 0)),
            pl.BlockSpec((d, nq), lambda b, e: (0, 0)),
            pl.BlockSpec((PEER_HEADS, 2, PEER_NKEYS, PEER_DHALF), lambda b, e: (0, 0, 0, 0)),
            pl.BlockSpec((eblk, d), lambda b, e: (0, 0)),
            pl.BlockSpec((eblk, d), lambda b, e: (jnp.minimum(e + 1, n_eblk - 1), 0)),
            pl.BlockSpec((d, eblk), lambda b, e: (0, e)),
            pl.BlockSpec((None, 1, d), lambda b, e: (2 * layer + 1, 0, 0)),
            pl.BlockSpec((None, 1, d), lambda b, e: (2 * layer + 1, 0, 0)),
        ],
        out_specs=pl.BlockSpec((t, d), lambda b, e: (b, 0)),
        out_shape=jax.ShapeDtypeStruct((nt, d), F32),
        scratch_shapes=[
            pltpu.VMEM((t, d), BF16),
            pltpu.VMEM((2 * PEER_HEADS, t, PEER_DHALF), BF16),
            pltpu.VMEM(blk_shape, F32), pltpu.VMEM(blk_shape, F32),
            pltpu.VMEM(sel_shape, F32), pltpu.VMEM(sel_shape, F32),
            pltpu.VMEM((eblk, t), F32), pltpu.VMEM((eblk, t), F32),
            pltpu.VMEM((eblk, t), BF16),
            pltpu.VMEM((d, t), F32),
        ],
        compiler_params=_params(("parallel", "arbitrary")),
        name="peer",
    )(x, mod, wq_bf, keys_bf, u_bf, u_bf, vt_bf, ln_g, ln_b)


def _rope_angles(pos, dim):
    inv = ROPE_BASE ** (-jnp.arange(0, dim, 2, dtype=F32) / dim)
    ang = pos.astype(F32)[:, None] * inv[None, :]
    return jnp.cos(ang), jnp.sin(ang)


def _axial_tables(seq_len):
    t = jnp.arange(seq_len)
    half = ATT_DH // 2
    cr, sr = _rope_angles(t // GRID_W, half)
    cc, sc = _rope_angles(t % GRID_W, half)
    cos = jnp.concatenate([cr, cr, cc, cc], -1)
    sin = jnp.concatenate([-sr, sr, -sc, sc], -1)
    reps = LANES // ATT_DH
    return jnp.tile(cos, (1, reps)), jnp.tile(sin, (1, reps))


def kernel(x_prompt, x_sample, state_ret_fwd, state_ret_bwd, cache_k, cache_v, c, c_ctx, mod_w, mod_b, ln_g, ln_b,
           ret_w_in, ret_w_out, ret_decay, attn_w_in, attn_w_out, attn_sink, peer_wq, peer_keys, peer_u, peer_v):
    nb_p, len_p, d = x_prompt.shape
    nb_s, len_s, _ = x_sample.shape
    n_p = nb_p * len_p
    n_s = nb_s * len_s
    for tile in (PROJ_TM, OUT_TM, PEER_T):
        assert n_p % tile == 0 and len_s % tile == 0

    def grp(blk, tile):
        return _group_index(blk, n_p // tile, len_s // tile)

    x = jnp.concatenate([x_prompt.reshape(n_p, d), x_sample.reshape(n_s, d)], axis=0)
    n_grp = 1 + nb_s
    n_grp_pad = -(-n_grp // 8) * 8
    cond = jnp.concatenate([c_ctx[None, :], c, jnp.zeros((n_grp_pad - n_grp, d), F32)], axis=0)
    mod = _modulation(cond, mod_w, mod_b).reshape(DEPTH, n_grp_pad, 6, d)

    ln_g3 = ln_g.reshape(DEPTH * 2, 1, d)
    ln_b3 = ln_b.reshape(DEPTH * 2, 1, d)
    ret_rope = _rope_angles(jnp.arange(len_s), RET_DK)
    att_rope = _axial_tables(len_s)
    kv_w = ATT_HKV * ATT_DH
    cache_k4 = cache_k.reshape(cache_k.shape[0], cache_k.shape[1], cache_k.shape[2], kv_w)
    cache_v4 = cache_v.reshape(cache_v.shape[0], cache_v.shape[1], cache_v.shape[2], kv_w)

    new_sf, new_sb, new_k, new_v = [], [], [], []
    for i in range(DEPTH):
        j = i // N_MIXERS
        if i % N_MIXERS == 0:
            z = _project(x, mod, ret_w_in[j].astype(BF16), i, 0, grp)
            tables = _ret_tables(ret_decay[j])
            o_p, sf, sb = _retention(z, tables, 0, nb_p, len_p, False)
            o_s = _retention(z, tables, n_p, nb_s, len_s, True, rope=ret_rope,
                             states=(state_ret_fwd, state_ret_bwd), j=j)
            new_sf.append(sf)
            new_sb.append(sb)
            x = _ret_out(o_p, o_s, z, x, mod, ret_w_out[j].astype(BF16), ln_g3, ln_b3, i, grp)
        else:
            z = _project(x, mod, attn_w_in[j].astype(BF16), i, 0, grp)
            sink = attn_sink[j].astype(F32)
            o_p = _attn_context(z, sink, nb_p, len_p)
            o_s = _attn_latent(z, sink, cache_k4, cache_v4, att_rope, n_p, nb_s, len_s, j)
            hq_w = ATT_HQ * ATT_DH
            new_k.append(z[:n_p, hq_w:hq_w + kv_w].reshape(nb_p, len_p, ATT_HKV, ATT_DH))
            new_v.append(z[:n_p, hq_w + kv_w:hq_w + 2 * kv_w].reshape(nb_p, len_p, ATT_HKV, ATT_DH))
            x = _attn_out(o_p, o_s, x, mod, attn_w_out[j].astype(BF16), ln_g3, ln_b3, i, grp)
        x = _peer(x, mod, peer_wq[i].astype(BF16), peer_keys[i].astype(BF16), peer_u[i].astype(BF16),
                  peer_v[i].T.astype(BF16), ln_g3, ln_b3, i, grp)

    y_p = x[:n_p].reshape(nb_p, len_p, d)
    y_s = x[n_p:].reshape(nb_s, len_s, d)
    return (y_p, y_s, jnp.stack(new_sf, 1), jnp.stack(new_sb, 1), jnp.stack(new_k, 1), jnp.stack(new_v, 1))
```

```python
import functools
import math

import jax
import jax.numpy as jnp
from jax import lax
from jax.experimental import pallas as pl
from jax.experimental.pallas import tpu as pltpu

F32 = jnp.float32
BF16 = jnp.bfloat16

DEPTH = 4
N_MIXERS = 2
GRID_W = 64
RET_HEADS = 4
RET_DK = 256
RET_DV = 512
RET_CHUNK = 128
RET_SUB = 8
ATT_HQ = 16
ATT_HKV = 4
ATT_G = ATT_HQ // ATT_HKV
ATT_DH = 64
ATT_BLK = 128
ROPE_BASE = 10000.0
PEER_HEADS = 8
PEER_NKEYS = 128
PEER_DHALF = 128
PEER_TOPK = 16
DEEPNORM_ALPHA = (2.0 * DEPTH) ** 0.25
LN_EPS = 1e-5
GN_EPS = 1e-5
NEG_INF = -1e30
GELU_C = math.sqrt(2.0 / math.pi)

LANES = 128
VMEM_LIMIT = 56 * 1024 * 1024
PROJ_TM = 512
PROJ_TN = 3072
OUT_TM = 256
PEER_T = 512
PEER_NI1 = 8
PEER_ILG = (4, 2, 2)
PEER_NSUB = 2


def _dot(a, b):
    return jnp.dot(a, b, preferred_element_type=F32)


def _dot_nt(a, b):
    return lax.dot_general(a, b, (((1,), (1,)), ((), ())), preferred_element_type=F32)


def _params(sem, flags=None):
    return pltpu.CompilerParams(dimension_semantics=sem, vmem_limit_bytes=VMEM_LIMIT, flags=flags)


def _group_index(blk, n_prompt_blocks, blocks_per_seq):
    return jnp.where(blk < n_prompt_blocks, 0, 1 + (blk - n_prompt_blocks) // blocks_per_seq)


def _layer_norm(r, g, b):
    mu = jnp.mean(r, -1, keepdims=True)
    d = r - mu
    var = jnp.mean(d * d, -1, keepdims=True)
    return d * lax.rsqrt(var + LN_EPS) * g + b


def _mod_kernel(c_ref, w_ref, b_ref, o_ref):
    c = c_ref[...]
    a = (c * jax.nn.sigmoid(c)).astype(BF16)
    o_ref[...] = _dot(a, w_ref[...].astype(BF16)) + b_ref[...]


def _modulation(cond, mod_w, mod_b):
    ngp, d = cond.shape
    n_out = mod_w.shape[-1]
    tn = 1536
    return pl.pallas_call(
        _mod_kernel,
        grid=(DEPTH, n_out // tn),
        in_specs=[
            pl.BlockSpec((ngp, d), lambda i, n: (0, 0)),
            pl.BlockSpec((None, d, tn), lambda i, n: (i, 0, n)),
            pl.BlockSpec((None, 1, tn), lambda i, n: (i, 0, n)),
        ],
        out_specs=pl.BlockSpec((None, ngp, tn), lambda i, n: (i, 0, n)),
        out_shape=jax.ShapeDtypeStruct((DEPTH, ngp, n_out), F32),
        compiler_params=_params(("parallel", "parallel")),
        name="modulation",
    )(cond, mod_w, mod_b.reshape(DEPTH, 1, n_out))


def _proj_kernel(x_ref, m_ref, w_ref, o_ref, *, off):
    sh = m_ref[off:off + 1, :]
    sc = m_ref[off + 1:off + 2, :]
    h = (x_ref[...] * (1.0 + sc) + sh).astype(BF16)
    o_ref[...] = _dot(h, w_ref[...])


def _project(x, mod, w_bf, layer, off, grp):
    nt, d = x.shape
    n = w_bf.shape[1]
    tn = PROJ_TN if n % PROJ_TN == 0 else n
    tm = PROJ_TM
    return pl.pallas_call(
        functools.partial(_proj_kernel, off=off),
        grid=(n // tn, nt // tm),
        in_specs=[
            pl.BlockSpec((tm, d), lambda j, b: (b, 0)),
            pl.BlockSpec((None, None, 6, d), lambda j, b: (layer, grp(b, tm), 0, 0)),
            pl.BlockSpec((d, tn), lambda j, b: (0, j)),
        ],
        out_specs=pl.BlockSpec((tm, tn), lambda j, b: (b, j)),
        out_shape=jax.ShapeDtypeStruct((nt, n), F32),
        compiler_params=_params(("parallel", "parallel")),
        name="mod_proj",
    )(x, mod, w_bf)


def _rope_half(x, cos, sin):
    x1 = x[:, :LANES]
    x2 = x[:, LANES:]
    return jnp.concatenate([x1 * cos - x2 * sin, x1 * sin + x2 * cos], axis=1)


def _ret_kernel(*refs, latent, nc, n_sub):
    it = iter(refs)
    qa, ka, va, qb, kb, vb = (next(it) for _ in range(6))
    if latent:
        cosa, sina, cosb, sinb = (next(it) for _ in range(4))
    dfb, qdf, qdb, kdf, kdb, cdf, cdb = (next(it) for _ in range(7))
    if latent:
        s0f, s0b = next(it), next(it)
    o_ref = next(it)
    if not latent:
        sf_out, sb_out = next(it), next(it)
    sf, sb = next(it), next(it)

    c = pl.program_id(2)

    @pl.when(c == 0)
    def _init():
        o_ref[...] = jnp.zeros(o_ref.shape, F32)
        if latent:
            sf[...] = s0f[...]
            sb[...] = s0b[...]
        else:
            sf[...] = jnp.zeros(sf.shape, F32)
            sb[...] = jnp.zeros(sb.shape, F32)

    def prep(q, k, cos_ref, sin_ref, rows):
        k = k * (RET_DK ** -0.5)
        if latent:
            cos = cos_ref[rows, :]
            sin = sin_ref[rows, :]
            q = _rope_half(q, cos, sin)
            k = _rope_half(k, cos, sin)
        return q, k

    for sub in range(n_sub):
        ca = c * n_sub + sub
        ra = slice(sub * RET_CHUNK, (sub + 1) * RET_CHUNK)
        rb = slice((n_sub - 1 - sub) * RET_CHUNK, (n_sub - sub) * RET_CHUNK)

        q, k = prep(qa[ra, :], ka[ra, :], cosa if latent else None, sina if latent else None, ra)
        v = va[ra, :].astype(BF16)
        att = (_dot_nt(q.astype(BF16), k.astype(BF16)) * dfb[...]).astype(BF16)
        o_c = _dot(att, v) + _dot((q * qdf[...]).astype(BF16), sf[...].astype(BF16))
        rows_c = pl.ds(pl.multiple_of(ca * RET_CHUNK, RET_CHUNK), RET_CHUNK)
        o_ref[rows_c, :] += o_c
        kd_t = (k * kdf[...]).T.astype(BF16)
        sf[...] = sf[...] * cdf[...] + _dot(kd_t, v)

        q2, k2 = prep(qb[rb, :], kb[rb, :], cosb if latent else None, sinb if latent else None, rb)
        v2 = vb[rb, :].astype(BF16)
        o_b = _dot((q2 * qdb[...]).astype(BF16), sb[...].astype(BF16))
        rows_b = pl.ds(pl.multiple_of((nc - 1 - ca) * RET_CHUNK, RET_CHUNK), RET_CHUNK)
        o_ref[rows_b, :] += o_b
        kd2_t = (k2 * kdb[...]).T.astype(BF16)
        sb[...] = sb[...] * cdb[...] + _dot(kd2_t, v2)

    if not latent:
        @pl.when(c == nc // n_sub - 1)
        def _fin():
            sf_out[...] = sf[...]
            sb_out[...] = sb[...]


def _ret_tables(decay):
    log_g = jax.nn.log_sigmoid(decay.astype(F32))
    lf, lb = log_g[0], log_g[1]
    idx = jnp.arange(RET_CHUNK, dtype=F32)
    diff = idx[:, None] - idx[None, :]
    fmask = diff >= 0
    bmask = diff < 0
    df = jnp.where(fmask[None], jnp.exp(jnp.where(fmask, diff, 0.0)[None] * lf[:, None, None]), 0.0)
    db = jnp.where(bmask[None], jnp.exp(jnp.where(bmask, -diff, 0.0)[None] * lb[:, None, None]), 0.0)
    dfb = df + db
    def rows(e):
        return jnp.broadcast_to(e[:, :, None], (RET_HEADS, RET_CHUNK, RET_DK))
    qdf = rows(jnp.exp((idx + 1.0)[None, :] * lf[:, None]))
    kdf = rows(jnp.exp((RET_CHUNK - 1.0 - idx)[None, :] * lf[:, None]))
    qdb = rows(jnp.exp((RET_CHUNK - idx)[None, :] * lb[:, None]))
    kdb = rows(jnp.exp(idx[None, :] * lb[:, None]))
    cdf = jnp.broadcast_to(jnp.exp(RET_CHUNK * lf)[:, None, None], (RET_HEADS, 1, RET_DV))
    cdb = jnp.broadcast_to(jnp.exp(RET_CHUNK * lb)[:, None, None], (RET_HEADS, 1, RET_DV))
    return dfb, qdf, qdb, kdf, kdb, cdf, cdb


def _retention(z, tables, row0, nseq, seq_len, latent, rope=None, states=None, j=0):
    nc = seq_len // RET_CHUNK
    n_sub = math.gcd(RET_SUB, nc)
    blk = n_sub * RET_CHUNK
    ns = nc // n_sub
    base = row0 // blk
    h_ = RET_HEADS
    kcol = (h_ * RET_DK) // RET_DK
    vcol = (2 * h_ * RET_DK) // RET_DV

    def fw(b, h, c):
        return base + b * ns + c

    def bw(b, h, c):
        return base + b * ns + (ns - 1 - c)

    in_specs = [
        pl.BlockSpec((blk, RET_DK), lambda b, h, c: (fw(b, h, c), h)),
        pl.BlockSpec((blk, RET_DK), lambda b, h, c: (fw(b, h, c), kcol + h)),
        pl.BlockSpec((blk, RET_DV), lambda b, h, c: (fw(b, h, c), vcol + h)),
        pl.BlockSpec((blk, RET_DK), lambda b, h, c: (bw(b, h, c), h)),
        pl.BlockSpec((blk, RET_DK), lambda b, h, c: (bw(b, h, c), kcol + h)),
        pl.BlockSpec((blk, RET_DV), lambda b, h, c: (bw(b, h, c), vcol + h)),
    ]
    args = [z, z, z, z, z, z]
    if latent:
        cos, sin = rope
        in_specs += [
            pl.BlockSpec((blk, LANES), lambda b, h, c: (c, 0)),
            pl.BlockSpec((blk, LANES), lambda b, h, c: (c, 0)),
            pl.BlockSpec((blk, LANES), lambda b, h, c: (ns - 1 - c, 0)),
            pl.BlockSpec((blk, LANES), lambda b, h, c: (ns - 1 - c, 0)),
        ]
        args += [cos, sin, cos, sin]
    dfb, qdf, qdb, kdf, kdb, cdf, cdb = tables
    in_specs += [pl.BlockSpec((None, RET_CHUNK, RET_CHUNK), lambda b, h, c: (h, 0, 0))]
    in_specs += [pl.BlockSpec((None, RET_CHUNK, RET_DK), lambda b, h, c: (h, 0, 0))] * 4
    in_specs += [pl.BlockSpec((None, 1, RET_DV), lambda b, h, c: (h, 0, 0))] * 2
    args += [dfb, qdf, qdb, kdf, kdb, cdf, cdb]
    if latent:
        s0f, s0b = states
        st_spec = pl.BlockSpec((None, None, None, RET_DK, RET_DV), lambda b, h, c: (b, j, h, 0, 0))
        in_specs += [st_spec, st_spec]
        args += [s0f, s0b]

    o_shape = jax.ShapeDtypeStruct((nseq * seq_len, h_ * RET_DV), F32)
    o_spec = pl.BlockSpec((seq_len, RET_DV), lambda b, h, c: (b, h))
    if latent:
        out_shape, out_specs = o_shape, o_spec
    else:
        s_shape = jax.ShapeDtypeStruct((nseq, h_, RET_DK, RET_DV), F32)
        s_spec = pl.BlockSpec((None, None, RET_DK, RET_DV), lambda b, h, c: (b, h, 0, 0))
        out_shape, out_specs = (o_shape, s_shape, s_shape), (o_spec, s_spec, s_spec)

    return pl.pallas_call(
        functools.partial(_ret_kernel, latent=latent, nc=nc, n_sub=n_sub),
        grid=(nseq, h_, ns),
        in_specs=in_specs,
        out_specs=out_specs,
        out_shape=out_shape,
        scratch_shapes=[pltpu.VMEM((RET_DK, RET_DV), F32), pltpu.VMEM((RET_DK, RET_DV), F32)],
        compiler_params=_params(("parallel", "parallel", "arbitrary")),
        name="retention_latent" if latent else "retention_context",
    )(*args)


def _ret_out_kernel(op_ref, os_ref, g_ref, x_ref, m_ref, w_ref, lg_ref, lb_ref, out_ref, *, n_ctx_tiles):
    is_ctx = pl.program_id(0) < n_ctx_tiles
    y = None
    for h in range(RET_HEADS):
        cols = slice(h * RET_DV, (h + 1) * RET_DV)
        oh = jnp.where(is_ctx, op_ref[:, cols], os_ref[:, cols])
        mu = jnp.mean(oh, -1, keepdims=True)
        d = oh - mu
        var = jnp.mean(d * d, -1, keepdims=True)
        on = d * lax.rsqrt(var + GN_EPS)
        gh = g_ref[:, cols]
        act = (gh * jax.nn.sigmoid(gh) * on).astype(BF16)
        part = _dot(act, w_ref[cols, :])
        y = part if y is None else y + part
    gate = m_ref[2:3, :]
    r = DEEPNORM_ALPHA * x_ref[...] + gate * y
    out_ref[...] = _layer_norm(r, lg_ref[...], lb_ref[...])


def _ret_out(o_ctx, o_lat, z, x, mod, w_bf, ln_g, ln_b, layer, grp):
    nt, d = x.shape
    tm = OUT_TM
    hv = RET_HEADS * RET_DV
    gcol = (2 * RET_HEADS * RET_DK + hv) // hv
    n_ctx = o_ctx.shape[0] // tm
    return pl.pallas_call(
        functools.partial(_ret_out_kernel, n_ctx_tiles=n_ctx),
        grid=(nt // tm,),
        in_specs=[
            pl.BlockSpec((tm, hv), lambda b: (jnp.minimum(b, n_ctx - 1), 0)),
            pl.BlockSpec((tm, hv), lambda b: (jnp.maximum(b - n_ctx, 0), 0)),
            pl.BlockSpec((tm, hv), lambda b: (b, gcol)),
            pl.BlockSpec((tm, d), lambda b: (b, 0)),
            pl.BlockSpec((None, None, 6, d), lambda b: (layer, grp(b, tm), 0, 0)),
            pl.BlockSpec((hv, d), lambda b: (0, 0)),
            pl.BlockSpec((None, 1, d), lambda b: (2 * layer, 0, 0)),
            pl.BlockSpec((None, 1, d), lambda b: (2 * layer, 0, 0)),
        ],
        out_specs=pl.BlockSpec((tm, d), lambda b: (b, 0)),
        out_shape=jax.ShapeDtypeStruct((nt, d), F32),
        compiler_params=_params(("parallel",)),
        name="retention_out",
    )(o_ctx, o_lat, z, x, mod, w_bf, ln_g, ln_b)


def _softmax_pv(parts, sink):
    m = sink
    for s, _ in parts:
        m = jnp.maximum(m, jnp.max(s, -1, keepdims=True))
    den = jnp.exp(sink - m)
    o = None
    for s, v in parts:
        p = jnp.exp(s - m)
        den = den + jnp.sum(p, -1, keepdims=True)
        pv = _dot(p.astype(BF16), v)
        o = pv if o is None else o + pv
    return o / den


def _attn_ctx_kernel(sink_ref, q_ref, k_ref, v_ref, o_ref):
    scale = ATT_DH ** -0.5
    n = q_ref.shape[0]
    gidx = lax.broadcasted_iota(jnp.int32, (ATT_G * n, 1), 0) // n
    for h in range(ATT_HKV):
        kh = k_ref[:, h * ATT_DH:(h + 1) * ATT_DH].astype(BF16)
        vh = v_ref[:, h * ATT_DH:(h + 1) * ATT_DH].astype(BF16)
        heads = [h * ATT_G + g for g in range(ATT_G)]
        qg = jnp.concatenate([q_ref[:, hq * ATT_DH:(hq + 1) * ATT_DH] for hq in heads], axis=0)
        qg = (qg * scale).astype(BF16)
        sink = jnp.zeros((ATT_G * n, 1), F32)
        for g, hq in enumerate(heads):
            sink = jnp.where(gidx == g, sink_ref[hq], sink)
        o = _softmax_pv([(_dot_nt(qg, kh), vh)], sink)
        for g, hq in enumerate(heads):
            o_ref[:, hq * ATT_DH:(hq + 1) * ATT_DH] = o[g * n:(g + 1) * n, :]


def _attn_context(z, sink, nseq, seq_len):
    hq_w = ATT_HQ * ATT_DH
    kv_w = ATT_HKV * ATT_DH
    return pl.pallas_call(
        _attn_ctx_kernel,
        grid=(nseq,),
        in_specs=[
            pl.BlockSpec(memory_space=pltpu.SMEM),
            pl.BlockSpec((seq_len, hq_w), lambda b: (b, 0)),
            pl.BlockSpec((seq_len, kv_w), lambda b: (b, hq_w // kv_w)),
            pl.BlockSpec((seq_len, kv_w), lambda b: (b, hq_w // kv_w + 1)),
        ],
        out_specs=pl.BlockSpec((seq_len, hq_w), lambda b: (b, 0)),
        out_shape=jax.ShapeDtypeStruct((nseq * seq_len, hq_w), F32),
        compiler_params=_params(("parallel",)),
        name="attention_context",
    )(sink, z, z, z)


def _rope_axial(x, cos, sin):
    w = x.shape[1]
    lane = lax.broadcasted_iota(jnp.int32, (x.shape[0], LANES), 1)
    first = (lane % 32) < 16
    outs = []
    for cg in range(w // LANES):
        xg = x[:, cg * LANES:(cg + 1) * LANES]
        up = pltpu.roll(xg, LANES - 16, 1)
        dn = pltpu.roll(xg, 16, 1)
        outs.append(xg * cos + jnp.where(first, up, dn) * sin)
    return jnp.concatenate(outs, axis=1) if len(outs) > 1 else outs[0]


def _attn_lat_kernel(sink_ref, q_ref, kp_ref, kc_ref, kn_ref, vp_ref, vc_ref, vn_ref, ck_ref, cv_ref,
                     cq_ref, sq_ref, cp_ref, sp_ref, cn_ref, sn_ref, o_ref, *, nb):
    scale = ATT_DH ** -0.5
    qb = pl.program_id(1)
    q = _rope_axial(q_ref[...], cq_ref[...], sq_ref[...]) * scale
    kw = jnp.concatenate([
        _rope_axial(kp_ref[...], cp_ref[...], sp_ref[...]),
        _rope_axial(kc_ref[...], cq_ref[...], sq_ref[...]),
        _rope_axial(kn_ref[...], cn_ref[...], sn_ref[...]),
    ], axis=0)
    vw = jnp.concatenate([vp_ref[...], vc_ref[...], vn_ref[...]], axis=0)
    ck = ck_ref[...]
    cv = cv_ref[...]

    rows = ATT_G * ATT_BLK
    i = lax.broadcasted_iota(jnp.int32, (rows, 3 * ATT_BLK), 0) % ATT_BLK
    jj = lax.broadcasted_iota(jnp.int32, (rows, 3 * ATT_BLK), 1)
    lo = jnp.where(qb > 0, 0, ATT_BLK)
    hi = jnp.where(qb < nb - 1, 3 * ATT_BLK, 2 * ATT_BLK)
    valid = (jj >= jnp.maximum(i, lo)) & (jj < jnp.minimum(i + 2 * ATT_BLK + 1, hi))
    gidx = lax.broadcasted_iota(jnp.int32, (rows, 1), 0) // ATT_BLK

    for h in range(ATT_HKV):
        hs = slice(h * ATT_DH, (h + 1) * ATT_DH)
        kh = kw[:, hs].astype(BF16)
        vh = vw[:, hs].astype(BF16)
        ckh = ck[:, hs].astype(BF16)
        cvh = cv[:, hs].astype(BF16)
        heads = [h * ATT_G + g for g in range(ATT_G)]
        qg = jnp.concatenate([q[:, hq * ATT_DH:(hq + 1) * ATT_DH] for hq in heads], axis=0).astype(BF16)
        sink = jnp.zeros((rows, 1), F32)
        for g, hq in enumerate(heads):
            sink = jnp.where(gidx == g, sink_ref[hq], sink)
        s_loc = jnp.where(valid, _dot_nt(qg, kh), NEG_INF)
        s_ctx = _dot_nt(qg, ckh)
        o = _softmax_pv([(s_loc, vh), (s_ctx, cvh)], sink)
        for g, hq in enumerate(heads):
            o_ref[:, hq * ATT_DH:(hq + 1) * ATT_DH] = o[g * ATT_BLK:(g + 1) * ATT_BLK, :]


def _attn_latent(z, sink, cache_k, cache_v, rope, row0, nseq, seq_len, j):
    nb = seq_len // ATT_BLK
    base = row0 // ATT_BLK
    hq_w = ATT_HQ * ATT_DH
    kv_w = ATT_HKV * ATT_DH
    kcol = hq_w // kv_w
    past = cache_k.shape[2]
    cos, sin = rope

    def prv(q):
        return jnp.maximum(q - 1, 0)

    def nxt(q):
        return jnp.minimum(q + 1, nb - 1)

    def kv_spec(sel, col):
        return pl.BlockSpec((ATT_BLK, kv_w), lambda b, q: (base + b * nb + sel(q), col))

    def rope_spec(sel):
        return pl.BlockSpec((ATT_BLK, LANES), lambda b, q: (sel(q), 0))

    same = lambda q: q
    cache_spec = pl.BlockSpec((None, None, past, kv_w), lambda b, q: (b, j, 0, 0))
    return pl.pallas_call(
        functools.partial(_attn_lat_kernel, nb=nb),
        grid=(nseq, nb),
        in_specs=[
            pl.BlockSpec(memory_space=pltpu.SMEM),
            pl.BlockSpec((ATT_BLK, hq_w), lambda b, q: (base + b * nb + q, 0)),
            kv_spec(prv, kcol), kv_spec(same, kcol), kv_spec(nxt, kcol),
            kv_spec(prv, kcol + 1), kv_spec(same, kcol + 1), kv_spec(nxt, kcol + 1),
            cache_spec, cache_spec,
            rope_spec(same), rope_spec(same), rope_spec(prv), rope_spec(prv), rope_spec(nxt), rope_spec(nxt),
        ],
        out_specs=pl.BlockSpec((ATT_BLK, hq_w), lambda b, q: (b * nb + q, 0)),
        out_shape=jax.ShapeDtypeStruct((nseq * seq_len, hq_w), F32),
        compiler_params=_params(("parallel", "parallel")),
        name="attention_latent",
    )(sink, z, z, z, z, z, z, z, cache_k, cache_v, cos, sin, cos, sin, cos, sin)


def _attn_out_kernel(op_ref, os_ref, x_ref, m_ref, w_ref, lg_ref, lb_ref, out_ref, *, n_ctx_tiles):
    is_ctx = pl.program_id(0) < n_ctx_tiles
    o = jnp.where(is_ctx, op_ref[...], os_ref[...])
    y = _dot(o.astype(BF16), w_ref[...])
    gate = m_ref[2:3, :]
    r = DEEPNORM_ALPHA * x_ref[...] + gate * y
    out_ref[...] = _layer_norm(r, lg_ref[...], lb_ref[...])


def _attn_out(o_ctx, o_lat, x, mod, w_bf, ln_g, ln_b, layer, grp):
    nt, d = x.shape
    tm = OUT_TM
    k = o_ctx.shape[1]
    n_ctx = o_ctx.shape[0] // tm
    return pl.pallas_call(
        functools.partial(_attn_out_kernel, n_ctx_tiles=n_ctx),
        grid=(nt // tm,),
        in_specs=[
            pl.BlockSpec((tm, k), lambda b: (jnp.minimum(b, n_ctx - 1), 0)),
            pl.BlockSpec((tm, k), lambda b: (jnp.maximum(b - n_ctx, 0), 0)),
            pl.BlockSpec((tm, d), lambda b: (b, 0)),
            pl.BlockSpec((None, None, 6, d), lambda b: (layer, grp(b, tm), 0, 0)),
            pl.BlockSpec((k, d), lambda b: (0, 0)),
            pl.BlockSpec((None, 1, d), lambda b: (2 * layer, 0, 0)),
            pl.BlockSpec((None, 1, d), lambda b: (2 * layer, 0, 0)),
        ],
        out_specs=pl.BlockSpec((tm, d), lambda b: (b, 0)),
        out_shape=jax.ShapeDtypeStruct((nt, d), F32),
        compiler_params=_params(("parallel",)),
        name="attention_out",
    )(o_ctx, o_lat, x, mod, w_bf, ln_g, ln_b)


def _sort_network(n):
    pairs = []

    def merge(lo, hi, r):
        step = r * 2
        if step < hi - lo:
            merge(lo, hi, step)
            merge(lo + r, hi, step)
            for i in range(lo + r, hi - r, step):
                pairs.append((i, i + r))
        else:
            pairs.append((lo, lo + r))

    def sort(lo, hi):
        if hi - lo >= 1:
            mid = lo + (hi - lo) // 2
            sort(lo, mid)
            sort(mid + 1, hi)
            merge(lo, hi, 1)

    sort(0, n - 1)
    return pairs


def _pop_top(lists, k):
    vals = []
    for r in range(k):
        head = lists[0]
        m = jnp.max(head, axis=0, keepdims=True)
        vals.append(m)
        left = k - 1 - r
        if left > 0:
            took = head >= m
            nxt = [lists[i + 1] if i + 1 < len(lists) else NEG_INF for i in range(left)]
            lists = [jnp.where(took, nxt[i], lists[i]) for i in range(left)]
    return vals


def _top_values(s, k):
    tiles = [s[8 * j:8 * (j + 1), :] for j in range(s.shape[0] // 8)]
    for i, j in _sort_network(len(tiles)):
        tiles[i], tiles[j] = jnp.maximum(tiles[i], tiles[j]), jnp.minimum(tiles[i], tiles[j])
    return _pop_top(tiles, k)


def _peer_kernel(x_ref, m_ref, wq_ref, keys_ref, u0_ref, u_ref, vt_ref, lg_ref, lb_ref, out_ref,
                 h_scr, q_scr, d1_scr, e1_scr, s2_scr, e2_scr, at0_scr, at1_scr, g_scr, acc_scr,
                 *, n_eblk):
    t = x_ref.shape[0]
    nlt = t // LANES
    e = pl.program_id(1)
    k = PEER_TOPK
    nk = PEER_NKEYS
    hk = nk // PEER_NSUB

    @pl.when(e == 0)
    def _scores():
        sh = m_ref[3:4, :]
        sc = m_ref[4:5, :]
        h_scr[...] = (x_ref[...] * (1.0 + sc) + sh).astype(BF16)
        q = _dot(h_scr[...], wq_ref[...])
        for ph in range(2 * PEER_HEADS):
            q_scr[ph] = q[:, ph * PEER_DHALF:(ph + 1) * PEER_DHALF].astype(BF16)
        acc_scr[...] = jnp.zeros(acc_scr.shape, F32)
        at0_scr[...] = _dot_nt(u0_ref[...], h_scr[...])

        def unit(uidx, carry):
            lt = uidx // PEER_HEADS
            p = uidx % PEER_HEADS
            t0 = pl.multiple_of(lt * LANES, LANES)
            s, vals = [], []
            for hh in range(2):
                sk = _dot_nt(keys_ref[p, hh], q_scr[2 * p + hh, pl.ds(t0, LANES), :])
                s.append(sk)
                vals.append(_top_values(sk, k))
            v1m = [v - vals[0][0] for v in vals[0]]
            v2m = [v - vals[1][0] for v in vals[1]]
            row = lax.broadcasted_iota(jnp.int32, (8, LANES), 0)
            base = jnp.zeros((8, LANES), F32)
            for r in range(4):
                base = jnp.where(row == r, v1m[r], base)
                base = jnp.where(row == 4 + r, v2m[r], base)
            cands = [base + jnp.where(row < 4, v2m[i], v1m[i + 4] if i + 4 < k else NEG_INF) for i in range(k)]
            tops = _pop_top(cands, k + 1)
            thr = 0.5 * (tops[k - 1] + tops[k])
            zsum = jnp.exp(tops[0])
            for r in range(1, k):
                zsum = zsum + jnp.exp(tops[r])
            s1m = jnp.where(s[0] >= vals[0][k - 1], s[0] - vals[0][0], NEG_INF)
            s2m = jnp.where(s[1] >= vals[1][k - 1], s[1] - vals[1][0], NEG_INF)
            d1 = thr - s1m
            e1 = jnp.exp(s1m) * (0.5 / zsum)
            for j in range(nk // PEER_NI1):
                d1_scr[j, p, lt] = d1[j * PEER_NI1:(j + 1) * PEER_NI1, :]
                e1_scr[j, p, lt] = e1[j * PEER_NI1:(j + 1) * PEER_NI1, :]
            s2_scr[p, lt] = s2m
            e2_scr[p, lt] = jnp.exp(s2m)
            return carry

        lax.fori_loop(0, nlt * PEER_HEADS, unit, 0, unroll=8)

    def step(at_r, at_w):
        mxu_w = 2 * LANES
        for ts_i in range(t // mxu_w):
            ts = slice(ts_i * mxu_w, (ts_i + 1) * mxu_w)
            at_w[:, ts] = _dot_nt(u_ref[...], h_scr[ts, :])
            for ig in range(len(PEER_ILG)):
                il_lo = sum(PEER_ILG[:ig])
                il_hi = il_lo + PEER_ILG[ig]
                for lt in range(ts_i * mxu_w // LANES, (ts_i + 1) * mxu_w // LANES):
                    tl = slice(lt * LANES, (lt + 1) * LANES)
                    for hf in range(PEER_NSUB):
                        rows2 = slice(hf * hk, (hf + 1) * hk)
                        ils = range(il_lo, il_hi)
                        w = {il: jnp.zeros((hk, LANES), F32) for il in ils}
                        for p in range(PEER_HEADS):
                            s2h = s2_scr[p, lt, rows2, :]
                            e2h = e2_scr[p, lt, rows2, :]
                            for il in ils:
                                d1 = d1_scr[e, p, lt, il:il + 1, :]
                                e1 = e1_scr[e, p, lt, il:il + 1, :]
                                w[il] = w[il] + jnp.where(s2h >= d1, e2h, 0.0) * e1
                        for il in ils:
                            rows = slice(il * nk + hf * hk, il * nk + (hf + 1) * hk)
                            a = at_r[rows, tl]
                            u = a * (a * a * (GELU_C * 0.044715) + GELU_C)
                            g_scr[rows, tl] = (w[il] * (a + a * jnp.tanh(u))).astype(BF16)
                ks = slice(il_lo * nk, il_hi * nk)
                acc_scr[:, ts] += _dot(vt_ref[:, ks], g_scr[ks, ts])

    @pl.when(e % 2 == 0)
    def _even():
        step(at0_scr, at1_scr)

    @pl.when(e % 2 == 1)
    def _odd():
        step(at1_scr, at0_scr)

    @pl.when(e == n_eblk - 1)
    def _finish():
        y = acc_scr[...].T
        gate = m_ref[5:6, :]
        r = DEEPNORM_ALPHA * x_ref[...] + gate * y
        out_ref[...] = _layer_norm(r, lg_ref[...], lb_ref[...])


def _peer(x, mod, wq_bf, keys_bf, u_bf, vt_bf, ln_g, ln_b, layer, grp):
    nt, d = x.shape
    t = PEER_T
    nlt = t // LANES
    eblk = PEER_NI1 * PEER_NKEYS
    n_eblk = u_bf.shape[0] // eblk
    nq = wq_bf.shape[1]
    sel_shape = (PEER_HEADS, nlt, PEER_NKEYS, LANES)
    assert PEER_NI1 % 8 == 0
    blk_shape = (n_eblk, PEER_HEADS, nlt, PEER_NI1, LANES)
    return pl.pallas_call(
        functools.partial(_peer_kernel, n_eblk=n_eblk),
        grid=(nt // t, n_eblk),
        in_specs=[
            pl.BlockSpec((t, d), lambda b, e: (b, 0)),
            pl.BlockSpec((None, None, 6, d), lambda b, e: (layer, grp(b, t), 0, 0)),
            pl.BlockSpec((d, nq), lambda b, e: (0, 0)),
            pl.BlockSpec((PEER_HEADS, 2, PEER_NKEYS, PEER_DHALF), lambda b, e: (0, 0, 0, 0)),
            pl.BlockSpec((eblk, d), lambda b, e: (0, 0)),
            pl.BlockSpec((eblk, d), lambda b, e: (jnp.minimum(e + 1, n_eblk - 1), 0)),
            pl.BlockSpec((d, eblk), lambda b, e: (0, e)),
            pl.BlockSpec((None, 1, d), lambda b, e: (2 * layer + 1, 0, 0)),
            pl.BlockSpec((None, 1, d), lambda b, e: (2 * layer + 1, 0, 0)),
        ],
        out_specs=pl.BlockSpec((t, d), lambda b, e: (b, 0)),
        out_shape=jax.ShapeDtypeStruct((nt, d), F32),
        scratch_shapes=[
            pltpu.VMEM((t, d), BF16),
            pltpu.VMEM((2 * PEER_HEADS, t, PEER_DHALF), BF16),
            pltpu.VMEM(blk_shape, F32), pltpu.VMEM(blk_shape, F32),
            pltpu.VMEM(sel_shape, F32), pltpu.VMEM(sel_shape, F32),
            pltpu.VMEM((eblk, t), F32), pltpu.VMEM((eblk, t), F32),
            pltpu.VMEM((eblk, t), BF16),
            pltpu.VMEM((d, t), F32),
        ],
        compiler_params=_params(("parallel", "arbitrary")),
        name="peer",
    )(x, mod, wq_bf, keys_bf, u_bf, u_bf, vt_bf, ln_g, ln_b)


def _rope_angles(pos, dim):
    inv = ROPE_BASE ** (-jnp.arange(0, dim, 2, dtype=F32) / dim)
    ang = pos.astype(F32)[:, None] * inv[None, :]
    return jnp.cos(ang), jnp.sin(ang)


def _axial_tables(seq_len):
    t = jnp.arange(seq_len)
    half = ATT_DH // 2
    cr, sr = _rope_angles(t // GRID_W, half)
    cc, sc = _rope_angles(t % GRID_W, half)
    cos = jnp.concatenate([cr, cr, cc, cc], -1)
    sin = jnp.concatenate([-sr, sr, -sc, sc], -1)
    reps = LANES // ATT_DH
    return jnp.tile(cos, (1, reps)), jnp.tile(sin, (1, reps))


def kernel(x_prompt, x_sample, state_ret_fwd, state_ret_bwd, cache_k, cache_v, c, c_ctx, mod_w, mod_b, ln_g, ln_b,
           ret_w_in, ret_w_out, ret_decay, attn_w_in, attn_w_out, attn_sink, peer_wq, peer_keys, peer_u, peer_v):
    nb_p, len_p, d = x_prompt.shape
    nb_s, len_s, _ = x_sample.shape
    n_p = nb_p * len_p
    n_s = nb_s * len_s
    for tile in (PROJ_TM, OUT_TM, PEER_T):
        assert n_p % tile == 0 and len_s % tile == 0

    def grp(blk, tile):
        return _group_index(blk, n_p // tile, len_s // tile)

    x = jnp.concatenate([x_prompt.reshape(n_p, d), x_sample.reshape(n_s, d)], axis=0)
    n_grp = 1 + nb_s
    n_grp_pad = -(-n_grp // 8) * 8
    cond = jnp.concatenate([c_ctx[None, :], c, jnp.zeros((n_grp_pad - n_grp, d), F32)], axis=0)
    mod = _modulation(cond, mod_w, mod_b).reshape(DEPTH, n_grp_pad, 6, d)

    ln_g3 = ln_g.reshape(DEPTH * 2, 1, d)
    ln_b3 = ln_b.reshape(DEPTH * 2, 1, d)
    ret_rope = _rope_angles(jnp.arange(len_s), RET_DK)
    att_rope = _axial_tables(len_s)
    kv_w = ATT_HKV * ATT_DH
    cache_k4 = cache_k.reshape(cache_k.shape[0], cache_k.shape[1], cache_k.shape[2], kv_w)
    cache_v4 = cache_v.reshape(cache_v.shape[0], cache_v.shape[1], cache_v.shape[2], kv_w)

    new_sf, new_sb, new_k, new_v = [], [], [], []
    for i in range(DEPTH):
        j = i // N_MIXERS
        if i % N_MIXERS == 0:
            z = _project(x, mod, ret_w_in[j].astype(BF16), i, 0, grp)
            tables = _ret_tables(ret_decay[j])
            o_p, sf, sb = _retention(z, tables, 0, nb_p, len_p, False)
            o_s = _retention(z, tables, n_p, nb_s, len_s, True, rope=ret_rope,
                             states=(state_ret_fwd, state_ret_bwd), j=j)
            new_sf.append(sf)
            new_sb.append(sb)
            x = _ret_out(o_p, o_s, z, x, mod, ret_w_out[j].astype(BF16), ln_g3, ln_b3, i, grp)
        else:
            z = _project(x, mod, attn_w_in[j].astype(BF16), i, 0, grp)
            sink = attn_sink[j].astype(F32)
            o_p = _attn_context(z, sink, nb_p, len_p)
            o_s = _attn_latent(z, sink, cache_k4, cache_v4, att_rope, n_p, nb_s, len_s, j)
            hq_w = ATT_HQ * ATT_DH
            new_k.append(z[:n_p, hq_w:hq_w + kv_w].reshape(nb_p, len_p, ATT_HKV, ATT_DH))
            new_v.append(z[:n_p, hq_w + kv_w:hq_w + 2 * kv_w].reshape(nb_p, len_p, ATT_HKV, ATT_DH))
            x = _attn_out(o_p, o_s, x, mod, attn_w_out[j].astype(BF16), ln_g3, ln_b3, i, grp)
        x = _peer(x, mod, peer_wq[i].astype(BF16), peer_keys[i].astype(BF16), peer_u[i].astype(BF16),
                  peer_v[i].T.astype(BF16), ln_g3, ln_b3, i, grp)

    y_p = x[:n_p].reshape(nb_p, len_p, d)
    y_s = x[n_p:].reshape(nb_s, len_s, d)
    return (y_p, y_s, jnp.stack(new_sf, 1), jnp.stack(new_sb, 1), jnp.stack(new_k, 1), jnp.stack(new_v, 1))
```

```python
import functools
import math

import jax
import jax.numpy as jnp
from jax import lax
from jax.experimental import pallas as pl
from jax.experimental.pallas import tpu as pltpu

F32 = jnp.float32
BF16 = jnp.bfloat16

DEPTH = 4
N_MIXERS = 2
GRID_W = 64
RET_HEADS = 4
RET_DK = 256
RET_DV = 512
RET_CHUNK = 128
RET_SUB = 8
ATT_HQ = 16
ATT_HKV = 4
ATT_G = ATT_HQ // ATT_HKV
ATT_DH = 64
ATT_BLK = 128
ROPE_BASE = 10000.0
PEER_HEADS = 8
PEER_NKEYS = 128
PEER_DHALF = 128
PEER_TOPK = 16
DEEPNORM_ALPHA = (2.0 * DEPTH) ** 0.25
LN_EPS = 1e-5
GN_EPS = 1e-5
NEG_INF = -1e30
GELU_C = math.sqrt(2.0 / math.pi)

LANES = 128
VMEM_LIMIT = 56 * 1024 * 1024
PROJ_TM = 512
PROJ_TN = 3072
OUT_TM = 512
PEER_T = 512
PEER_NI1 = 8
PEER_ILG = (4, 2, 2)
PEER_NSUB = 2


def _dot(a, b):
    return jnp.dot(a, b, preferred_element_type=F32)


def _dot_nt(a, b):
    return lax.dot_general(a, b, (((1,), (1,)), ((), ())), preferred_element_type=F32)


def _params(sem, flags=None):
    return pltpu.CompilerParams(dimension_semantics=sem, vmem_limit_bytes=VMEM_LIMIT, flags=flags)


def _group_index(blk, n_prompt_blocks, blocks_per_seq):
    return jnp.where(blk < n_prompt_blocks, 0, 1 + (blk - n_prompt_blocks) // blocks_per_seq)


def _layer_norm(r, g, b):
    mu = jnp.mean(r, -1, keepdims=True)
    d = r - mu
    var = jnp.mean(d * d, -1, keepdims=True)
    return d * lax.rsqrt(var + LN_EPS) * g + b


def _mod_kernel(c_ref, w_ref, b_ref, o_ref):
    c = c_ref[...]
    a = (c * jax.nn.sigmoid(c)).astype(BF16)
    o_ref[...] = _dot(a, w_ref[...].astype(BF16)) + b_ref[...]


def _modulation(cond, mod_w, mod_b):
    ngp, d = cond.shape
    n_out = mod_w.shape[-1]
    tn = 1536
    return pl.pallas_call(
        _mod_kernel,
        grid=(DEPTH, n_out // tn),
        in_specs=[
            pl.BlockSpec((ngp, d), lambda i, n: (0, 0)),
            pl.BlockSpec((None, d, tn), lambda i, n: (i, 0, n)),
            pl.BlockSpec((None, 1, tn), lambda i, n: (i, 0, n)),
        ],
        out_specs=pl.BlockSpec((None, ngp, tn), lambda i, n: (i, 0, n)),
        out_shape=jax.ShapeDtypeStruct((DEPTH, ngp, n_out), F32),
        compiler_params=_params(("parallel", "parallel")),
        name="modulation",
    )(cond, mod_w, mod_b.reshape(DEPTH, 1, n_out))


def _proj_kernel(x_ref, m_ref, w_ref, o_ref, *, off):
    sh = m_ref[off:off + 1, :]
    sc = m_ref[off + 1:off + 2, :]
    h = (x_ref[...] * (1.0 + sc) + sh).astype(BF16)
    o_ref[...] = _dot(h, w_ref[...])


def _project(x, mod, w_bf, layer, off, grp):
    nt, d = x.shape
    n = w_bf.shape[1]
    tn = PROJ_TN if n % PROJ_TN == 0 else n
    tm = PROJ_TM
    return pl.pallas_call(
        functools.partial(_proj_kernel, off=off),
        grid=(n // tn, nt // tm),
        in_specs=[
            pl.BlockSpec((tm, d), lambda j, b: (b, 0)),
            pl.BlockSpec((None, None, 6, d), lambda j, b: (layer, grp(b, tm), 0, 0)),
            pl.BlockSpec((d, tn), lambda j, b: (0, j)),
        ],
        out_specs=pl.BlockSpec((tm, tn), lambda j, b: (b, j)),
        out_shape=jax.ShapeDtypeStruct((nt, n), F32),
        compiler_params=_params(("parallel", "parallel")),
        name="mod_proj",
    )(x, mod, w_bf)


def _rope_half(x, cos, sin):
    x1 = x[:, :LANES]
    x2 = x[:, LANES:]
    return jnp.concatenate([x1 * cos - x2 * sin, x1 * sin + x2 * cos], axis=1)


def _ret_kernel(*refs, latent, nc, n_sub):
    it = iter(refs)
    qa, ka, va, qb, kb, vb = (next(it) for _ in range(6))
    if latent:
        cosa, sina, cosb, sinb = (next(it) for _ in range(4))
    dfb, qdf, qdb, kdf, kdb, cdf, cdb = (next(it) for _ in range(7))
    if latent:
        s0f, s0b = next(it), next(it)
    o_ref = next(it)
    if not latent:
        sf_out, sb_out = next(it), next(it)
    sf, sb = next(it), next(it)

    c = pl.program_id(2)

    @pl.when(c == 0)
    def _init():
        o_ref[...] = jnp.zeros(o_ref.shape, F32)
        if latent:
            sf[...] = s0f[...]
            sb[...] = s0b[...]
        else:
            sf[...] = jnp.zeros(sf.shape, F32)
            sb[...] = jnp.zeros(sb.shape, F32)

    def prep(q, k, cos_ref, sin_ref, rows):
        k = k * (RET_DK ** -0.5)
        if latent:
            cos = cos_ref[rows, :]
            sin = sin_ref[rows, :]
            q = _rope_half(q, cos, sin)
            k = _rope_half(k, cos, sin)
        return q, k

    for sub in range(n_sub):
        ca = c * n_sub + sub
        ra = slice(sub * RET_CHUNK, (sub + 1) * RET_CHUNK)
        rb = slice((n_sub - 1 - sub) * RET_CHUNK, (n_sub - sub) * RET_CHUNK)

        q, k = prep(qa[ra, :], ka[ra, :], cosa if latent else None, sina if latent else None, ra)
        v = va[ra, :].astype(BF16)
        att = (_dot_nt(q.astype(BF16), k.astype(BF16)) * dfb[...]).astype(BF16)
        o_c = _dot(att, v) + _dot((q * qdf[...]).astype(BF16), sf[...].astype(BF16))
        rows_c = pl.ds(pl.multiple_of(ca * RET_CHUNK, RET_CHUNK), RET_CHUNK)
        o_ref[rows_c, :] += o_c
        kd_t = (k * kdf[...]).T.astype(BF16)
        sf[...] = sf[...] * cdf[...] + _dot(kd_t, v)

        q2, k2 = prep(qb[rb, :], kb[rb, :], cosb if latent else None, sinb if latent else None, rb)
        v2 = vb[rb, :].astype(BF16)
        o_b = _dot((q2 * qdb[...]).astype(BF16), sb[...].astype(BF16))
        rows_b = pl.ds(pl.multiple_of((nc - 1 - ca) * RET_CHUNK, RET_CHUNK), RET_CHUNK)
        o_ref[rows_b, :] += o_b
        kd2_t = (k2 * kdb[...]).T.astype(BF16)
        sb[...] = sb[...] * cdb[...] + _dot(kd2_t, v2)

    if not latent:
        @pl.when(c == nc // n_sub - 1)
        def _fin():
            sf_out[...] = sf[...]
            sb_out[...] = sb[...]


def _ret_tables(decay):
    log_g = jax.nn.log_sigmoid(decay.astype(F32))
    lf, lb = log_g[0], log_g[1]
    idx = jnp.arange(RET_CHUNK, dtype=F32)
    diff = idx[:, None] - idx[None, :]
    fmask = diff >= 0
    bmask = diff < 0
    df = jnp.where(fmask[None], jnp.exp(jnp.where(fmask, diff, 0.0)[None] * lf[:, None, None]), 0.0)
    db = jnp.where(bmask[None], jnp.exp(jnp.where(bmask, -diff, 0.0)[None] * lb[:, None, None]), 0.0)
    dfb = df + db
    def rows(e):
        return jnp.broadcast_to(e[:, :, None], (RET_HEADS, RET_CHUNK, RET_DK))
    qdf = rows(jnp.exp((idx + 1.0)[None, :] * lf[:, None]))
    kdf = rows(jnp.exp((RET_CHUNK - 1.0 - idx)[None, :] * lf[:, None]))
    qdb = rows(jnp.exp((RET_CHUNK - idx)[None, :] * lb[:, None]))
    kdb = rows(jnp.exp(idx[None, :] * lb[:, None]))
    cdf = jnp.broadcast_to(jnp.exp(RET_CHUNK * lf)[:, None, None], (RET_HEADS, 1, RET_DV))
    cdb = jnp.broadcast_to(jnp.exp(RET_CHUNK * lb)[:, None, None], (RET_HEADS, 1, RET_DV))
    return dfb, qdf, qdb, kdf, kdb, cdf, cdb


def _retention(z, tables, row0, nseq, seq_len, latent, rope=None, states=None, j=0):
    nc = seq_len // RET_CHUNK
    n_sub = math.gcd(RET_SUB, nc)
    blk = n_sub * RET_CHUNK
    ns = nc // n_sub
    base = row0 // blk
    h_ = RET_HEADS
    kcol = (h_ * RET_DK) // RET_DK
    vcol = (2 * h_ * RET_DK) // RET_DV

    def fw(b, h, c):
        return base + b * ns + c

    def bw(b, h, c):
        return base + b * ns + (ns - 1 - c)

    in_specs = [
        pl.BlockSpec((blk, RET_DK), lambda b, h, c: (fw(b, h, c), h)),
        pl.BlockSpec((blk, RET_DK), lambda b, h, c: (fw(b, h, c), kcol + h)),
        pl.BlockSpec((blk, RET_DV), lambda b, h, c: (fw(b, h, c), vcol + h)),
        pl.BlockSpec((blk, RET_DK), lambda b, h, c: (bw(b, h, c), h)),
        pl.BlockSpec((blk, RET_DK), lambda b, h, c: (bw(b, h, c), kcol + h)),
        pl.BlockSpec((blk, RET_DV), lambda b, h, c: (bw(b, h, c), vcol + h)),
    ]
    args = [z, z, z, z, z, z]
    if latent:
        cos, sin = rope
        in_specs += [
            pl.BlockSpec((blk, LANES), lambda b, h, c: (c, 0)),
            pl.BlockSpec((blk, LANES), lambda b, h, c: (c, 0)),
            pl.BlockSpec((blk, LANES), lambda b, h, c: (ns - 1 - c, 0)),
            pl.BlockSpec((blk, LANES), lambda b, h, c: (ns - 1 - c, 0)),
        ]
        args += [cos, sin, cos, sin]
    dfb, qdf, qdb, kdf, kdb, cdf, cdb = tables
    in_specs += [pl.BlockSpec((None, RET_CHUNK, RET_CHUNK), lambda b, h, c: (h, 0, 0))]
    in_specs += [pl.BlockSpec((None, RET_CHUNK, RET_DK), lambda b, h, c: (h, 0, 0))] * 4
    in_specs += [pl.BlockSpec((None, 1, RET_DV), lambda b, h, c: (h, 0, 0))] * 2
    args += [dfb, qdf, qdb, kdf, kdb, cdf, cdb]
    if latent:
        s0f, s0b = states
        st_spec = pl.BlockSpec((None, None, None, RET_DK, RET_DV), lambda b, h, c: (b, j, h, 0, 0))
        in_specs += [st_spec, st_spec]
        args += [s0f, s0b]

    o_shape = jax.ShapeDtypeStruct((nseq * seq_len, h_ * RET_DV), F32)
    o_spec = pl.BlockSpec((seq_len, RET_DV), lambda b, h, c: (b, h))
    if latent:
        out_shape, out_specs = o_shape, o_spec
    else:
        s_shape = jax.ShapeDtypeStruct((nseq, h_, RET_DK, RET_DV), F32)
        s_spec = pl.BlockSpec((None, None, RET_DK, RET_DV), lambda b, h, c: (b, h, 0, 0))
        out_shape, out_specs = (o_shape, s_shape, s_shape), (o_spec, s_spec, s_spec)

    return pl.pallas_call(
        functools.partial(_ret_kernel, latent=latent, nc=nc, n_sub=n_sub),
        grid=(nseq, h_, ns),
        in_specs=in_specs,
        out_specs=out_specs,
        out_shape=out_shape,
        scratch_shapes=[pltpu.VMEM((RET_DK, RET_DV), F32), pltpu.VMEM((RET_DK, RET_DV), F32)],
        compiler_params=_params(("parallel", "parallel", "arbitrary")),
        name="retention_latent" if latent else "retention_context",
    )(*args)


def _ret_out_kernel(op_ref, os_ref, g_ref, x_ref, m_ref, w_ref, lg_ref, lb_ref, out_ref, *, n_ctx_tiles):
    is_ctx = pl.program_id(0) < n_ctx_tiles
    y = None
    for h in range(RET_HEADS):
        cols = slice(h * RET_DV, (h + 1) * RET_DV)
        oh = jnp.where(is_ctx, op_ref[:, cols], os_ref[:, cols])
        mu = jnp.mean(oh, -1, keepdims=True)
        d = oh - mu
        var = jnp.mean(d * d, -1, keepdims=True)
        on = d * lax.rsqrt(var + GN_EPS)
        gh = g_ref[:, cols]
        act = (gh * jax.nn.sigmoid(gh) * on).astype(BF16)
        part = _dot(act, w_ref[cols, :])
        y = part if y is None else y + part
    gate = m_ref[2:3, :]
    r = DEEPNORM_ALPHA * x_ref[...] + gate * y
    out_ref[...] = _layer_norm(r, lg_ref[...], lb_ref[...])


def _ret_out(o_ctx, o_lat, z, x, mod, w_bf, ln_g, ln_b, layer, grp):
    nt, d = x.shape
    tm = OUT_TM
    hv = RET_HEADS * RET_DV
    gcol = (2 * RET_HEADS * RET_DK + hv) // hv
    n_ctx = o_ctx.shape[0] // tm
    return pl.pallas_call(
        functools.partial(_ret_out_kernel, n_ctx_tiles=n_ctx),
        grid=(nt // tm,),
        in_specs=[
            pl.BlockSpec((tm, hv), lambda b: (jnp.minimum(b, n_ctx - 1), 0)),
            pl.BlockSpec((tm, hv), lambda b: (jnp.maximum(b - n_ctx, 0), 0)),
            pl.BlockSpec((tm, hv), lambda b: (b, gcol)),
            pl.BlockSpec((tm, d), lambda b: (b, 0)),
            pl.BlockSpec((None, None, 6, d), lambda b: (layer, grp(b, tm), 0, 0)),
            pl.BlockSpec((hv, d), lambda b: (0, 0)),
            pl.BlockSpec((None, 1, d), lambda b: (2 * layer, 0, 0)),
            pl.BlockSpec((None, 1, d), lambda b: (2 * layer, 0, 0)),
        ],
        out_specs=pl.BlockSpec((tm, d), lambda b: (b, 0)),
        out_shape=jax.ShapeDtypeStruct((nt, d), F32),
        compiler_params=_params(("parallel",)),
        name="retention_out",
    )(o_ctx, o_lat, z, x, mod, w_bf, ln_g, ln_b)


def _softmax_pv(parts, sink):
    m = sink
    for s, _ in parts:
        m = jnp.maximum(m, jnp.max(s, -1, keepdims=True))
    den = jnp.exp(sink - m)
    o = None
    for s, v in parts:
        p = jnp.exp(s - m)
        den = den + jnp.sum(p, -1, keepdims=True)
        pv = _dot(p.astype(BF16), v)
        o = pv if o is None else o + pv
    return o / den


def _attn_ctx_kernel(sink_ref, q_ref, k_ref, v_ref, o_ref):
    scale = ATT_DH ** -0.5
    n = q_ref.shape[0]
    gidx = lax.broadcasted_iota(jnp.int32, (ATT_G * n, 1), 0) // n
    for h in range(ATT_HKV):
        kh = k_ref[:, h * ATT_DH:(h + 1) * ATT_DH].astype(BF16)
        vh = v_ref[:, h * ATT_DH:(h + 1) * ATT_DH].astype(BF16)
        heads = [h * ATT_G + g for g in range(ATT_G)]
        qg = jnp.concatenate([q_ref[:, hq * ATT_DH:(hq + 1) * ATT_DH] for hq in heads], axis=0)
        qg = (qg * scale).astype(BF16)
        sink = jnp.zeros((ATT_G * n, 1), F32)
        for g, hq in enumerate(heads):
            sink = jnp.where(gidx == g, sink_ref[hq], sink)
        o = _softmax_pv([(_dot_nt(qg, kh), vh)], sink)
        for g, hq in enumerate(heads):
            o_ref[:, hq * ATT_DH:(hq + 1) * ATT_DH] = o[g * n:(g + 1) * n, :]


def _attn_context(z, sink, nseq, seq_len):
    hq_w = ATT_HQ * ATT_DH
    kv_w = ATT_HKV * ATT_DH
    return pl.pallas_call(
        _attn_ctx_kernel,
        grid=(nseq,),
        in_specs=[
            pl.BlockSpec(memory_space=pltpu.SMEM),
            pl.BlockSpec((seq_len, hq_w), lambda b: (b, 0)),
            pl.BlockSpec((seq_len, kv_w), lambda b: (b, hq_w // kv_w)),
            pl.BlockSpec((seq_len, kv_w), lambda b: (b, hq_w // kv_w + 1)),
        ],
        out_specs=pl.BlockSpec((seq_len, hq_w), lambda b: (b, 0)),
        out_shape=jax.ShapeDtypeStruct((nseq * seq_len, hq_w), F32),
        compiler_params=_params(("parallel",)),
        name="attention_context",
    )(sink, z, z, z)


def _rope_axial(x, cos, sin):
    w = x.shape[1]
    lane = lax.broadcasted_iota(jnp.int32, (x.shape[0], LANES), 1)
    first = (lane % 32) < 16
    outs = []
    for cg in range(w // LANES):
        xg = x[:, cg * LANES:(cg + 1) * LANES]
        up = pltpu.roll(xg, LANES - 16, 1)
        dn = pltpu.roll(xg, 16, 1)
        outs.append(xg * cos + jnp.where(first, up, dn) * sin)
    return jnp.concatenate(outs, axis=1) if len(outs) > 1 else outs[0]


def _attn_lat_kernel(sink_ref, q_ref, kp_ref, kc_ref, kn_ref, vp_ref, vc_ref, vn_ref, ck_ref, cv_ref,
                     cq_ref, sq_ref, cp_ref, sp_ref, cn_ref, sn_ref, o_ref, *, nb):
    scale = ATT_DH ** -0.5
    qb = pl.program_id(1)
    q = _rope_axial(q_ref[...], cq_ref[...], sq_ref[...]) * scale
    kw = jnp.concatenate([
        _rope_axial(kp_ref[...], cp_ref[...], sp_ref[...]),
        _rope_axial(kc_ref[...], cq_ref[...], sq_ref[...]),
        _rope_axial(kn_ref[...], cn_ref[...], sn_ref[...]),
    ], axis=0)
    vw = jnp.concatenate([vp_ref[...], vc_ref[...], vn_ref[...]], axis=0)
    ck = ck_ref[...]
    cv = cv_ref[...]

    rows = ATT_G * ATT_BLK
    i = lax.broadcasted_iota(jnp.int32, (rows, 3 * ATT_BLK), 0) % ATT_BLK
    jj = lax.broadcasted_iota(jnp.int32, (rows, 3 * ATT_BLK), 1)
    lo = jnp.where(qb > 0, 0, ATT_BLK)
    hi = jnp.where(qb < nb - 1, 3 * ATT_BLK, 2 * ATT_BLK)
    valid = (jj >= jnp.maximum(i, lo)) & (jj < jnp.minimum(i + 2 * ATT_BLK + 1, hi))
    gidx = lax.broadcasted_iota(jnp.int32, (rows, 1), 0) // ATT_BLK

    for h in range(ATT_HKV):
        hs = slice(h * ATT_DH, (h + 1) * ATT_DH)
        kh = kw[:, hs].astype(BF16)
        vh = vw[:, hs].astype(BF16)
        ckh = ck[:, hs].astype(BF16)
        cvh = cv[:, hs].astype(BF16)
        heads = [h * ATT_G + g for g in range(ATT_G)]
        qg = jnp.concatenate([q[:, hq * ATT_DH:(hq + 1) * ATT_DH] for hq in heads], axis=0).astype(BF16)
        sink = jnp.zeros((rows, 1), F32)
        for g, hq in enumerate(heads):
            sink = jnp.where(gidx == g, sink_ref[hq], sink)
        s_loc = jnp.where(valid, _dot_nt(qg, kh), NEG_INF)
        s_ctx = _dot_nt(qg, ckh)
        o = _softmax_pv([(s_loc, vh), (s_ctx, cvh)], sink)
        for g, hq in enumerate(heads):
            o_ref[:, hq * ATT_DH:(hq + 1) * ATT_DH] = o[g * ATT_BLK:(g + 1) * ATT_BLK, :]


def _attn_latent(z, sink, cache_k, cache_v, rope, row0, nseq, seq_len, j):
    nb = seq_len // ATT_BLK
    base = row0 // ATT_BLK
    hq_w = ATT_HQ * ATT_DH
    kv_w = ATT_HKV * ATT_DH
    kcol = hq_w // kv_w
    past = cache_k.shape[2]
    cos, sin = rope

    def prv(q):
        return jnp.maximum(q - 1, 0)

    def nxt(q):
        return jnp.minimum(q + 1, nb - 1)

    def kv_spec(sel, col):
        return pl.BlockSpec((ATT_BLK, kv_w), lambda b, q: (base + b * nb + sel(q), col))

    def rope_spec(sel):
        return pl.BlockSpec((ATT_BLK, LANES), lambda b, q: (sel(q), 0))

    same = lambda q: q
    cache_spec = pl.BlockSpec((None, None, past, kv_w), lambda b, q: (b, j, 0, 0))
    return pl.pallas_call(
        functools.partial(_attn_lat_kernel, nb=nb),
        grid=(nseq, nb),
        in_specs=[
            pl.BlockSpec(memory_space=pltpu.SMEM),
            pl.BlockSpec((ATT_BLK, hq_w), lambda b, q: (base + b * nb + q, 0)),
            kv_spec(prv, kcol), kv_spec(same, kcol), kv_spec(nxt, kcol),
            kv_spec(prv, kcol + 1), kv_spec(same, kcol + 1), kv_spec(nxt, kcol + 1),
            cache_spec, cache_spec,
            rope_spec(same), rope_spec(same), rope_spec(prv), rope_spec(prv), rope_spec(nxt), rope_spec(nxt),
        ],
        out_specs=pl.BlockSpec((ATT_BLK, hq_w), lambda b, q: (b * nb + q, 0)),
        out_shape=jax.ShapeDtypeStruct((nseq * seq_len, hq_w), F32),
        compiler_params=_params(("parallel", "parallel")),
        name="attention_latent",
    )(sink, z, z, z, z, z, z, z, cache_k, cache_v, cos, sin, cos, sin, cos, sin)


def _attn_out_kernel(op_ref, os_ref, x_ref, m_ref, w_ref, lg_ref, lb_ref, out_ref, *, n_ctx_tiles):
    is_ctx = pl.program_id(0) < n_ctx_tiles
    o = jnp.where(is_ctx, op_ref[...], os_ref[...])
    y = _dot(o.astype(BF16), w_ref[...])
    gate = m_ref[2:3, :]
    r = DEEPNORM_ALPHA * x_ref[...] + gate * y
    out_ref[...] = _layer_norm(r, lg_ref[...], lb_ref[...])


def _attn_out(o_ctx, o_lat, x, mod, w_bf, ln_g, ln_b, layer, grp):
    nt, d = x.shape
    tm = OUT_TM
    k = o_ctx.shape[1]
    n_ctx = o_ctx.shape[0] // tm
    return pl.pallas_call(
        functools.partial(_attn_out_kernel, n_ctx_tiles=n_ctx),
        grid=(nt // tm,),
        in_specs=[
            pl.BlockSpec((tm, k), lambda b: (jnp.minimum(b, n_ctx - 1), 0)),
            pl.BlockSpec((tm, k), lambda b: (jnp.maximum(b - n_ctx, 0), 0)),
            pl.BlockSpec((tm, d), lambda b: (b, 0)),
            pl.BlockSpec((None, None, 6, d), lambda b: (layer, grp(b, tm), 0, 0)),
            pl.BlockSpec((k, d), lambda b: (0, 0)),
            pl.BlockSpec((None, 1, d), lambda b: (2 * layer, 0, 0)),
            pl.BlockSpec((None, 1, d), lambda b: (2 * layer, 0, 0)),
        ],
        out_specs=pl.BlockSpec((tm, d), lambda b: (b, 0)),
        out_shape=jax.ShapeDtypeStruct((nt, d), F32),
        compiler_params=_params(("parallel",)),
        name="attention_out",
    )(o_ctx, o_lat, x, mod, w_bf, ln_g, ln_b)


def _sort_network(n):
    pairs = []

    def merge(lo, hi, r):
        step = r * 2
        if step < hi - lo:
            merge(lo, hi, step)
            merge(lo + r, hi, step)
            for i in range(lo + r, hi - r, step):
                pairs.append((i, i + r))
        else:
            pairs.append((lo, lo + r))

    def sort(lo, hi):
        if hi - lo >= 1:
            mid = lo + (hi - lo) // 2
            sort(lo, mid)
            sort(mid + 1, hi)
            merge(lo, hi, 1)

    sort(0, n - 1)
    return pairs


def _pop_top(lists, k):
    vals = []
    for r in range(k):
        head = lists[0]
        m = jnp.max(head, axis=0, keepdims=True)
        vals.append(m)
        left = k - 1 - r
        if left > 0:
            took = head >= m
            nxt = [lists[i + 1] if i + 1 < len(lists) else NEG_INF for i in range(left)]
            lists = [jnp.where(took, nxt[i], lists[i]) for i in range(left)]
    return vals


def _top_values(s, k):
    tiles = [s[8 * j:8 * (j + 1), :] for j in range(s.shape[0] // 8)]
    for i, j in _sort_network(len(tiles)):
        tiles[i], tiles[j] = jnp.maximum(tiles[i], tiles[j]), jnp.minimum(tiles[i], tiles[j])
    return _pop_top(tiles, k)


def _peer_kernel(x_ref, m_ref, wq_ref, keys_ref, u0_ref, u_ref, vt_ref, lg_ref, lb_ref, out_ref,
                 h_scr, q_scr, d1_scr, e1_scr, s2_scr, e2_scr, at0_scr, at1_scr, g_scr, acc_scr,
                 *, n_eblk):
    t = x_ref.shape[0]
    nlt = t // LANES
    e = pl.program_id(1)
    k = PEER_TOPK
    nk = PEER_NKEYS
    hk = nk // PEER_NSUB

    @pl.when(e == 0)
    def _scores():
        sh = m_ref[3:4, :]
        sc = m_ref[4:5, :]
        h_scr[...] = (x_ref[...] * (1.0 + sc) + sh).astype(BF16)
        q = _dot(h_scr[...], wq_ref[...])
        for ph in range(2 * PEER_HEADS):
            q_scr[ph] = q[:, ph * PEER_DHALF:(ph + 1) * PEER_DHALF].astype(BF16)
        acc_scr[...] = jnp.zeros(acc_scr.shape, F32)
        at0_scr[...] = _dot_nt(u0_ref[...], h_scr[...])

        def unit(uidx, carry):
            lt = uidx // PEER_HEADS
            p = uidx % PEER_HEADS
            t0 = pl.multiple_of(lt * LANES, LANES)
            s, vals = [], []
            for hh in range(2):
                sk = _dot_nt(keys_ref[p, hh], q_scr[2 * p + hh, pl.ds(t0, LANES), :])
                s.append(sk)
                vals.append(_top_values(sk, k))
            v1m = [v - vals[0][0] for v in vals[0]]
            v2m = [v - vals[1][0] for v in vals[1]]
            row = lax.broadcasted_iota(jnp.int32, (8, LANES), 0)
            base = jnp.zeros((8, LANES), F32)
            for r in range(4):
                base = jnp.where(row == r, v1m[r], base)
                base = jnp.where(row == 4 + r, v2m[r], base)
            cands = [base + jnp.where(row < 4, v2m[i], v1m[i + 4] if i + 4 < k else NEG_INF) for i in range(k)]
            tops = _pop_top(cands, k + 1)
            thr = 0.5 * (tops[k - 1] + tops[k])
            zsum = jnp.exp(tops[0])
            for r in range(1, k):
                zsum = zsum + jnp.exp(tops[r])
            s1m = jnp.where(s[0] >= vals[0][k - 1], s[0] - vals[0][0], NEG_INF)
            s2m = jnp.where(s[1] >= vals[1][k - 1], s[1] - vals[1][0], NEG_INF)
            d1 = thr - s1m
            e1 = jnp.exp(s1m) * (0.5 / zsum)
            for j in range(nk // PEER_NI1):
                d1_scr[j, p, lt] = d1[j * PEER_NI1:(j + 1) * PEER_NI1, :]
                e1_scr[j, p, lt] = e1[j * PEER_NI1:(j + 1) * PEER_NI1, :]
            s2_scr[p, lt] = s2m
            e2_scr[p, lt] = jnp.exp(s2m)
            return carry

        lax.fori_loop(0, nlt * PEER_HEADS, unit, 0, unroll=8)

    def step(at_r, at_w):
        mxu_w = 2 * LANES
        for ts_i in range(t // mxu_w):
            ts = slice(ts_i * mxu_w, (ts_i + 1) * mxu_w)
            at_w[:, ts] = _dot_nt(u_ref[...], h_scr[ts, :])
            for ig in range(len(PEER_ILG)):
                il_lo = sum(PEER_ILG[:ig])
                il_hi = il_lo + PEER_ILG[ig]
                for lt in range(ts_i * mxu_w // LANES, (ts_i + 1) * mxu_w // LANES):
                    tl = slice(lt * LANES, (lt + 1) * LANES)
                    for hf in range(PEER_NSUB):
                        rows2 = slice(hf * hk, (hf + 1) * hk)
                        ils = range(il_lo, il_hi)
                        w = {il: jnp.zeros((hk, LANES), F32) for il in ils}
                        for p in range(PEER_HEADS):
                            s2h = s2_scr[p, lt, rows2, :]
                            e2h = e2_scr[p, lt, rows2, :]
                            for il in ils:
                                d1 = d1_scr[e, p, lt, il:il + 1, :]
                                e1 = e1_scr[e, p, lt, il:il + 1, :]
                                w[il] = w[il] + jnp.where(s2h >= d1, e2h, 0.0) * e1
                        for il in ils:
                            rows = slice(il * nk + hf * hk, il * nk + (hf + 1) * hk)
                            a = at_r[rows, tl]
                            u = a * (a * a * (GELU_C * 0.044715) + GELU_C)
                            g_scr[rows, tl] = (w[il] * (a + a * jnp.tanh(u))).astype(BF16)
                ks = slice(il_lo * nk, il_hi * nk)
                acc_scr[:, ts] += _dot(vt_ref[:, ks], g_scr[ks, ts])

    @pl.when(e % 2 == 0)
    def _even():
        step(at0_scr, at1_scr)

    @pl.when(e % 2 == 1)
    def _odd():
        step(at1_scr, at0_scr)

    @pl.when(e == n_eblk - 1)
    def _finish():
        y = acc_scr[...].T
        gate = m_ref[5:6, :]
        r = DEEPNORM_ALPHA * x_ref[...] + gate * y
        out_ref[...] = _layer_norm(r, lg_ref[...], lb_ref[...])


def _peer(x, mod, wq_bf, keys_bf, u_bf, vt_bf, ln_g, ln_b, layer, grp):
    nt, d = x.shape
    t = PEER_T
    nlt = t // LANES
    eblk = PEER_NI1 * PEER_NKEYS
    n_eblk = u_bf.shape[0] // eblk
    nq = wq_bf.shape[1]
    sel_shape = (PEER_HEADS, nlt, PEER_NKEYS, LANES)
    assert PEER_NI1 % 8 == 0
    blk_shape = (n_eblk, PEER_HEADS, nlt, PEER_NI1, LANES)
    return pl.pallas_call(
        functools.partial(_peer_kernel, n_eblk=n_eblk),
        grid=(nt // t, n_eblk),
        in_specs=[
            pl.BlockSpec((t, d), lambda b, e: (b, 0)),
            pl.BlockSpec((None, None, 6, d), lambda b, e: (layer, grp(b, t), 0, 0)),
            pl.BlockSpec((d, nq), lambda b, e: (0, 0)),
            pl.BlockSpec((PEER_HEADS, 2, PEER_NKEYS, PEER_DHALF), lambda b, e: (0, 0, 0, 0)),
            pl.BlockSpec((eblk, d), lambda b, e: (0, 0)),
            pl.BlockSpec((eblk, d), lambda b, e: (jnp.minimum(e + 1, n_eblk - 1), 0)),
            pl.BlockSpec((d, eblk), lambda b, e: (0, e)),
            pl.BlockSpec((None, 1, d), lambda b, e: (2 * layer + 1, 0, 0)),
            pl.BlockSpec((None, 1, d), lambda b, e: (2 * layer + 1, 0, 0)),
        ],
        out_specs=pl.BlockSpec((t, d), lambda b, e: (b, 0)),
        out_shape=jax.ShapeDtypeStruct((nt, d), F32),
        scratch_shapes=[
            pltpu.VMEM((t, d), BF16),
            pltpu.VMEM((2 * PEER_HEADS, t, PEER_DHALF), BF16),
            pltpu.VMEM(blk_shape, F32), pltpu.VMEM(blk_shape, F32),
            pltpu.VMEM(sel_shape, F32), pltpu.VMEM(sel_shape, F32),
            pltpu.VMEM((eblk, t), F32), pltpu.VMEM((eblk, t), F32),
            pltpu.VMEM((eblk, t), BF16),
            pltpu.VMEM((d, t), F32),
        ],
        compiler_params=_params(("parallel", "arbitrary")),
        name="peer",
    )(x, mod, wq_bf, keys_bf, u_bf, u_bf, vt_bf, ln_g, ln_b)


def _rope_angles(pos, dim):
    inv = ROPE_BASE ** (-jnp.arange(0, dim, 2, dtype=F32) / dim)
    ang = pos.astype(F32)[:, None] * inv[None, :]
    return jnp.cos(ang), jnp.sin(ang)


def _axial_tables(seq_len):
    t = jnp.arange(seq_len)
    half = ATT_DH // 2
    cr, sr = _rope_angles(t // GRID_W, half)
    cc, sc = _rope_angles(t % GRID_W, half)
    cos = jnp.concatenate([cr, cr, cc, cc], -1)
    sin = jnp.concatenate([-sr, sr, -sc, sc], -1)
    reps = LANES // ATT_DH
    return jnp.tile(cos, (1, reps)), jnp.tile(sin, (1, reps))


def kernel(x_prompt, x_sample, state_ret_fwd, state_ret_bwd, cache_k, cache_v, c, c_ctx, mod_w, mod_b, ln_g, ln_b,
           ret_w_in, ret_w_out, ret_decay, attn_w_in, attn_w_out, attn_sink, peer_wq, peer_keys, peer_u, peer_v):
    nb_p, len_p, d = x_prompt.shape
    nb_s, len_s, _ = x_sample.shape
    n_p = nb_p * len_p
    n_s = nb_s * len_s
    for tile in (PROJ_TM, OUT_TM, PEER_T):
        assert n_p % tile == 0 and len_s % tile == 0

    def grp(blk, tile):
        return _group_index(blk, n_p // tile, len_s // tile)

    x = jnp.concatenate([x_prompt.reshape(n_p, d), x_sample.reshape(n_s, d)], axis=0)
    n_grp = 1 + nb_s
    n_grp_pad = -(-n_grp // 8) * 8
    cond = jnp.concatenate([c_ctx[None, :], c, jnp.zeros((n_grp_pad - n_grp, d), F32)], axis=0)
    mod = _modulation(cond, mod_w, mod_b).reshape(DEPTH, n_grp_pad, 6, d)

    ln_g3 = ln_g.reshape(DEPTH * 2, 1, d)
    ln_b3 = ln_b.reshape(DEPTH * 2, 1, d)
    ret_rope = _rope_angles(jnp.arange(len_s), RET_DK)
    att_rope = _axial_tables(len_s)
    kv_w = ATT_HKV * ATT_DH
    cache_k4 = cache_k.reshape(cache_k.shape[0], cache_k.shape[1], cache_k.shape[2], kv_w)
    cache_v4 = cache_v.reshape(cache_v.shape[0], cache_v.shape[1], cache_v.shape[2], kv_w)

    new_sf, new_sb, new_k, new_v = [], [], [], []
    for i in range(DEPTH):
        j = i // N_MIXERS
        if i % N_MIXERS == 0:
            z = _project(x, mod, ret_w_in[j].astype(BF16), i, 0, grp)
            tables = _ret_tables(ret_decay[j])
            o_p, sf, sb = _retention(z, tables, 0, nb_p, len_p, False)
            o_s = _retention(z, tables, n_p, nb_s, len_s, True, rope=ret_rope,
                             states=(state_ret_fwd, state_ret_bwd), j=j)
            new_sf.append(sf)
            new_sb.append(sb)
            x = _ret_out(o_p, o_s, z, x, mod, ret_w_out[j].astype(BF16), ln_g3, ln_b3, i, grp)
        else:
            z = _project(x, mod, attn_w_in[j].astype(BF16), i, 0, grp)
            sink = attn_sink[j].astype(F32)
            o_p = _attn_context(z, sink, nb_p, len_p)
            o_s = _attn_latent(z, sink, cache_k4, cache_v4, att_rope, n_p, nb_s, len_s, j)
            hq_w = ATT_HQ * ATT_DH
            new_k.append(z[:n_p, hq_w:hq_w + kv_w].reshape(nb_p, len_p, ATT_HKV, ATT_DH))
            new_v.append(z[:n_p, hq_w + kv_w:hq_w + 2 * kv_w].reshape(nb_p, len_p, ATT_HKV, ATT_DH))
            x = _attn_out(o_p, o_s, x, mod, attn_w_out[j].astype(BF16), ln_g3, ln_b3, i, grp)
        x = _peer(x, mod, peer_wq[i].astype(BF16), peer_keys[i].astype(BF16), peer_u[i].astype(BF16),
                  peer_v[i].T.astype(BF16), ln_g3, ln_b3, i, grp)

    y_p = x[:n_p].reshape(nb_p, len_p, d)
    y_s = x[n_p:].reshape(nb_s, len_s, d)
    return (y_p, y_s, jnp.stack(new_sf, 1), jnp.stack(new_sb, 1), jnp.stack(new_k, 1), jnp.stack(new_v, 1))
```

```python
import functools
import math

import jax
import jax.numpy as jnp
from jax import lax
from jax.experimental import pallas as pl
from jax.experimental.pallas import tpu as pltpu

F32 = jnp.float32
BF16 = jnp.bfloat16

DEPTH = 4
N_MIXERS = 2
GRID_W = 64
RET_HEADS = 4
RET_DK = 256
RET_DV = 512
RET_CHUNK = 128
RET_SUB = 8
ATT_HQ = 16
ATT_HKV = 4
ATT_G = ATT_HQ // ATT_HKV
ATT_DH = 64
ATT_BLK = 128
ROPE_BASE = 10000.0
PEER_HEADS = 8
PEER_NKEYS = 128
PEER_DHALF = 128
PEER_TOPK = 16
DEEPNORM_ALPHA = (2.0 * DEPTH) ** 0.25
LN_EPS = 1e-5
GN_EPS = 1e-5
NEG_INF = -1e30
GELU_C = math.sqrt(2.0 / math.pi)

LANES = 128
VMEM_LIMIT = 56 * 1024 * 1024
PROJ_TM = 512
PROJ_TN = 3072
OUT_TM = 512
PEER_T = 512
PEER_NI1 = 8
PEER_ILG = (4, 2, 2)
PEER_NSUB = 2


def _dot(a, b):
    return jnp.dot(a, b, preferred_element_type=F32)


def _dot_nt(a, b):
    return lax.dot_general(a, b, (((1,), (1,)), ((), ())), preferred_element_type=F32)


def _params(sem, flags=None):
    return pltpu.CompilerParams(dimension_semantics=sem, vmem_limit_bytes=VMEM_LIMIT, flags=flags)


def _group_index(blk, n_prompt_blocks, blocks_per_seq):
    return jnp.where(blk < n_prompt_blocks, 0, 1 + (blk - n_prompt_blocks) // blocks_per_seq)


def _layer_norm(r, g, b):
    mu = jnp.mean(r, -1, keepdims=True)
    d = r - mu
    var = jnp.mean(d * d, -1, keepdims=True)
    return d * lax.rsqrt(var + LN_EPS) * g + b


def _mod_kernel(c_ref, w_ref, b_ref, o_ref):
    c = c_ref[...]
    a = (c * jax.nn.sigmoid(c)).astype(BF16)
    o_ref[...] = _dot(a, w_ref[...].astype(BF16)) + b_ref[...]


def _modulation(cond, mod_w, mod_b):
    ngp, d = cond.shape
    n_out = mod_w.shape[-1]
    tn = 1536
    return pl.pallas_call(
        _mod_kernel,
        grid=(DEPTH, n_out // tn),
        in_specs=[
            pl.BlockSpec((ngp, d), lambda i, n: (0, 0)),
            pl.BlockSpec((None, d, tn), lambda i, n: (i, 0, n)),
            pl.BlockSpec((None, 1, tn), lambda i, n: (i, 0, n)),
        ],
        out_specs=pl.BlockSpec((None, ngp, tn), lambda i, n: (i, 0, n)),
        out_shape=jax.ShapeDtypeStruct((DEPTH, ngp, n_out), F32),
        compiler_params=_params(("parallel", "parallel")),
        name="modulation",
    )(cond, mod_w, mod_b.reshape(DEPTH, 1, n_out))


def _proj_kernel(x_ref, m_ref, w_ref, o_ref, *, off):
    sh = m_ref[off:off + 1, :]
    sc = m_ref[off + 1:off + 2, :]
    h = (x_ref[...] * (1.0 + sc) + sh).astype(BF16)
    o_ref[...] = _dot(h, w_ref[...])


def _project(x, mod, w_bf, layer, off, grp):
    nt, d = x.shape
    n = w_bf.shape[1]
    tn = PROJ_TN if n % PROJ_TN == 0 else n
    tm = PROJ_TM
    return pl.pallas_call(
        functools.partial(_proj_kernel, off=off),
        grid=(n // tn, nt // tm),
        in_specs=[
            pl.BlockSpec((tm, d), lambda j, b: (b, 0)),
            pl.BlockSpec((None, None, 6, d), lambda j, b: (layer, grp(b, tm), 0, 0)),
            pl.BlockSpec((d, tn), lambda j, b: (0, j)),
        ],
        out_specs=pl.BlockSpec((tm, tn), lambda j, b: (b, j)),
        out_shape=jax.ShapeDtypeStruct((nt, n), F32),
        compiler_params=_params(("parallel", "parallel")),
        name="mod_proj",
    )(x, mod, w_bf)


def _rope_half(x, cos, sin):
    x1 = x[:, :LANES]
    x2 = x[:, LANES:]
    return jnp.concatenate([x1 * cos - x2 * sin, x1 * sin + x2 * cos], axis=1)


def _ret_kernel(*refs, latent, nc, n_sub):
    it = iter(refs)
    qa, ka, va, qb, kb, vb = (next(it) for _ in range(6))
    if latent:
        cosa, sina, cosb, sinb = (next(it) for _ in range(4))
    dfb, qdf, qdb, kdf, kdb, cdf, cdb = (next(it) for _ in range(7))
    if latent:
        s0f, s0b = next(it), next(it)
    o_ref = next(it)
    if not latent:
        sf_out, sb_out = next(it), next(it)
    sf, sb = next(it), next(it)

    c = pl.program_id(2)

    @pl.when(c == 0)
    def _init():
        o_ref[...] = jnp.zeros(o_ref.shape, F32)
        if latent:
            sf[...] = s0f[...]
            sb[...] = s0b[...]
        else:
            sf[...] = jnp.zeros(sf.shape, F32)
            sb[...] = jnp.zeros(sb.shape, F32)

    def prep(q, k, cos_ref, sin_ref, rows):
        k = k * (RET_DK ** -0.5)
        if latent:
            cos = cos_ref[rows, :]
            sin = sin_ref[rows, :]
            q = _rope_half(q, cos, sin)
            k = _rope_half(k, cos, sin)
        return q, k

    for sub in range(n_sub):
        ca = c * n_sub + sub
        ra = slice(sub * RET_CHUNK, (sub + 1) * RET_CHUNK)
        rb = slice((n_sub - 1 - sub) * RET_CHUNK, (n_sub - sub) * RET_CHUNK)

        q, k = prep(qa[ra, :], ka[ra, :], cosa if latent else None, sina if latent else None, ra)
        v = va[ra, :].astype(BF16)
        att = (_dot_nt(q.astype(BF16), k.astype(BF16)) * dfb[...]).astype(BF16)
        o_c = _dot(att, v) + _dot((q * qdf[...]).astype(BF16), sf[...].astype(BF16))
        rows_c = pl.ds(pl.multiple_of(ca * RET_CHUNK, RET_CHUNK), RET_CHUNK)
        o_ref[rows_c, :] += o_c
        kd_t = (k * kdf[...]).T.astype(BF16)
        sf[...] = sf[...] * cdf[...] + _dot(kd_t, v)

        q2, k2 = prep(qb[rb, :], kb[rb, :], cosb if latent else None, sinb if latent else None, rb)
        v2 = vb[rb, :].astype(BF16)
        o_b = _dot((q2 * qdb[...]).astype(BF16), sb[...].astype(BF16))
        rows_b = pl.ds(pl.multiple_of((nc - 1 - ca) * RET_CHUNK, RET_CHUNK), RET_CHUNK)
        o_ref[rows_b, :] += o_b
        kd2_t = (k2 * kdb[...]).T.astype(BF16)
        sb[...] = sb[...] * cdb[...] + _dot(kd2_t, v2)

    if not latent:
        @pl.when(c == nc // n_sub - 1)
        def _fin():
            sf_out[...] = sf[...]
            sb_out[...] = sb[...]


def _ret_tables(decay):
    log_g = jax.nn.log_sigmoid(decay.astype(F32))
    lf, lb = log_g[0], log_g[1]
    idx = jnp.arange(RET_CHUNK, dtype=F32)
    diff = idx[:, None] - idx[None, :]
    fmask = diff >= 0
    bmask = diff < 0
    df = jnp.where(fmask[None], jnp.exp(jnp.where(fmask, diff, 0.0)[None] * lf[:, None, None]), 0.0)
    db = jnp.where(bmask[None], jnp.exp(jnp.where(bmask, -diff, 0.0)[None] * lb[:, None, None]), 0.0)
    dfb = df + db
    def rows(e):
        return jnp.broadcast_to(e[:, :, None], (RET_HEADS, RET_CHUNK, RET_DK))
    qdf = rows(jnp.exp((idx + 1.0)[None, :] * lf[:, None]))
    kdf = rows(jnp.exp((RET_CHUNK - 1.0 - idx)[None, :] * lf[:, None]))
    qdb = rows(jnp.exp((RET_CHUNK - idx)[None, :] * lb[:, None]))
    kdb = rows(jnp.exp(idx[None, :] * lb[:, None]))
    cdf = jnp.broadcast_to(jnp.exp(RET_CHUNK * lf)[:, None, None], (RET_HEADS, 1, RET_DV))
    cdb = jnp.broadcast_to(jnp.exp(RET_CHUNK * lb)[:, None, None], (RET_HEADS, 1, RET_DV))
    return dfb, qdf, qdb, kdf, kdb, cdf, cdb


def _retention(z, tables, row0, nseq, seq_len, latent, rope=None, states=None, j=0):
    nc = seq_len // RET_CHUNK
    n_sub = math.gcd(RET_SUB, nc)
    blk = n_sub * RET_CHUNK
    ns = nc // n_sub
    base = row0 // blk
    h_ = RET_HEADS
    kcol = (h_ * RET_DK) // RET_DK
    vcol = (2 * h_ * RET_DK) // RET_DV

    def fw(b, h, c):
        return base + b * ns + c

    def bw(b, h, c):
        return base + b * ns + (ns - 1 - c)

    in_specs = [
        pl.BlockSpec((blk, RET_DK), lambda b, h, c: (fw(b, h, c), h)),
        pl.BlockSpec((blk, RET_DK), lambda b, h, c: (fw(b, h, c), kcol + h)),
        pl.BlockSpec((blk, RET_DV), lambda b, h, c: (fw(b, h, c), vcol + h)),
        pl.BlockSpec((blk, RET_DK), lambda b, h, c: (bw(b, h, c), h)),
        pl.BlockSpec((blk, RET_DK), lambda b, h, c: (bw(b, h, c), kcol + h)),
        pl.BlockSpec((blk, RET_DV), lambda b, h, c: (bw(b, h, c), vcol + h)),
    ]
    args = [z, z, z, z, z, z]
    if latent:
        cos, sin = rope
        in_specs += [
            pl.BlockSpec((blk, LANES), lambda b, h, c: (c, 0)),
            pl.BlockSpec((blk, LANES), lambda b, h, c: (c, 0)),
            pl.BlockSpec((blk, LANES), lambda b, h, c: (ns - 1 - c, 0)),
            pl.BlockSpec((blk, LANES), lambda b, h, c: (ns - 1 - c, 0)),
        ]
        args += [cos, sin, cos, sin]
    dfb, qdf, qdb, kdf, kdb, cdf, cdb = tables
    in_specs += [pl.BlockSpec((None, RET_CHUNK, RET_CHUNK), lambda b, h, c: (h, 0, 0))]
    in_specs += [pl.BlockSpec((None, RET_CHUNK, RET_DK), lambda b, h, c: (h, 0, 0))] * 4
    in_specs += [pl.BlockSpec((None, 1, RET_DV), lambda b, h, c: (h, 0, 0))] * 2
    args += [dfb, qdf, qdb, kdf, kdb, cdf, cdb]
    if latent:
        s0f, s0b = states
        st_spec = pl.BlockSpec((None, None, None, RET_DK, RET_DV), lambda b, h, c: (b, j, h, 0, 0))
        in_specs += [st_spec, st_spec]
        args += [s0f, s0b]

    o_shape = jax.ShapeDtypeStruct((nseq * seq_len, h_ * RET_DV), F32)
    o_spec = pl.BlockSpec((seq_len, RET_DV), lambda b, h, c: (b, h))
    if latent:
        out_shape, out_specs = o_shape, o_spec
    else:
        s_shape = jax.ShapeDtypeStruct((nseq, h_, RET_DK, RET_DV), F32)
        s_spec = pl.BlockSpec((None, None, RET_DK, RET_DV), lambda b, h, c: (b, h, 0, 0))
        out_shape, out_specs = (o_shape, s_shape, s_shape), (o_spec, s_spec, s_spec)

    return pl.pallas_call(
        functools.partial(_ret_kernel, latent=latent, nc=nc, n_sub=n_sub),
        grid=(nseq, h_, ns),
        in_specs=in_specs,
        out_specs=out_specs,
        out_shape=out_shape,
        scratch_shapes=[pltpu.VMEM((RET_DK, RET_DV), F32), pltpu.VMEM((RET_DK, RET_DV), F32)],
        compiler_params=_params(("parallel", "parallel", "arbitrary")),
        name="retention_latent" if latent else "retention_context",
    )(*args)


def _ret_out_kernel(op_ref, os_ref, g_ref, x_ref, m_ref, w_ref, lg_ref, lb_ref, out_ref, *, n_ctx_tiles):
    is_ctx = pl.program_id(0) < n_ctx_tiles
    y = None
    for h in range(RET_HEADS):
        cols = slice(h * RET_DV, (h + 1) * RET_DV)
        oh = jnp.where(is_ctx, op_ref[:, cols], os_ref[:, cols])
        mu = jnp.mean(oh, -1, keepdims=True)
        d = oh - mu
        var = jnp.mean(d * d, -1, keepdims=True)
        on = d * lax.rsqrt(var + GN_EPS)
        gh = g_ref[:, cols]
        act = (gh * jax.nn.sigmoid(gh) * on).astype(BF16)
        part = _dot(act, w_ref[cols, :])
        y = part if y is None else y + part
    gate = m_ref[2:3, :]
    r = DEEPNORM_ALPHA * x_ref[...] + gate * y
    out_ref[...] = _layer_norm(r, lg_ref[...], lb_ref[...])


def _ret_out(o_ctx, o_lat, z, x, mod, w_bf, ln_g, ln_b, layer, grp):
    nt, d = x.shape
    tm = OUT_TM
    hv = RET_HEADS * RET_DV
    gcol = (2 * RET_HEADS * RET_DK + hv) // hv
    n_ctx = o_ctx.shape[0] // tm
    return pl.pallas_call(
        functools.partial(_ret_out_kernel, n_ctx_tiles=n_ctx),
        grid=(nt // tm,),
        in_specs=[
            pl.BlockSpec((tm, hv), lambda b: (jnp.minimum(b, n_ctx - 1), 0)),
            pl.BlockSpec((tm, hv), lambda b: (jnp.maximum(b - n_ctx, 0), 0)),
            pl.BlockSpec((tm, hv), lambda b: (b, gcol)),
            pl.BlockSpec((tm, d), lambda b: (b, 0)),
            pl.BlockSpec((None, None, 6, d), lambda b: (layer, grp(b, tm), 0, 0)),
            pl.BlockSpec((hv, d), lambda b: (0, 0)),
            pl.BlockSpec((None, 1, d), lambda b: (2 * layer, 0, 0)),
            pl.BlockSpec((None, 1, d), lambda b: (2 * layer, 0, 0)),
        ],
        out_specs=pl.BlockSpec((tm, d), lambda b: (b, 0)),
        out_shape=jax.ShapeDtypeStruct((nt, d), F32),
        compiler_params=_params(("parallel",)),
        name="retention_out",
    )(o_ctx, o_lat, z, x, mod, w_bf, ln_g, ln_b)


def _softmax_pv(parts, sink):
    m = sink
    for s, _ in parts:
        m = jnp.maximum(m, jnp.max(s, -1, keepdims=True))
    den = jnp.exp(sink - m)
    o = None
    for s, v in parts:
        p = jnp.exp(s - m)
        den = den + jnp.sum(p, -1, keepdims=True)
        pv = _dot(p.astype(BF16), v)
        o = pv if o is None else o + pv
    return o / den


def _attn_ctx_kernel(sink_ref, q_ref, k_ref, v_ref, o_ref):
    scale = ATT_DH ** -0.5
    n = q_ref.shape[0]
    gidx = lax.broadcasted_iota(jnp.int32, (ATT_G * n, 1), 0) // n
    for h in range(ATT_HKV):
        kh = k_ref[:, h * ATT_DH:(h + 1) * ATT_DH].astype(BF16)
        vh = v_ref[:, h * ATT_DH:(h + 1) * ATT_DH].astype(BF16)
        heads = [h * ATT_G + g for g in range(ATT_G)]
        qg = jnp.concatenate([q_ref[:, hq * ATT_DH:(hq + 1) * ATT_DH] for hq in heads], axis=0)
        qg = (qg * scale).astype(BF16)
        sink = jnp.zeros((ATT_G * n, 1), F32)
        for g, hq in enumerate(heads):
            sink = jnp.where(gidx == g, sink_ref[hq], sink)
        o = _softmax_pv([(_dot_nt(qg, kh), vh)], sink)
        for g, hq in enumerate(heads):
            o_ref[:, hq * ATT_DH:(hq + 1) * ATT_DH] = o[g * n:(g + 1) * n, :]


def _attn_context(z, sink, nseq, seq_len):
    hq_w = ATT_HQ * ATT_DH
    kv_w = ATT_HKV * ATT_DH
    return pl.pallas_call(
        _attn_ctx_kernel,
        grid=(nseq,),
        in_specs=[
            pl.BlockSpec(memory_space=pltpu.SMEM),
            pl.BlockSpec((seq_len, hq_w), lambda b: (b, 0)),
            pl.BlockSpec((seq_len, kv_w), lambda b: (b, hq_w // kv_w)),
            pl.BlockSpec((seq_len, kv_w), lambda b: (b, hq_w // kv_w + 1)),
        ],
        out_specs=pl.BlockSpec((seq_len, hq_w), lambda b: (b, 0)),
        out_shape=jax.ShapeDtypeStruct((nseq * seq_len, hq_w), F32),
        compiler_params=_params(("parallel",)),
        name="attention_context",
    )(sink, z, z, z)


def _rope_axial(x, cos, sin):
    w = x.shape[1]
    lane = lax.broadcasted_iota(jnp.int32, (x.shape[0], LANES), 1)
    first = (lane % 32) < 16
    outs = []
    for cg in range(w // LANES):
        xg = x[:, cg * LANES:(cg + 1) * LANES]
        up = pltpu.roll(xg, LANES - 16, 1)
        dn = pltpu.roll(xg, 16, 1)
        outs.append(xg * cos + jnp.where(first, up, dn) * sin)
    return jnp.concatenate(outs, axis=1) if len(outs) > 1 else outs[0]


def _attn_lat_kernel(sink_ref, q_ref, kp_ref, kc_ref, kn_ref, vp_ref, vc_ref, vn_ref, ck_ref, cv_ref,
                     cq_ref, sq_ref, cp_ref, sp_ref, cn_ref, sn_ref, o_ref, *, nb):
    scale = ATT_DH ** -0.5
    qb = pl.program_id(1)
    q = _rope_axial(q_ref[...], cq_ref[...], sq_ref[...]) * scale
    kw = jnp.concatenate([
        _rope_axial(kp_ref[...], cp_ref[...], sp_ref[...]),
        _rope_axial(kc_ref[...], cq_ref[...], sq_ref[...]),
        _rope_axial(kn_ref[...], cn_ref[...], sn_ref[...]),
    ], axis=0)
    vw = jnp.concatenate([vp_ref[...], vc_ref[...], vn_ref[...]], axis=0)
    ck = ck_ref[...]
    cv = cv_ref[...]

    rows = ATT_G * ATT_BLK
    i = lax.broadcasted_iota(jnp.int32, (rows, 3 * ATT_BLK), 0) % ATT_BLK
    jj = lax.broadcasted_iota(jnp.int32, (rows, 3 * ATT_BLK), 1)
    lo = jnp.where(qb > 0, 0, ATT_BLK)
    hi = jnp.where(qb < nb - 1, 3 * ATT_BLK, 2 * ATT_BLK)
    valid = (jj >= jnp.maximum(i, lo)) & (jj < jnp.minimum(i + 2 * ATT_BLK + 1, hi))
    gidx = lax.broadcasted_iota(jnp.int32, (rows, 1), 0) // ATT_BLK

    for h in range(ATT_HKV):
        hs = slice(h * ATT_DH, (h + 1) * ATT_DH)
        kh = kw[:, hs].astype(BF16)
        vh = vw[:, hs].astype(BF16)
        ckh = ck[:, hs].astype(BF16)
        cvh = cv[:, hs].astype(BF16)
        heads = [h * ATT_G + g for g in range(ATT_G)]
        qg = jnp.concatenate([q[:, hq * ATT_DH:(hq + 1) * ATT_DH] for hq in heads], axis=0).astype(BF16)
        sink = jnp.zeros((rows, 1), F32)
        for g, hq in enumerate(heads):
            sink = jnp.where(gidx == g, sink_ref[hq], sink)
        s_loc = jnp.where(valid, _dot_nt(qg, kh), NEG_INF)
        s_ctx = _dot_nt(qg, ckh)
        o = _softmax_pv([(s_loc, vh), (s_ctx, cvh)], sink)
        for g, hq in enumerate(heads):
            o_ref[:, hq * ATT_DH:(hq + 1) * ATT_DH] = o[g * ATT_BLK:(g + 1) * ATT_BLK, :]


def _attn_latent(z, sink, cache_k, cache_v, rope, row0, nseq, seq_len, j):
    nb = seq_len // ATT_BLK
    base = row0 // ATT_BLK
    hq_w = ATT_HQ * ATT_DH
    kv_w = ATT_HKV * ATT_DH
    kcol = hq_w // kv_w
    past = cache_k.shape[2]
    cos, sin = rope

    def prv(q):
        return jnp.maximum(q - 1, 0)

    def nxt(q):
        return jnp.minimum(q + 1, nb - 1)

    def kv_spec(sel, col):
        return pl.BlockSpec((ATT_BLK, kv_w), lambda b, q: (base + b * nb + sel(q), col))

    def rope_spec(sel):
        return pl.BlockSpec((ATT_BLK, LANES), lambda b, q: (sel(q), 0))

    same = lambda q: q
    cache_spec = pl.BlockSpec((None, None, past, kv_w), lambda b, q: (b, j, 0, 0))
    return pl.pallas_call(
        functools.partial(_attn_lat_kernel, nb=nb),
        grid=(nseq, nb),
        in_specs=[
            pl.BlockSpec(memory_space=pltpu.SMEM),
            pl.BlockSpec((ATT_BLK, hq_w), lambda b, q: (base + b * nb + q, 0)),
            kv_spec(prv, kcol), kv_spec(same, kcol), kv_spec(nxt, kcol),
            kv_spec(prv, kcol + 1), kv_spec(same, kcol + 1), kv_spec(nxt, kcol + 1),
            cache_spec, cache_spec,
            rope_spec(same), rope_spec(same), rope_spec(prv), rope_spec(prv), rope_spec(nxt), rope_spec(nxt),
        ],
        out_specs=pl.BlockSpec((ATT_BLK, hq_w), lambda b, q: (b * nb + q, 0)),
        out_shape=jax.ShapeDtypeStruct((nseq * seq_len, hq_w), F32),
        compiler_params=_params(("parallel", "parallel")),
        name="attention_latent",
    )(sink, z, z, z, z, z, z, z, cache_k, cache_v, cos, sin, cos, sin, cos, sin)


def _attn_out_kernel(op_ref, os_ref, x_ref, m_ref, w_ref, lg_ref, lb_ref, out_ref, *, n_ctx_tiles):
    is_ctx = pl.program_id(0) < n_ctx_tiles
    o = jnp.where(is_ctx, op_ref[...], os_ref[...])
    y = _dot(o.astype(BF16), w_ref[...])
    gate = m_ref[2:3, :]
    r = DEEPNORM_ALPHA * x_ref[...] + gate * y
    out_ref[...] = _layer_norm(r, lg_ref[...], lb_ref[...])


def _attn_out(o_ctx, o_lat, x, mod, w_bf, ln_g, ln_b, layer, grp):
    nt, d = x.shape
    tm = OUT_TM
    k = o_ctx.shape[1]
    n_ctx = o_ctx.shape[0] // tm
    return pl.pallas_call(
        functools.partial(_attn_out_kernel, n_ctx_tiles=n_ctx),
        grid=(nt // tm,),
        in_specs=[
            pl.BlockSpec((tm, k), lambda b: (jnp.minimum(b, n_ctx - 1), 0)),
            pl.BlockSpec((tm, k), lambda b: (jnp.maximum(b - n_ctx, 0), 0)),
            pl.BlockSpec((tm, d), lambda b: (b, 0)),
            pl.BlockSpec((None, None, 6, d), lambda b: (layer, grp(b, tm), 0, 0)),
            pl.BlockSpec((k, d), lambda b: (0, 0)),
            pl.BlockSpec((None, 1, d), lambda b: (2 * layer, 0, 0)),
            pl.BlockSpec((None, 1, d), lambda b: (2 * layer, 0, 0)),
        ],
        out_specs=pl.BlockSpec((tm, d), lambda b: (b, 0)),
        out_shape=jax.ShapeDtypeStruct((nt, d), F32),
        compiler_params=_params(("parallel",)),
        name="attention_out",
    )(o_ctx, o_lat, x, mod, w_bf, ln_g, ln_b)


def _sort_network(n):
    pairs = []

    def merge(lo, hi, r):
        step = r * 2
        if step < hi - lo:
            merge(lo, hi, step)
            merge(lo + r, hi, step)
            for i in range(lo + r, hi - r, step):
                pairs.append((i, i + r))
        else:
            pairs.append((lo, lo + r))

    def sort(lo, hi):
        if hi - lo >= 1:
            mid = lo + (hi - lo) // 2
            sort(lo, mid)
            sort(mid + 1, hi)
            merge(lo, hi, 1)

    sort(0, n - 1)
    return pairs


def _pop_top(lists, k):
    vals = []
    for r in range(k):
        head = lists[0]
        m = jnp.max(head, axis=0, keepdims=True)
        vals.append(m)
        left = k - 1 - r
        if left > 0:
            took = head >= m
            nxt = [lists[i + 1] if i + 1 < len(lists) else NEG_INF for i in range(left)]
            lists = [jnp.where(took, nxt[i], lists[i]) for i in range(left)]
    return vals


def _top_values(s, k):
    tiles = [s[8 * j:8 * (j + 1), :] for j in range(s.shape[0] // 8)]
    for i, j in _sort_network(len(tiles)):
        tiles[i], tiles[j] = jnp.maximum(tiles[i], tiles[j]), jnp.minimum(tiles[i], tiles[j])
    return _pop_top(tiles, k)


def _peer_kernel(x_ref, m_ref, wq_ref, keys_ref, u0_ref, u_ref, vt_ref, lg_ref, lb_ref, *rest, n_eblk, n_ctx_tiles):
    n_out = 1 if n_ctx_tiles is None else 2
    outs = rest[:n_out]
    h_scr, q_scr, d1_scr, e1_scr, s2_scr, e2_scr, at0_scr, at1_scr, g_scr, acc_scr = rest[n_out:]
    t = x_ref.shape[0]
    nlt = t // LANES
    e = pl.program_id(1)
    k = PEER_TOPK
    nk = PEER_NKEYS
    hk = nk // PEER_NSUB

    @pl.when(e == 0)
    def _scores():
        sh = m_ref[3:4, :]
        sc = m_ref[4:5, :]
        h_scr[...] = (x_ref[...] * (1.0 + sc) + sh).astype(BF16)
        q = _dot(h_scr[...], wq_ref[...])
        for ph in range(2 * PEER_HEADS):
            q_scr[ph] = q[:, ph * PEER_DHALF:(ph + 1) * PEER_DHALF].astype(BF16)
        acc_scr[...] = jnp.zeros(acc_scr.shape, F32)
        at0_scr[...] = _dot_nt(u0_ref[...], h_scr[...])

        def unit(uidx, carry):
            lt = uidx // PEER_HEADS
            p = uidx % PEER_HEADS
            t0 = pl.multiple_of(lt * LANES, LANES)
            s, vals = [], []
            for hh in range(2):
                sk = _dot_nt(keys_ref[p, hh], q_scr[2 * p + hh, pl.ds(t0, LANES), :])
                s.append(sk)
                vals.append(_top_values(sk, k))
            v1m = [v - vals[0][0] for v in vals[0]]
            v2m = [v - vals[1][0] for v in vals[1]]
            row = lax.broadcasted_iota(jnp.int32, (8, LANES), 0)
            base = jnp.zeros((8, LANES), F32)
            for r in range(4):
                base = jnp.where(row == r, v1m[r], base)
                base = jnp.where(row == 4 + r, v2m[r], base)
            cands = [base + jnp.where(row < 4, v2m[i], v1m[i + 4] if i + 4 < k else NEG_INF) for i in range(k)]
            tops = _pop_top(cands, k + 1)
            thr = 0.5 * (tops[k - 1] + tops[k])
            zsum = jnp.exp(tops[0])
            for r in range(1, k):
                zsum = zsum + jnp.exp(tops[r])
            s1m = jnp.where(s[0] >= vals[0][k - 1], s[0] - vals[0][0], NEG_INF)
            s2m = jnp.where(s[1] >= vals[1][k - 1], s[1] - vals[1][0], NEG_INF)
            d1 = thr - s1m
            e1 = jnp.exp(s1m) * (0.5 / zsum)
            for j in range(nk // PEER_NI1):
                d1_scr[j, p, lt] = d1[j * PEER_NI1:(j + 1) * PEER_NI1, :]
                e1_scr[j, p, lt] = e1[j * PEER_NI1:(j + 1) * PEER_NI1, :]
            s2_scr[p, lt] = s2m
            e2_scr[p, lt] = jnp.exp(s2m)
            return carry

        lax.fori_loop(0, nlt * PEER_HEADS, unit, 0, unroll=8)

    def step(at_r, at_w):
        mxu_w = 2 * LANES
        for ts_i in range(t // mxu_w):
            ts = slice(ts_i * mxu_w, (ts_i + 1) * mxu_w)
            at_w[:, ts] = _dot_nt(u_ref[...], h_scr[ts, :])
            for ig in range(len(PEER_ILG)):
                il_lo = sum(PEER_ILG[:ig])
                il_hi = il_lo + PEER_ILG[ig]
                for lt in range(ts_i * mxu_w // LANES, (ts_i + 1) * mxu_w // LANES):
                    tl = slice(lt * LANES, (lt + 1) * LANES)
                    for hf in range(PEER_NSUB):
                        rows2 = slice(hf * hk, (hf + 1) * hk)
                        ils = range(il_lo, il_hi)
                        w = {il: jnp.zeros((hk, LANES), F32) for il in ils}
                        for p in range(PEER_HEADS):
                            s2h = s2_scr[p, lt, rows2, :]
                            e2h = e2_scr[p, lt, rows2, :]
                            for il in ils:
                                d1 = d1_scr[e, p, lt, il:il + 1, :]
                                e1 = e1_scr[e, p, lt, il:il + 1, :]
                                w[il] = w[il] + jnp.where(s2h >= d1, e2h, 0.0) * e1
                        for il in ils:
                            rows = slice(il * nk + hf * hk, il * nk + (hf + 1) * hk)
                            a = at_r[rows, tl]
                            u = a * (a * a * (GELU_C * 0.044715) + GELU_C)
                            g_scr[rows, tl] = (w[il] * (a + a * jnp.tanh(u))).astype(BF16)
                ks = slice(il_lo * nk, il_hi * nk)
                acc_scr[:, ts] += _dot(vt_ref[:, ks], g_scr[ks, ts])

    @pl.when(e % 2 == 0)
    def _even():
        step(at0_scr, at1_scr)

    @pl.when(e % 2 == 1)
    def _odd():
        step(at1_scr, at0_scr)

    @pl.when(e == n_eblk - 1)
    def _finish():
        y = acc_scr[...].T
        gate = m_ref[5:6, :]
        r = DEEPNORM_ALPHA * x_ref[...] + gate * y
        y_new = _layer_norm(r, lg_ref[...], lb_ref[...])
        if n_ctx_tiles is None:
            outs[0][...] = y_new
        else:
            @pl.when(pl.program_id(0) < n_ctx_tiles)
            def _ctx():
                outs[0][...] = y_new

            @pl.when(pl.program_id(0) >= n_ctx_tiles)
            def _lat():
                outs[1][...] = y_new


def _peer(x, mod, wq_bf, keys_bf, u_bf, vt_bf, ln_g, ln_b, layer, grp, n_ctx=None):
    nt, d = x.shape
    t = PEER_T
    nlt = t // LANES
    eblk = PEER_NI1 * PEER_NKEYS
    n_eblk = u_bf.shape[0] // eblk
    nq = wq_bf.shape[1]
    sel_shape = (PEER_HEADS, nlt, PEER_NKEYS, LANES)
    assert PEER_NI1 % 8 == 0
    blk_shape = (n_eblk, PEER_HEADS, nlt, PEER_NI1, LANES)
    if n_ctx is None:
        nct = None
        out_specs = pl.BlockSpec((t, d), lambda b, e: (b, 0))
        out_shape = jax.ShapeDtypeStruct((nt, d), F32)
    else:
        nct = n_ctx // t
        out_specs = (pl.BlockSpec((t, d), lambda b, e: (jnp.minimum(b, nct - 1), 0)),
                     pl.BlockSpec((t, d), lambda b, e: (jnp.maximum(b - nct, 0), 0)))
        out_shape = (jax.ShapeDtypeStruct((n_ctx, d), F32), jax.ShapeDtypeStruct((nt - n_ctx, d), F32))
    return pl.pallas_call(
        functools.partial(_peer_kernel, n_eblk=n_eblk, n_ctx_tiles=nct),
        grid=(nt // t, n_eblk),
        in_specs=[
            pl.BlockSpec((t, d), lambda b, e: (b, 0)),
            pl.BlockSpec((None, None, 6, d), lambda b, e: (layer, grp(b, t), 0, 0)),
            pl.BlockSpec((d, nq), lambda b, e: (0, 0)),
            pl.BlockSpec((PEER_HEADS, 2, PEER_NKEYS, PEER_DHALF), lambda b, e: (0, 0, 0, 0)),
            pl.BlockSpec((eblk, d), lambda b, e: (0, 0)),
            pl.BlockSpec((eblk, d), lambda b, e: (jnp.minimum(e + 1, n_eblk - 1), 0)),
            pl.BlockSpec((d, eblk), lambda b, e: (0, e)),
            pl.BlockSpec((None, 1, d), lambda b, e: (2 * layer + 1, 0, 0)),
            pl.BlockSpec((None, 1, d), lambda b, e: (2 * layer + 1, 0, 0)),
        ],
        out_specs=out_specs,
        out_shape=out_shape,
        scratch_shapes=[
            pltpu.VMEM((t, d), BF16),
            pltpu.VMEM((2 * PEER_HEADS, t, PEER_DHALF), BF16),
            pltpu.VMEM(blk_shape, F32), pltpu.VMEM(blk_shape, F32),
            pltpu.VMEM(sel_shape, F32), pltpu.VMEM(sel_shape, F32),
            pltpu.VMEM((eblk, t), F32), pltpu.VMEM((eblk, t), F32),
            pltpu.VMEM((eblk, t), BF16),
            pltpu.VMEM((d, t), F32),
        ],
        compiler_params=_params(("parallel", "arbitrary")),
        name="peer",
    )(x, mod, wq_bf, keys_bf, u_bf, u_bf, vt_bf, ln_g, ln_b)


def _rope_angles(pos, dim):
    inv = ROPE_BASE ** (-jnp.arange(0, dim, 2, dtype=F32) / dim)
    ang = pos.astype(F32)[:, None] * inv[None, :]
    return jnp.cos(ang), jnp.sin(ang)


def _axial_tables(seq_len):
    t = jnp.arange(seq_len)
    half = ATT_DH // 2
    cr, sr = _rope_angles(t // GRID_W, half)
    cc, sc = _rope_angles(t % GRID_W, half)
    cos = jnp.concatenate([cr, cr, cc, cc], -1)
    sin = jnp.concatenate([-sr, sr, -sc, sc], -1)
    reps = LANES // ATT_DH
    return jnp.tile(cos, (1, reps)), jnp.tile(sin, (1, reps))


def kernel(x_prompt, x_sample, state_ret_fwd, state_ret_bwd, cache_k, cache_v, c, c_ctx, mod_w, mod_b, ln_g, ln_b,
           ret_w_in, ret_w_out, ret_decay, attn_w_in, attn_w_out, attn_sink, peer_wq, peer_keys, peer_u, peer_v):
    nb_p, len_p, d = x_prompt.shape
    nb_s, len_s, _ = x_sample.shape
    n_p = nb_p * len_p
    n_s = nb_s * len_s
    for tile in (PROJ_TM, OUT_TM, PEER_T):
        assert n_p % tile == 0 and len_s % tile == 0

    def grp(blk, tile):
        return _group_index(blk, n_p // tile, len_s // tile)

    x = jnp.concatenate([x_prompt.reshape(n_p, d), x_sample.reshape(n_s, d)], axis=0)
    n_grp = 1 + nb_s
    n_grp_pad = -(-n_grp // 8) * 8
    cond = jnp.concatenate([c_ctx[None, :], c, jnp.zeros((n_grp_pad - n_grp, d), F32)], axis=0)
    mod = _modulation(cond, mod_w, mod_b).reshape(DEPTH, n_grp_pad, 6, d)

    ln_g3 = ln_g.reshape(DEPTH * 2, 1, d)
    ln_b3 = ln_b.reshape(DEPTH * 2, 1, d)
    ret_rope = _rope_angles(jnp.arange(len_s), RET_DK)
    att_rope = _axial_tables(len_s)
    kv_w = ATT_HKV * ATT_DH
    cache_k4 = cache_k.reshape(cache_k.shape[0], cache_k.shape[1], cache_k.shape[2], kv_w)
    cache_v4 = cache_v.reshape(cache_v.shape[0], cache_v.shape[1], cache_v.shape[2], kv_w)

    new_sf, new_sb, new_k, new_v = [], [], [], []
    for i in range(DEPTH):
        j = i // N_MIXERS
        if i % N_MIXERS == 0:
            z = _project(x, mod, ret_w_in[j].astype(BF16), i, 0, grp)
            tables = _ret_tables(ret_decay[j])
            o_p, sf, sb = _retention(z, tables, 0, nb_p, len_p, False)
            o_s = _retention(z, tables, n_p, nb_s, len_s, True, rope=ret_rope,
                             states=(state_ret_fwd, state_ret_bwd), j=j)
            new_sf.append(sf)
            new_sb.append(sb)
            x = _ret_out(o_p, o_s, z, x, mod, ret_w_out[j].astype(BF16), ln_g3, ln_b3, i, grp)
        else:
            z = _project(x, mod, attn_w_in[j].astype(BF16), i, 0, grp)
            sink = attn_sink[j].astype(F32)
            o_p = _attn_context(z, sink, nb_p, len_p)
            o_s = _attn_latent(z, sink, cache_k4, cache_v4, att_rope, n_p, nb_s, len_s, j)
            hq_w = ATT_HQ * ATT_DH
            new_k.append(z[:n_p, hq_w:hq_w + kv_w].reshape(nb_p, len_p, ATT_HKV, ATT_DH))
            new_v.append(z[:n_p, hq_w + kv_w:hq_w + 2 * kv_w].reshape(nb_p, len_p, ATT_HKV, ATT_DH))
            x = _attn_out(o_p, o_s, x, mod, attn_w_out[j].astype(BF16), ln_g3, ln_b3, i, grp)
        x = _peer(x, mod, peer_wq[i].astype(BF16), peer_keys[i].astype(BF16), peer_u[i].astype(BF16),
                  peer_v[i].T.astype(BF16), ln_g3, ln_b3, i, grp, n_ctx=n_p if i == DEPTH - 1 else None)

    y_p = x[0].reshape(nb_p, len_p, d)
    y_s = x[1].reshape(nb_s, len_s, d)
    return (y_p, y_s, jnp.stack(new_sf, 1), jnp.stack(new_sb, 1), jnp.stack(new_k, 1), jnp.stack(new_v, 1))
```
